```python
import math
import jax, jax.numpy as jnp
from jax import lax
import numpy as np

D_MODEL = 1024
BATCH = 2
SEQ = 8192
DEPTH = 2
DEC_BATCH = 32
DEC_SEQ = 8
PAST_LEN = 8192
PAGE_SIZE = 128

N_A_LAYERS = DEPTH // 2
N_B_LAYERS = DEPTH - N_A_LAYERS
D_RNN = D_MODEL
N_RG_BLOCKS = 8
RG_BLOCK = D_RNN // N_RG_BLOCKS
CONV_W = 4
RG_C = 8.0
N_HEADS = 16
HEAD_DIM = D_MODEL // N_HEADS
N_KV = 4
HPG = N_HEADS // N_KV
CMP_BLOCK = 64
N_SEL = 16
WINDOW = 512
D_PHI = 128
Q_BLOCK = 128
DN_ALPHA = (2.0 * DEPTH) ** 0.25
DN_BETA = (8.0 * DEPTH) ** -0.25
LN_EPS = 1e-5
NEG = -1e30
FORCED = 1e6

kernel_name = 'hawk_nsa_yoco_decoder_step'


def alibi_slopes():
    return jnp.asarray(2.0 ** (-8.0 * np.arange(1, N_HEADS + 1) / N_HEADS), jnp.float32)


def layer_norm(x, g, b):
    xf = x.astype(jnp.float32)
    mu = xf.mean(-1, keepdims=True)
    var = jnp.square(xf - mu).mean(-1, keepdims=True)
    return ((xf - mu) * lax.rsqrt(var + LN_EPS) * g.astype(jnp.float32) + b.astype(jnp.float32)).astype(x.dtype)


def ada_mod(c, w, b):
    m = jax.nn.silu(c) @ w + b
    shift, scale, gate = jnp.split(m[:, None, :], 3, axis=-1)
    return shift, scale, gate


def masked_softmax(s, mask):
    s = jnp.where(mask, s.astype(jnp.float32), NEG)
    p = jax.nn.softmax(s, axis=-1)
    return jnp.where(mask, p, 0.0)


def lru_combine(left, right):
    a1, b1 = left
    a2, b2 = right
    return a1 * a2, a2 * b1 + b2


def rglru_mixer(h, h0, conv0, w_in, conv_w, conv_b, w_r, b_r, w_i, b_i, lam, w_out):
    B, T, _ = h.shape
    xb, zg = jnp.split(h @ w_in, 2, axis=-1)
    xp = jnp.concatenate([conv0.astype(xb.dtype), xb], axis=1)
    xc = conv_b + xp[:, 0:T] * conv_w[0]
    for k in range(1, CONV_W):
        xc = xc + xp[:, k:k + T] * conv_w[k]
    xg = xc.reshape(B, T, N_RG_BLOCKS, RG_BLOCK)
    r = jax.nn.sigmoid((jnp.einsum('btnc,ncd->btnd', xg, w_r).reshape(B, T, D_RNN) + b_r).astype(jnp.float32))
    i = jax.nn.sigmoid((jnp.einsum('btnc,ncd->btnd', xg, w_i).reshape(B, T, D_RNN) + b_i).astype(jnp.float32))
    log_a = -RG_C * jax.nn.softplus(-lam.astype(jnp.float32)) * r
    a = jnp.exp(log_a)
    gain = jnp.sqrt(jnp.maximum(-jnp.expm1(2.0 * log_a), 0.0))
    b = gain * i * xc.astype(jnp.float32)
    b = b.at[:, 0].add(a[:, 0] * h0.astype(jnp.float32))
    _, hs = lax.associative_scan(lru_combine, (a, b), axis=1)
    y = (hs.astype(h.dtype) * jax.nn.silu(zg)) @ w_out
    return y, hs[:, -1], xp[:, -(CONV_W - 1):]


def compress_blocks(kv, phi_pe, w_phi1, b_phi1, w_phi2, b_phi2):
    B, L = kv.shape[:2]
    nc = L // CMP_BLOCK
    blk = kv.reshape(B, nc, CMP_BLOCK, N_KV, 2, HEAD_DIM) + phi_pe[None, None, :, None]
    hid = jax.nn.silu(jnp.einsum('bnlgcd,cldp->bngcp', blk, w_phi1) + b_phi1)
    return jnp.einsum('bngcp,cpd->bngcd', hid, w_phi2) + b_phi2


def nsa_query_side(h, w_in, b_gate):
    B, T, _ = h.shape
    hd = N_HEADS * HEAD_DIM
    u = h @ w_in
    q = u[..., :hd].reshape(B, T, N_HEADS, HEAD_DIM)
    z = u[..., hd:2 * hd]
    gl = (u[..., 2 * hd:] + b_gate).reshape(B, T, N_HEADS, 3)
    return q, z, gl


def nsa_attend(q, gate_logits, t, kc, vc, c_end, fetch_sel, n_sel_blocks, wk, wv, w_pos):
    B, Tq = q.shape[:2]
    dt = q.dtype
    slopes = alibi_slopes().reshape(N_KV, HPG)
    qg = q.reshape(B, Tq, N_KV, HPG, HEAD_DIM) * (HEAD_DIM ** -0.5)
    tf = t.astype(jnp.float32)
    s = jnp.einsum('bqghd,bngd->bqghn', qg, kc).astype(jnp.float32)
    dist = tf[:, None] - c_end.astype(jnp.float32)[None, :]
    s = s - slopes[None, None, :, :, None] * dist[None, :, None, None, :]
    mask = (c_end[None, :] <= t[:, None])[None, :, None, None, :]
    p_cmp = masked_softmax(s, mask)
    o_cmp = jnp.einsum('bqghn,bngd->bqghd', p_cmp.astype(dt), vc)
    nc = kc.shape[1]
    imp = jnp.pad(p_cmp.sum(axis=3), ((0, 0), (0, 0), (0, 0), (0, n_sel_blocks - nc)))
    j = jnp.arange(n_sel_blocks)
    cb = t // CMP_BLOCK
    forced = (j[None, :] == 0) | (j[None, :] == cb[:, None]) | (j[None, :] == cb[:, None] - 1)
    causal_blk = j[None, :] <= cb[:, None]
    score = jnp.where(forced[:, None, :], FORCED, jnp.where(causal_blk[:, None, :], imp, -1.0))
    _, idx = lax.top_k(score, min(N_SEL, n_sel_blocks))
    kv_sel, pos = fetch_sel(idx)
    kl = pos.shape[-2] * pos.shape[-1]
    s = jnp.einsum('bqghd,bqgkld->bqghkl', qg, kv_sel[..., 0, :]).astype(jnp.float32)
    dist = (t[None, :, None, None, None] - pos).astype(jnp.float32)
    s = s - slopes[None, None, :, :, None, None] * dist[:, :, :, None]
    mask = (pos <= t[None, :, None, None, None])[:, :, :, None]
    p_sel = masked_softmax(s.reshape(B, Tq, N_KV, HPG, kl), mask.reshape(B, Tq, N_KV, 1, kl))
    o_sel = jnp.einsum('bqghs,bqgsd->bqghd', p_sel.astype(dt),
                       kv_sel[..., 1, :].reshape(B, Tq, N_KV, kl, HEAD_DIM))
    s = jnp.einsum('bqghd,bsgd->bqghs', qg, wk).astype(jnp.float32)
    dist = tf[:, None] - w_pos.astype(jnp.float32)[None, :]
    s = s - slopes[None, None, :, :, None] * dist[None, :, None, None, :]
    dpos = t[:, None] - w_pos[None, :]
    mask = ((dpos >= 0) & (dpos <= WINDOW) & (w_pos >= 0)[None, :])[None, :, None, None, :]
    p_win = masked_softmax(s, mask)
    o_win = jnp.einsum('bqghs,bsgd->bqghd', p_win.astype(dt), wv)
    g = jax.nn.sigmoid(gate_logits.astype(jnp.float32)).astype(dt).reshape(B, Tq, N_KV, HPG, 3)
    o = g[..., 0:1] * o_cmp + g[..., 1:2] * o_sel + g[..., 2:3] * o_win
    return o.reshape(B, Tq, N_HEADS * HEAD_DIM)


def nsa_prompt(h, kv_sel, kv_win, kc, vc, c_end, w_in, b_gate, w_out):
    B, T, _ = h.shape
    q, z, gl = nsa_query_side(h, w_in, b_gate)
    nqb = T // Q_BLOCK
    win_pad = jnp.pad(kv_win, ((0, 0), (WINDOW, 0), (0, 0), (0, 0), (0, 0)))
    bidx = jnp.arange(B)[:, None, None, None, None]
    gidx = jnp.arange(N_KV)[None, None, :, None, None]
    offs = jnp.arange(CMP_BLOCK)

    def fetch(idx):
        pos = idx[..., None] * CMP_BLOCK + offs
        return kv_sel[bidx, pos, gidx], pos

    def one_block(args):
        qb, gb, start = args
        t = start + jnp.arange(Q_BLOCK)
        wkv = lax.dynamic_slice_in_dim(win_pad, start, WINDOW + Q_BLOCK, axis=1)
        w_pos = start - WINDOW + jnp.arange(WINDOW + Q_BLOCK)
        return nsa_attend(qb, gb, t, kc, vc, c_end, fetch, T // CMP_BLOCK,
                          wkv[..., 0, :], wkv[..., 1, :], w_pos)

    qb = q.reshape(B, nqb, Q_BLOCK, N_HEADS, HEAD_DIM).swapaxes(0, 1)
    gb = gl.reshape(B, nqb, Q_BLOCK, N_HEADS, 3).swapaxes(0, 1)
    starts = jnp.arange(nqb, dtype=jnp.int32) * Q_BLOCK
    o = lax.map(one_block, (qb, gb, starts))
    o = o.swapaxes(0, 1).reshape(B, T, N_HEADS * HEAD_DIM)
    return (o * jax.nn.silu(z)) @ w_out


def nsa_sample(h, new_sel, win_keys, past_len, kc, vc, c_end, cache_sel, page_table, w_in, b_gate, w_out):
    B, S, _ = h.shape
    q, z, gl = nsa_query_side(h, w_in, b_gate)
    t = past_len + jnp.arange(S)
    bidx = jnp.arange(B)[:, None, None, None, None]
    gidx = jnp.arange(N_KV)[None, None, :, None, None]
    offs = jnp.arange(CMP_BLOCK)

    def fetch(idx):
        pos = idx[..., None] * CMP_BLOCK + offs
        pp = jnp.minimum(pos, past_len - 1)
        phys = page_table[bidx, pp // PAGE_SIZE]
        past_rows = cache_sel[phys, pp % PAGE_SIZE, gidx]
        new_rows = new_sel[bidx, jnp.clip(pos - past_len, 0, S - 1), gidx]
        return jnp.where((pos < past_len)[..., None, None], past_rows, new_rows), pos

    wb = win_keys.shape[1] - S
    w_pos = past_len - wb + jnp.arange(wb + S)
    n_sb = -(-(past_len + S) // CMP_BLOCK)
    o = nsa_attend(q, gl, t, kc, vc, c_end, fetch, n_sb, win_keys[..., 0, :], win_keys[..., 1, :], w_pos)
    return (o * jax.nn.silu(z)) @ w_out


def setup_inputs(seed: int = 0) -> dict:
    key = jax.random.key(seed)
    ks = iter(jax.random.split(key, 48))

    def nrm(shape, s):
        return jax.random.normal(next(ks), shape, jnp.float32) * s

    n_pages = PAST_LEN // PAGE_SIZE
    n_phys = (5 * DEC_BATCH * n_pages) // 4
    wb = min(WINDOW, PAST_LEN)
    perm = jax.random.permutation(next(ks), n_phys)[:DEC_BATCH * n_pages]
    page_table = perm.reshape(DEC_BATCH, n_pages).astype(jnp.int32)
    a0 = jax.random.uniform(next(ks), (N_A_LAYERS, D_RNN), jnp.float32, 0.9, 0.999)
    lam_a = jnp.log(a0) - jnp.log1p(-a0)
    hd = N_HEADS * HEAD_DIM
    return {
        'x_prompt': nrm((BATCH, SEQ, D_MODEL), 1.0),
        'x_sample': nrm((DEC_BATCH, DEC_SEQ, D_MODEL), 1.0),
        'c_prompt': nrm((BATCH, D_MODEL), 1.0),
        'c_sample': nrm((DEC_BATCH, D_MODEL), 1.0),
        'state_h': nrm((N_A_LAYERS, DEC_BATCH, D_RNN), 0.5),
        'state_conv': nrm((N_A_LAYERS, DEC_BATCH, CONV_W - 1, D_RNN), 1.0),
        'cache_cmp': nrm((n_phys, PAGE_SIZE, N_KV, 2, HEAD_DIM), 1.0),
        'cache_sel': nrm((n_phys, PAGE_SIZE, N_KV, 2, HEAD_DIM), 1.0),
        'state_win': nrm((DEC_BATCH, wb, N_KV, 2, HEAD_DIM), 1.0),
        'page_table': page_table,
        'w_ada': nrm((DEPTH, D_MODEL, 3 * D_MODEL), 0.2 * D_MODEL ** -0.5),
        'b_ada': nrm((DEPTH, 3 * D_MODEL), 0.01),
        'ln_g': 1.0 + nrm((DEPTH, D_MODEL), 0.02),
        'ln_b': nrm((DEPTH, D_MODEL), 0.02),
        'w_in_a': nrm((N_A_LAYERS, D_MODEL, 2 * D_RNN), D_MODEL ** -0.5),
        'conv_w_a': nrm((N_A_LAYERS, CONV_W, D_RNN), CONV_W ** -0.5),
        'conv_b_a': nrm((N_A_LAYERS, D_RNN), 0.02),
        'w_r_a': nrm((N_A_LAYERS, N_RG_BLOCKS, RG_BLOCK, RG_BLOCK), RG_BLOCK ** -0.5),
        'b_r_a': nrm((N_A_LAYERS, D_RNN), 0.02),
        'w_i_a': nrm((N_A_LAYERS, N_RG_BLOCKS, RG_BLOCK, RG_BLOCK), RG_BLOCK ** -0.5),
        'b_i_a': nrm((N_A_LAYERS, D_RNN), 0.02),
        'lam_a': lam_a,
        'w_out_a': nrm((N_A_LAYERS, D_RNN, D_MODEL), DN_BETA * D_RNN ** -0.5),
        'w_kv': nrm((D_MODEL, 3 * N_KV * 2 * HEAD_DIM), D_MODEL ** -0.5),
        'phi_pe': nrm((CMP_BLOCK, 2, HEAD_DIM), 0.1),
        'w_phi1': nrm((2, CMP_BLOCK, HEAD_DIM, D_PHI), (CMP_BLOCK * HEAD_DIM) ** -0.5),
        'b_phi1': nrm((2, D_PHI), 0.02),
        'w_phi2': nrm((2, D_PHI, HEAD_DIM), D_PHI ** -0.5),
        'b_phi2': nrm((2, HEAD_DIM), 0.02),
        'w_in_b': nrm((N_B_LAYERS, D_MODEL, 2 * hd + 3 * N_HEADS), D_MODEL ** -0.5),
        'b_gate_b': nrm((N_B_LAYERS, 3 * N_HEADS), 0.1),
        'w_out_b': nrm((N_B_LAYERS, hd, D_MODEL), DN_BETA * hd ** -0.5),
    }


def reference(x_prompt, x_sample, c_prompt, c_sample, state_h, state_conv, cache_cmp, cache_sel, state_win,
              page_table, w_ada, b_ada, ln_g, ln_b, w_in_a, conv_w_a, conv_b_a, w_r_a, b_r_a, w_i_a, b_i_a,
              lam_a, w_out_a, w_kv, phi_pe, w_phi1, b_phi1, w_phi2, b_phi2, w_in_b, b_gate_b, w_out_b):
    Bp, T, _ = x_prompt.shape
    Bs, S, _ = x_sample.shape
    past_len = page_table.shape[1] * PAGE_SIZE
    xp, xs = x_prompt, x_sample
    hp_list, cp_list, hs_list, cs_list = [], [], [], []
    for layer in range(DEPTH):
        shp, scp, gp = ada_mod(c_prompt, w_ada[layer], b_ada[layer])
        shs, scs, gs = ada_mod(c_sample, w_ada[layer], b_ada[layer])
        mp = xp * (1.0 + scp) + shp
        ms = xs * (1.0 + scs) + shs
        if layer < N_A_LAYERS:
            a = layer
            fp, hp, cp = rglru_mixer(mp, jnp.zeros((Bp, D_RNN), xp.dtype),
                                     jnp.zeros((Bp, CONV_W - 1, D_RNN), xp.dtype),
                                     w_in_a[a], conv_w_a[a], conv_b_a[a], w_r_a[a], b_r_a[a],
                                     w_i_a[a], b_i_a[a], lam_a[a], w_out_a[a])
            fs, hs, cs = rglru_mixer(ms, state_h[a], state_conv[a],
                                     w_in_a[a], conv_w_a[a], conv_b_a[a], w_r_a[a], b_r_a[a],
                                     w_i_a[a], b_i_a[a], lam_a[a], w_out_a[a])
            hp_list.append(hp)
            cp_list.append(cp)
            hs_list.append(hs)
            cs_list.append(cs)
        else:
            if layer == N_A_LAYERS:
                kvp = (xp @ w_kv).reshape(Bp, T, 3, N_KV, 2, HEAD_DIM)
                kvs = (xs @ w_kv).reshape(Bs, S, 3, N_KV, 2, HEAD_DIM)
                new_cmp_p, new_sel_p, win_p = kvp[:, :, 0], kvp[:, :, 1], kvp[:, :, 2]
                new_cmp_s, new_sel_s, win_s = kvs[:, :, 0], kvs[:, :, 1], kvs[:, :, 2]
                nc_p = T // CMP_BLOCK
                comp_p = compress_blocks(new_cmp_p[:, :nc_p * CMP_BLOCK], phi_pe, w_phi1, b_phi1, w_phi2, b_phi2)
                c_end_p = (jnp.arange(nc_p) + 1) * CMP_BLOCK - 1
                past_cmp = cache_cmp[page_table].reshape(Bs, past_len, N_KV, 2, HEAD_DIM)
                full_cmp = jnp.concatenate([past_cmp, new_cmp_s.astype(past_cmp.dtype)], axis=1)
                nc_s = (past_len + S) // CMP_BLOCK
                comp_s = compress_blocks(full_cmp[:, :nc_s * CMP_BLOCK], phi_pe, w_phi1, b_phi1, w_phi2, b_phi2)
                c_end_s = (jnp.arange(nc_s) + 1) * CMP_BLOCK - 1
                win_keys_s = jnp.concatenate([state_win.astype(win_s.dtype), win_s], axis=1)
                new_win_p = win_p[:, -min(WINDOW, T):]
                new_win_s = win_keys_s[:, -state_win.shape[1]:]
            bl = layer - N_A_LAYERS
            fp = nsa_prompt(mp, new_sel_p, win_p, comp_p[..., 0, :], comp_p[..., 1, :], c_end_p,
                            w_in_b[bl], b_gate_b[bl], w_out_b[bl])
            fs = nsa_sample(ms, new_sel_s, win_keys_s, past_len, comp_s[..., 0, :], comp_s[..., 1, :], c_end_s,
                            cache_sel, page_table, w_in_b[bl], b_gate_b[bl], w_out_b[bl])
        xp = layer_norm(DN_ALPHA * xp + (1.0 + gp) * fp, ln_g[layer], ln_b[layer])
        xs = layer_norm(DN_ALPHA * xs + (1.0 + gs) * fs, ln_g[layer], ln_b[layer])
    new_h_p = jnp.stack(hp_list)
    new_conv_p = jnp.stack(cp_list)
    new_h_s = jnp.stack(hs_list)
    new_conv_s = jnp.stack(cs_list)
    return (xp, xs, new_cmp_p, new_sel_p, new_win_p, new_h_p, new_conv_p,
            new_cmp_s, new_sel_s, new_win_s, new_h_s, new_conv_s)
```

```python
import functools
import math

import numpy as np
import jax
import jax.numpy as jnp
from jax import lax
from jax.experimental import pallas as pl
from jax.experimental.pallas import tpu as pltpu

f32 = jnp.float32
bf16 = jnp.bfloat16
i32 = jnp.int32

DEPTH = 2
N_RG_BLOCKS = 8
CONV_W = 4
RG_C = 8.0
N_HEADS = 16
HEAD_DIM = 64
N_KV = 4
HPG = N_HEADS // N_KV
CMP_BLOCK = 64
N_SEL = 16
WINDOW = 512
PAGE_SIZE = 128
DN_ALPHA = (2.0 * DEPTH) ** 0.25
LN_EPS = 1e-5
NEG = -1e30
FORCED = 1e6
REMOVED = -3e38
GRP = 2 * HEAD_DIM
QGRP = HPG * HEAD_DIM
VMEM_LIMIT_BYTES = 56 * 1024 * 1024


def _params(sem):
    return pltpu.CompilerParams(dimension_semantics=sem, vmem_limit_bytes=VMEM_LIMIT_BYTES)


def _dot(a, b):
    return jnp.dot(a, b, preferred_element_type=f32)


def _dot_nt(a, b):
    return lax.dot_general(a, b, (((1,), (1,)), ((), ())), preferred_element_type=f32)


def _sigmoid(x):
    return 1.0 / (1.0 + jnp.exp(-x))


def _silu(x):
    return x * _sigmoid(x)


def _log1p(e):
    u = 1.0 + e
    dlt = u - 1.0
    return jnp.where(dlt == 0.0, e, jnp.log(u) * (e / jnp.where(dlt == 0.0, 1.0, dlt)))


def _layer_norm(x, g, b):
    mu = jnp.mean(x, axis=-1, keepdims=True)
    xc = x - mu
    var = jnp.mean(xc * xc, axis=-1, keepdims=True)
    return xc * lax.rsqrt(var + LN_EPS) * g + b


def _ada_kernel(c_ref, w_ref, b_ref, o_ref):
    a = _silu(c_ref[...])
    o_ref[0] = jnp.dot(a, w_ref[0], preferred_element_type=f32,
                       precision=lax.Precision.HIGHEST) + b_ref[0]


def _ada_mod(c_all, w_ada, b_ada):
    rows, d = c_all.shape
    depth = w_ada.shape[0]
    return pl.pallas_call(
        _ada_kernel,
        grid=(depth, 3),
        in_specs=[
            pl.BlockSpec((rows, d), lambda l, n: (0, 0)),
            pl.BlockSpec((1, d, d), lambda l, n: (l, 0, n)),
            pl.BlockSpec((1, 1, d), lambda l, n: (l, 0, n)),
        ],
        out_specs=pl.BlockSpec((1, rows, d), lambda l, n: (l, 0, n)),
        out_shape=jax.ShapeDtypeStruct((depth, rows, 3 * d), f32),
        compiler_params=_params(("arbitrary", "arbitrary")),
        name="ada_mod",
    )(c_all, w_ada, b_ada.reshape(depth, 1, 3 * d))


def _rglru_gates(xc, wg_ref, br, bi, lam):
    xcb = xc.astype(bf16)
    rg = xc.shape[1] // N_RG_BLOCKS
    rs, is_ = [], []
    for n in range(N_RG_BLOCKS):
        g = _dot(xcb[:, n * rg:(n + 1) * rg], wg_ref[n])
        rs.append(g[:, :rg])
        is_.append(g[:, rg:])
    r = _sigmoid(jnp.concatenate(rs, axis=1) + br)
    i = _sigmoid(jnp.concatenate(is_, axis=1) + bi)
    nl = -lam
    softplus = jnp.maximum(nl, 0.0) + _log1p(jnp.exp(-jnp.abs(nl)))
    log_a = (-RG_C * softplus) * r
    a = jnp.exp(log_a)
    gain = jnp.sqrt(jnp.maximum(-jnp.tanh(log_a) * (a * a + 1.0), 0.0))
    b = gain * i * xc
    return a, b


def _rglru_prompt_kernel(x_ref, mod_ref, h0_ref, c0_ref, win_ref, cw_ref, cb_ref, wg_ref, br_ref,
                         bi_ref, lam_ref, wout_ref, lng_ref, lnb_ref,
                         xo_ref, hl_ref, cl_ref, xbuf, a_s, b_s, hs_s, hc, *, tc, d):
    t = pl.program_id(1)

    @pl.when(t == 0)
    def _():
        xbuf[0:8, :] = jnp.zeros((8, d), f32)
        xbuf[8 - (CONV_W - 1):8, :] = c0_ref[0]
        hc[...] = jnp.broadcast_to(h0_ref[0], (8, d))

    x = x_ref[0]
    mod = mod_ref[0]
    shift, scale, gate = mod[0:1], mod[1:2], mod[2:3]
    m = x * (1.0 + scale) + shift
    u = _dot(m.astype(bf16), win_ref[...])
    xb = u[:, :d]
    zg = u[:, d:]
    xbuf[8:8 + tc, :] = xb
    base = 8 - (CONV_W - 1)
    xc = cb_ref[...] + xbuf[base:base + tc, :] * cw_ref[0:1, :]
    for k in range(1, CONV_W):
        xc = xc + xbuf[base + k:base + k + tc, :] * cw_ref[k:k + 1, :]
    tail = xbuf[8 + tc - (CONV_W - 1):8 + tc, :]
    xbuf[base:8, :] = tail
    cl_ref[0] = tail

    a, b = _rglru_gates(xc, wg_ref, br_ref[...], bi_ref[...], lam_ref[...])

    rowi = lax.broadcasted_iota(i32, (tc, 1), 0) % 8
    for s in (1, 2, 4):
        ok = rowi >= s
        a_sh = pltpu.roll(a, s, 0)
        b_sh = pltpu.roll(b, s, 0)
        b = jnp.where(ok, a * b_sh + b, b)
        a = jnp.where(ok, a * a_sh, a)
    a_s[...] = a
    b_s[...] = b

    def tile_step(j, hprev):
        r0 = pl.multiple_of(j * 8, 8)
        ht = a_s[pl.ds(r0, 8), :] * hprev + b_s[pl.ds(r0, 8), :]
        hs_s[pl.ds(r0, 8), :] = ht
        return jnp.broadcast_to(ht[7:8, :], (8, d))

    hlast = lax.fori_loop(0, tc // 8, tile_step, hc[...])
    hc[...] = hlast
    hl_ref[0] = hlast[0:1, :]

    y = _dot((hs_s[...] * _silu(zg)).astype(bf16), wout_ref[...])
    xo_ref[0] = _layer_norm(DN_ALPHA * x + (1.0 + gate) * y, lng_ref[...], lnb_ref[...])


def _rglru_prompt(x, mod, h0, c0, w_in, conv_w, conv_b, w_gate, b_r, b_i, lam, w_out, ln_g, ln_b, tc):
    bsz, t, d = x.shape
    tc = min(tc, t)
    const2 = lambda b, i: (0, 0)
    const3 = lambda b, i: (0, 0, 0)
    perb = lambda b, i: (b, 0, 0)
    kern = functools.partial(_rglru_prompt_kernel, tc=tc, d=d)
    return pl.pallas_call(
        kern,
        grid=(bsz, t // tc),
        in_specs=[
            pl.BlockSpec((1, tc, d), lambda b, i: (b, i, 0)),
            pl.BlockSpec((1, 3, d), perb),
            pl.BlockSpec((1, 1, d), perb),
            pl.BlockSpec((1, CONV_W - 1, d), perb),
            pl.BlockSpec(w_in.shape, const2),
            pl.BlockSpec(conv_w.shape, const2),
            pl.BlockSpec((1, d), const2),
            pl.BlockSpec(w_gate.shape, const3),
            pl.BlockSpec((1, d), const2),
            pl.BlockSpec((1, d), const2),
            pl.BlockSpec((1, d), const2),
            pl.BlockSpec(w_out.shape, const2),
            pl.BlockSpec((1, d), const2),
            pl.BlockSpec((1, d), const2),
        ],
        out_specs=[
            pl.BlockSpec((1, tc, d), lambda b, i: (b, i, 0)),
            pl.BlockSpec((1, 1, d), perb),
            pl.BlockSpec((1, CONV_W - 1, d), perb),
        ],
        out_shape=[
            jax.ShapeDtypeStruct((bsz, t, d), f32),
            jax.ShapeDtypeStruct((bsz, 1, d), f32),
            jax.ShapeDtypeStruct((bsz, CONV_W - 1, d), f32),
        ],
        scratch_shapes=[
            pltpu.VMEM((tc + 8, d), f32),
            pltpu.VMEM((tc, d), f32),
            pltpu.VMEM((tc, d), f32),
            pltpu.VMEM((tc, d), f32),
            pltpu.VMEM((8, d), f32),
        ],
        compiler_params=_params(("arbitrary", "arbitrary")),
        name="rglru_prompt",
    )(x, mod, h0, c0, w_in, conv_w, conv_b, w_gate, b_r, b_i, lam, w_out, ln_g, ln_b)


def _rglru_sample_kernel(x_ref, mod_ref, h0_ref, c0_ref, win_ref, cw_ref, cb_ref, wg_ref, br_ref,
                         bi_ref, lam_ref, wout_ref, lng_ref, lnb_ref,
                         xo_ref, hl_ref, cl_ref, *, s_len, bsz, d):
    x = x_ref[...]
    mod = mod_ref[...]
    shift, scale, gate = mod[0:1], mod[1:2], mod[2:3]
    m = x * (1.0 + scale) + shift
    u = _dot(m.reshape(s_len * bsz, d).astype(bf16), win_ref[...])
    xb = u[:, :d].reshape(s_len, bsz, d)
    zg = u[:, d:]
    xp = jnp.concatenate([c0_ref[...], xb], axis=0)
    cw = cw_ref[...]
    xc = cb_ref[...] + xp[0:s_len] * cw[0:1]
    for k in range(1, CONV_W):
        xc = xc + xp[k:k + s_len] * cw[k:k + 1]
    cl_ref[...] = xp[s_len:s_len + CONV_W - 1]
    a, b = _rglru_gates(xc.reshape(s_len * bsz, d), wg_ref, br_ref[...], bi_ref[...], lam_ref[...])
    h = h0_ref[...]
    hs = []
    for s in range(s_len):
        h = a[s * bsz:(s + 1) * bsz] * h + b[s * bsz:(s + 1) * bsz]
        hs.append(h)
    hl_ref[...] = h
    hs = jnp.concatenate(hs, axis=0)
    y = _dot((hs * _silu(zg)).astype(bf16), wout_ref[...])
    xo = _layer_norm(DN_ALPHA * x + (1.0 + gate) * y.reshape(s_len, bsz, d), lng_ref[...], lnb_ref[...])
    xo_ref[...] = xo


def _rglru_sample(x_tm, mod_tm, h0, c0_tm, w_in, conv_w, conv_b, w_gate, b_r, b_i, lam, w_out, ln_g, ln_b):
    s_len, bsz, d = x_tm.shape
    kern = functools.partial(_rglru_sample_kernel, s_len=s_len, bsz=bsz, d=d)
    return pl.pallas_call(
        kern,
        out_shape=[
            jax.ShapeDtypeStruct((s_len, bsz, d), f32),
            jax.ShapeDtypeStruct((bsz, d), f32),
            jax.ShapeDtypeStruct((CONV_W - 1, bsz, d), f32),
        ],
        compiler_params=pltpu.CompilerParams(vmem_limit_bytes=VMEM_LIMIT_BYTES),
        name="rglru_sample",
    )(x_tm, mod_tm, h0, c0_tm, w_in, conv_w, conv_b, w_gate, b_r, b_i, lam, w_out, ln_g, ln_b)


def _proj_kernel(x_ref, mod_ref, wkv_ref, wq_ref, wz_ref, wg_ref, bg_ref,
                 cmp_ref, sel_ref, win_ref, selb_ref, winb_ref, q_ref, z_ref, gl_ref, *, nb, tr, d):
    x = x_ref[...]
    mod = mod_ref[...]
    m = x * (1.0 + mod[:, 1:2, :]) + mod[:, 0:1, :]
    xb = x.reshape(nb * tr, d).astype(bf16)
    mb = m.reshape(nb * tr, d).astype(bf16)
    kv = _dot(xb, wkv_ref[...])
    w = kv.shape[1] // 3
    sel = kv[:, w:2 * w]
    win = kv[:, 2 * w:]
    cmp_ref[...] = kv[:, :w]
    sel_ref[...] = sel
    win_ref[...] = win
    selb_ref[...] = sel.astype(bf16)
    winb_ref[...] = win.astype(bf16)
    q_ref[...] = (_dot(mb, wq_ref[...]) * (HEAD_DIM ** -0.5)).astype(bf16)
    z_ref[...] = _dot(mb, wz_ref[...])
    gl_ref[...] = _dot(mb, wg_ref[...]) + bg_ref[...]


def _proj(x, mod, w_kv, w_q, w_z, w_g, b_g, nb, tr):
    bsz, t, d = x.shape
    kvw = w_kv.shape[1] // 3
    hd = w_q.shape[1]
    gw = w_g.shape[1]
    nt = t // tr
    const2 = lambda b, i: (0, 0)
    blk = lambda width: pl.BlockSpec((nb * tr, width), lambda b, i: (b * nt + i, 0))
    out = lambda width, dt: jax.ShapeDtypeStruct((bsz * t, width), dt)
    kern = functools.partial(_proj_kernel, nb=nb, tr=tr, d=d)
    return pl.pallas_call(
        kern,
        grid=(bsz // nb, nt),
        in_specs=[
            pl.BlockSpec((nb, tr, d), lambda b, i: (b, i, 0)),
            pl.BlockSpec((nb, 3, d), lambda b, i: (b, 0, 0)),
            pl.BlockSpec(w_kv.shape, const2),
            pl.BlockSpec(w_q.shape, const2),
            pl.BlockSpec(w_z.shape, const2),
            pl.BlockSpec(w_g.shape, const2),
            pl.BlockSpec((1, gw), const2),
        ],
        out_specs=[blk(kvw), blk(kvw), blk(kvw), blk(kvw), blk(kvw), blk(hd), blk(hd), blk(gw)],
        out_shape=[out(kvw, f32), out(kvw, f32), out(kvw, f32), out(kvw, bf16), out(kvw, bf16),
                   out(hd, bf16), out(hd, f32), out(gw, f32)],
        compiler_params=_params(("arbitrary", "arbitrary")),
        name="nsa_proj",
    )(x, mod, w_kv, w_q, w_z, w_g, b_g)


def _compress_kernel(x_ref, pe_ref, w1_ref, b1_ref, w2_ref, b2_ref, o_ref, *, nb):
    stride = CMP_BLOCK * N_KV

    def tok(l, g):
        x = x_ref[pl.ds(l * N_KV + g, nb, stride=stride), :] + pe_ref[l:l + 1, :]
        return x.astype(bf16)

    acc = None
    for lp in range(CMP_BLOCK // 2):
        lhs = jnp.concatenate(
            [jnp.concatenate([tok(2 * lp, g), tok(2 * lp + 1, g)], axis=1) for g in range(N_KV)],
            axis=0)
        part = _dot(lhs, w1_ref[lp])
        acc = part if acc is None else acc + part
    hid = _silu(acc + b1_ref[...])
    out = _dot(hid.astype(bf16), w2_ref[...]) + b2_ref[...]
    for g in range(N_KV):
        o_ref[g] = out[g * nb:(g + 1) * nb, :].astype(o_ref.dtype)


def _compress(x2d, pe, w1, b1, w2, b2, nb, out_dtype):
    rows, width = x2d.shape
    n_blocks = rows // (CMP_BLOCK * N_KV)
    nb = min(nb, n_blocks)
    while n_blocks % nb:
        nb -= 8
    const2 = lambda i: (0, 0)
    kern = functools.partial(_compress_kernel, nb=nb)
    return pl.pallas_call(
        kern,
        grid=(n_blocks // nb,),
        in_specs=[
            pl.BlockSpec((nb * CMP_BLOCK * N_KV, width), lambda i: (i, 0)),
            pl.BlockSpec(pe.shape, const2),
            pl.BlockSpec(w1.shape, lambda i: (0, 0, 0)),
            pl.BlockSpec(b1.shape, const2),
            pl.BlockSpec(w2.shape, const2),
            pl.BlockSpec(b2.shape, const2),
        ],
        out_specs=pl.BlockSpec((N_KV, nb, GRP), lambda i: (0, i, 0)),
        out_shape=jax.ShapeDtypeStruct((N_KV, n_blocks, GRP), out_dtype),
        compiler_params=_params(("arbitrary",)),
        name="nsa_compress",
    )(x2d, pe, w1, b1, w2, b2)


def _cmp_branch(q4, kc, vc, slope, t_col, nq):
    n_c = kc.shape[0]
    s = _dot_nt(q4, kc)
    jrow = lax.broadcasted_iota(i32, (1, n_c), 1)
    c_end = (jrow + 1) * CMP_BLOCK - 1
    dist = t_col.astype(f32) - c_end.astype(f32)
    s = s - slope * dist
    mask = c_end <= t_col
    s = jnp.where(mask, s, NEG)
    e = jnp.exp(s - jnp.max(s, axis=-1, keepdims=True))
    p = e / jnp.sum(e, axis=-1, keepdims=True)
    p = jnp.where(mask, p, 0.0)
    o = _dot(p.astype(bf16), vc)
    imp = p[0:nq]
    for h in range(1, HPG):
        imp = imp + p[h * nq:(h + 1) * nq]
    return o, imp


def _select_blocks(imp, tq_col, n_sb):
    nq, lanes = imp.shape
    jr = lax.broadcasted_iota(i32, (1, lanes), 1)
    cb = tq_col // CMP_BLOCK
    forced = (jr == 0) | (jr == cb) | (jr == cb - 1)
    causal = jr <= cb
    score = jnp.where(forced, FORCED, jnp.where(causal, imp, -1.0))
    score = jnp.where(jr < n_sb, score, REMOVED)
    jf = jr.astype(f32)
    sel = jnp.zeros((nq, lanes), f32)
    for _ in range(min(N_SEL, n_sb)):
        mx = jnp.max(score, axis=-1, keepdims=True)
        idx = jnp.min(jnp.where(score == mx, jf, 1e9), axis=-1, keepdims=True)
        hit = jf == idx
        sel = jnp.where(hit, 1.0, sel)
        score = jnp.where(hit, REMOVED, score)
    return sel


def _flash_update(carry, q4, k, v, slope, valid, pos_rel):
    m_i, l_i, acc = carry
    s = _dot_nt(q4, k) + slope * pos_rel
    s = jnp.where(valid, s, NEG)
    m_new = jnp.maximum(m_i, jnp.max(s, axis=-1, keepdims=True))
    alpha = jnp.exp(m_i - m_new)
    p = jnp.exp(s - m_new)
    l_new = alpha * l_i + jnp.sum(p, axis=-1, keepdims=True)
    acc = alpha * acc + _dot(p.astype(bf16), v)
    return m_new, l_new, acc


def _block_mask(selb, pos, reps):
    lanes = selb.shape[1]
    onehot = (lax.broadcasted_iota(i32, (lanes, 1), 0) == pos // CMP_BLOCK).astype(bf16)
    msk = _dot(selb, onehot)
    return jnp.concatenate([msk] * reps, axis=0) > 0.5


def _gate_mix(gl, o_cmp, o_sel, o_win, nq):
    gs = _sigmoid(gl)
    outs = []
    for h in range(HPG):
        sl = slice(h * nq, (h + 1) * nq)
        outs.append(gs[:, 3 * h:3 * h + 1] * o_cmp[sl] + gs[:, 3 * h + 1:3 * h + 2] * o_sel[sl]
                    + gs[:, 3 * h + 2:3 * h + 3] * o_win[sl])
    return jnp.concatenate(outs, axis=1)


def _init_carry(rows):
    return (jnp.full((rows, 1), NEG, f32), jnp.zeros((rows, 1), f32), jnp.zeros((rows, HEAD_DIM), f32))


def _nsa_prompt_kernel(q_ref, gl_ref, comp_ref, ksel_ref, kwin_ref, slope_ref, o_ref, *, t_len, tq, tk, tkw):
    qi = pl.program_id(2)
    t0 = qi * tq
    rows = HPG * tq
    q = q_ref[0]
    q4 = jnp.concatenate([q[:, h * HEAD_DIM:(h + 1) * HEAD_DIM] for h in range(HPG)], axis=0)
    slope = slope_ref[0]
    t_col = t0 + lax.broadcasted_iota(i32, (rows, 1), 0) % tq
    tq_col = t0 + lax.broadcasted_iota(i32, (tq, 1), 0)

    comp = comp_ref[0]
    kc = comp[:, :HEAD_DIM].astype(bf16)
    vc = comp[:, HEAD_DIM:].astype(bf16)
    o_cmp, imp = _cmp_branch(q4, kc, vc, slope, t_col, tq)
    selb = _select_blocks(imp, tq_col, t_len // CMP_BLOCK).astype(bf16)

    def sel_step(kt, carry):
        p0 = pl.multiple_of(kt * tk, tk)
        kv = ksel_ref[0, pl.ds(p0, tk), :]
        pos = p0 + lax.broadcasted_iota(i32, (1, tk), 1)
        valid = _block_mask(selb, pos, HPG) & (pos <= t_col)
        return _flash_update(carry, q4, kv[:, :HEAD_DIM], kv[:, HEAD_DIM:], slope, valid,
                             (pos - t0).astype(f32))

    n_tiles = (t0 + tq + tk - 1) // tk
    _, l_i, acc = lax.fori_loop(0, n_tiles, sel_step, _init_carry(rows))
    o_sel = acc / l_i

    def win_step(kw, carry):
        p0 = pl.multiple_of(kw * tkw, tkw)
        kv = kwin_ref[0, pl.ds(p0, tkw), :]
        pos = p0 + lax.broadcasted_iota(i32, (1, tkw), 1)
        dpos = t_col - pos
        valid = (dpos >= 0) & (dpos <= WINDOW)
        return _flash_update(carry, q4, kv[:, :HEAD_DIM], kv[:, HEAD_DIM:], slope, valid,
                             (pos - t0).astype(f32))

    first = jnp.maximum(t0 - WINDOW, 0) // tkw
    last = (t0 + tq - 1) // tkw
    _, l_i, acc = lax.fori_loop(first, last + 1, win_step, _init_carry(rows))
    o_win = acc / l_i

    o_ref[0] = _gate_mix(gl_ref[0], o_cmp, o_sel, o_win, tq)


def _nsa_prompt(q, gl, comp, ksel, kwin, slopes, tq, tk, tkw):
    bsz, t_len, _ = q.shape
    tq = min(tq, t_len)
    tk = min(tk, t_len)
    tkw = min(tkw, t_len)
    nc = t_len // CMP_BLOCK
    kern = functools.partial(_nsa_prompt_kernel, t_len=t_len, tq=tq, tk=tk, tkw=tkw)
    return pl.pallas_call(
        kern,
        grid=(bsz, N_KV, t_len // tq),
        in_specs=[
            pl.BlockSpec((1, tq, QGRP), lambda b, g, i: (b, i, g)),
            pl.BlockSpec((1, tq, 128), lambda b, g, i: (b, i, g)),
            pl.BlockSpec((1, nc, GRP), lambda b, g, i: (g, b, 0)),
            pl.BlockSpec((1, t_len, GRP), lambda b, g, i: (b, 0, g)),
            pl.BlockSpec((1, t_len, GRP), lambda b, g, i: (b, 0, g)),
            pl.BlockSpec((1, HPG * tq, 1), lambda b, g, i: (g, 0, 0)),
        ],
        out_specs=pl.BlockSpec((1, tq, QGRP), lambda b, g, i: (b, i, g)),
        out_shape=jax.ShapeDtypeStruct((bsz, t_len, N_KV * QGRP), f32),
        compiler_params=_params(("arbitrary", "arbitrary", "arbitrary")),
        name="nsa_prompt_attn",
    )(q, gl, comp, ksel, kwin, slopes)


def _nsa_sample_kernel(pt_ref, q_ref, gl_ref, compall_ref, bidx_ref, newsel_ref, swin_ref, nwin_ref,
                       slope_ref, *rest, s_len, past_len, pp, n_groups, lanes):
    page_refs = rest[:pp]
    o_ref = rest[pp]
    sel_s, ocmp_s, m_s, l_s, acc_s = rest[pp + 1:]
    j = pl.program_id(1)
    rows = HPG * s_len
    n_c = past_len // CMP_BLOCK
    q = q_ref[0].astype(f32)
    t_col = past_len + lax.broadcasted_iota(i32, (rows, 1), 0) % s_len
    tq_col = past_len + lax.broadcasted_iota(i32, (s_len, 1), 0)

    def q_of(g):
        return jnp.concatenate(
            [q[:, (g * HPG + h) * HEAD_DIM:(g * HPG + h + 1) * HEAD_DIM] for h in range(HPG)],
            axis=0).astype(bf16)

    @pl.when(j == 0)
    def _():
        n_all = compall_ref.shape[1]
        gather = (lax.broadcasted_iota(i32, (n_c, n_all), 1) == bidx_ref[0]).astype(bf16)
        for g in range(N_KV):
            kcvc = _dot(gather, compall_ref[g])
            kc = kcvc[:, :HEAD_DIM].astype(bf16)
            vc = kcvc[:, HEAD_DIM:].astype(bf16)
            o_cmp, imp = _cmp_branch(q_of(g), kc, vc, slope_ref[g], t_col, s_len)
            imp = jnp.concatenate([imp, jnp.zeros((s_len, lanes - n_c), f32)], axis=1)
            sel_s[g] = _select_blocks(imp, tq_col, n_c + 1)
            ocmp_s[g] = o_cmp
            m_s[g] = jnp.full((rows, 1), NEG, f32)
            l_s[g] = jnp.zeros((rows, 1), f32)
            acc_s[g] = jnp.zeros((rows, HEAD_DIM), f32)

    def attend(g, keys, pos, valid_extra):
        carry = (m_s[g], l_s[g], acc_s[g])
        valid = _block_mask(sel_s[g].astype(bf16), pos, HPG) & (pos <= t_col)
        if valid_extra is not None:
            valid = valid & valid_extra
        k = keys[:, g * GRP:g * GRP + HEAD_DIM].astype(bf16)
        v = keys[:, g * GRP + HEAD_DIM:(g + 1) * GRP].astype(bf16)
        m_new, l_new, acc = _flash_update(carry, q_of(g), k, v, slope_ref[g], valid,
                                          (pos - past_len).astype(f32))
        m_s[g] = m_new
        l_s[g] = l_new
        acc_s[g] = acc

    nk = pp * PAGE_SIZE
    keys = jnp.concatenate([r[0] for r in page_refs], axis=0)
    pos = j * nk + lax.broadcasted_iota(i32, (1, nk), 1)
    for g in range(N_KV):
        attend(g, keys, pos, None)

    @pl.when(j == n_groups - 1)
    def _():
        new_keys = newsel_ref[0]
        npos = past_len + lax.broadcasted_iota(i32, (1, new_keys.shape[0]), 1)
        outs = []
        gl = gl_ref[0]
        for g in range(N_KV):
            attend(g, new_keys, npos, None)
            o_sel = acc_s[g] / l_s[g]
            carry = _init_carry(rows)
            q4 = q_of(g)
            for wkeys, wbase in ((swin_ref[0], past_len - swin_ref.shape[1]), (nwin_ref[0], past_len)):
                wpos = wbase + lax.broadcasted_iota(i32, (1, wkeys.shape[0]), 1)
                dpos = t_col - wpos
                valid = (dpos >= 0) & (dpos <= WINDOW)
                k = wkeys[:, g * GRP:g * GRP + HEAD_DIM].astype(bf16)
                v = wkeys[:, g * GRP + HEAD_DIM:(g + 1) * GRP].astype(bf16)
                carry = _flash_update(carry, q4, k, v, slope_ref[g], valid, (wpos - past_len).astype(f32))
            o_win = carry[2] / carry[1]
            outs.append(_gate_mix(gl[:, g * 128:(g + 1) * 128], ocmp_s[g], o_sel, o_win, s_len))
        o_ref[0] = jnp.concatenate(outs, axis=1)


def _nsa_sample(page_table, q, gl, comp_all, bidx, new_sel, state_win, new_win, slopes, cache_sel, pp):
    bsz, s_len, hd = q.shape
    n_pages = page_table.shape[1]
    past_len = n_pages * PAGE_SIZE
    pp = min(pp, n_pages)
    n_groups = n_pages // pp
    n_c = past_len // CMP_BLOCK
    lanes = -(-(n_c + 2) // 128) * 128
    rows = HPG * s_len
    kvw = cache_sel.shape[-1]
    perb = lambda b, j, pt: (b, 0, 0)

    def page_spec(i):
        return pl.BlockSpec((1, PAGE_SIZE, kvw), lambda b, j, pt: (pt[b, j * pp + i], 0, 0))

    kern = functools.partial(_nsa_sample_kernel, s_len=s_len, past_len=past_len, pp=pp,
                             n_groups=n_groups, lanes=lanes)
    grid_spec = pltpu.PrefetchScalarGridSpec(
        num_scalar_prefetch=1,
        grid=(bsz, n_groups),
        in_specs=[
            pl.BlockSpec((1, s_len, hd), perb),
            pl.BlockSpec((1, s_len, gl.shape[-1]), perb),
            pl.BlockSpec(comp_all.shape, lambda b, j, pt: (0, 0, 0)),
            pl.BlockSpec((1, n_c, 1), perb),
            pl.BlockSpec((1,) + new_sel.shape[1:], perb),
            pl.BlockSpec((1,) + state_win.shape[1:], perb),
            pl.BlockSpec((1,) + new_win.shape[1:], perb),
            pl.BlockSpec(slopes.shape, lambda b, j, pt: (0, 0, 0)),
        ] + [page_spec(i) for i in range(pp)],
        out_specs=pl.BlockSpec((1, s_len, hd), perb),
        scratch_shapes=[
            pltpu.VMEM((N_KV, s_len, lanes), f32),
            pltpu.VMEM((N_KV, rows, HEAD_DIM), f32),
            pltpu.VMEM((N_KV, rows, 1), f32),
            pltpu.VMEM((N_KV, rows, 1), f32),
            pltpu.VMEM((N_KV, rows, HEAD_DIM), f32),
        ],
    )
    return pl.pallas_call(
        kern,
        grid_spec=grid_spec,
        out_shape=jax.ShapeDtypeStruct((bsz, s_len, hd), f32),
        compiler_params=_params(("arbitrary", "arbitrary")),
        name="nsa_sample_attn",
    )(page_table, q, gl, comp_all, bidx, new_sel, state_win, new_win, slopes, *([cache_sel] * pp))


def _out_kernel(o_ref, z_ref, x_ref, mod_ref, w_ref, lng_ref, lnb_ref, xo_ref, *, nb, tr, d):
    y = _dot((o_ref[...] * _silu(z_ref[...])).astype(bf16), w_ref[...]).reshape(nb, tr, d)
    gate = mod_ref[...][:, 2:3, :]
    xo_ref[...] = _layer_norm(DN_ALPHA * x_ref[...] + (1.0 + gate) * y, lng_ref[...], lnb_ref[...])


def _out_proj(o, z, x, mod, w_out, ln_g, ln_b, nb, tr):
    bsz, t, d = x.shape
    hd = o.shape[-1]
    nt = t // tr
    const2 = lambda b, i: (0, 0)
    kern = functools.partial(_out_kernel, nb=nb, tr=tr, d=d)
    return pl.pallas_call(
        kern,
        grid=(bsz // nb, nt),
        in_specs=[
            pl.BlockSpec((nb * tr, hd), lambda b, i: (b * nt + i, 0)),
            pl.BlockSpec((nb * tr, hd), lambda b, i: (b * nt + i, 0)),
            pl.BlockSpec((nb, tr, d), lambda b, i: (b, i, 0)),
            pl.BlockSpec((nb, 3, d), lambda b, i: (b, 0, 0)),
            pl.BlockSpec(w_out.shape, const2),
            pl.BlockSpec((1, d), const2),
            pl.BlockSpec((1, d), const2),
        ],
        out_specs=pl.BlockSpec((nb, tr, d), lambda b, i: (b, i, 0)),
        out_shape=jax.ShapeDtypeStruct((bsz, t, d), f32),
        compiler_params=_params(("arbitrary", "arbitrary")),
        name="nsa_out",
    )(o, z, x, mod, w_out, ln_g, ln_b)


def _head_slopes(nq):
    s = 2.0 ** (-8.0 * np.arange(1, N_HEADS + 1) / N_HEADS)
    s = np.repeat(s.reshape(N_KV, HPG), nq, axis=1)
    return jnp.asarray(s.reshape(N_KV, HPG * nq, 1), f32)


def kernel(x_prompt, x_sample, c_prompt, c_sample, state_h, state_conv, cache_cmp, cache_sel, state_win, page_table, w_ada, b_ada, ln_g, ln_b, w_in_a, conv_w_a, conv_b_a, w_r_a, b_r_a, w_i_a, b_i_a, lam_a, w_out_a, w_kv, phi_pe, w_phi1, b_phi1, w_phi2, b_phi2, w_in_b, b_gate_b, w_out_b):
    bp, t_len, d = x_prompt.shape
    bs, s_len, _ = x_sample.shape
    n_pages = page_table.shape[1]
    past_len = n_pages * PAGE_SIZE
    hd = N_HEADS * HEAD_DIM
    kvw = N_KV * GRP
    assert s_len <= CMP_BLOCK and t_len % 128 == 0 and w_ada.shape[0] == DEPTH == 2

    n_c = bp + bs
    pad = -n_c % 8
    c_all = jnp.concatenate([c_prompt, c_sample, jnp.zeros((pad, d), f32)], axis=0)
    mods = _ada_mod(c_all, w_ada, b_ada).reshape(DEPTH, n_c + pad, 3, d)
    mod_p = mods[:, :bp]
    mod_s = mods[:, bp:n_c]

    rg = d // N_RG_BLOCKS
    w_in0 = w_in_a[0].astype(bf16)
    w_gate = jnp.concatenate([w_r_a[0], w_i_a[0]], axis=-1).astype(bf16)
    w_out0 = w_out_a[0].astype(bf16)
    row = lambda v: v.reshape(1, -1)
    rg_args = (w_in0, conv_w_a[0], row(conv_b_a[0]), w_gate, row(b_r_a[0]), row(b_i_a[0]), row(lam_a[0]),
               w_out0, row(ln_g[0]), row(ln_b[0]))
    xp1, h_p, conv_p = _rglru_prompt(x_prompt, mod_p[0], jnp.zeros((bp, 1, d), f32),
                                     jnp.zeros((bp, CONV_W - 1, d), f32), *rg_args, tc=256)
    xs1_tm, h_s, conv_s_tm = _rglru_sample(
        x_sample.transpose(1, 0, 2), mod_s[0].transpose(1, 0, 2), state_h[0],
        state_conv[0].transpose(1, 0, 2), *rg_args)
    xs1 = xs1_tm.transpose(1, 0, 2)
    new_h_p = h_p.reshape(1, bp, d)
    new_conv_p = conv_p.reshape(1, bp, CONV_W - 1, d)
    new_h_s = h_s.reshape(1, bs, d)
    new_conv_s = conv_s_tm.transpose(1, 0, 2).reshape(1, bs, CONV_W - 1, d)

    w_b = w_in_b[0]
    w_q = w_b[:, :hd].astype(bf16)
    w_z = w_b[:, hd:2 * hd].astype(bf16)
    gpg = HPG * 3
    w_g = jnp.pad(w_b[:, 2 * hd:].reshape(d, N_KV, gpg), ((0, 0), (0, 0), (0, 128 - gpg)))
    w_g = w_g.reshape(d, N_KV * 128).astype(bf16)
    b_g = jnp.pad(b_gate_b[0].reshape(N_KV, gpg), ((0, 0), (0, 128 - gpg))).reshape(1, N_KV * 128)
    w_kvb = w_kv.astype(bf16)
    cmp_p, sel_p, win_p, selb_p, winb_p, q_p, z_p, gl_p = _proj(
        xp1, mod_p[1], w_kvb, w_q, w_z, w_g, b_g, nb=1, tr=min(512, t_len))
    cmp_s, sel_s, win_s, selb_s, winb_s, q_s, z_s, gl_s = _proj(
        xs1, mod_s[1], w_kvb, w_q, w_z, w_g, b_g, nb=bs, tr=s_len)
    per_p = lambda v: v.reshape(bp, t_len, v.shape[-1])
    per_s = lambda v: v.reshape(bs, s_len, v.shape[-1])

    pe = phi_pe.reshape(CMP_BLOCK, GRP)
    zero = jnp.zeros((CMP_BLOCK, HEAD_DIM, w_phi1.shape[-1]), f32)
    w1 = jnp.concatenate([jnp.concatenate([w_phi1[0], zero], axis=-1),
                          jnp.concatenate([zero, w_phi1[1]], axis=-1)], axis=1)
    w1 = w1.reshape(CMP_BLOCK // 2, 2 * GRP, w1.shape[-1]).astype(bf16)
    b1 = b_phi1.reshape(1, -1)
    dphi = w_phi2.shape[1]
    zero2 = jnp.zeros((dphi, HEAD_DIM), f32)
    w2 = jnp.concatenate([jnp.concatenate([w_phi2[0], zero2], axis=1),
                          jnp.concatenate([zero2, w_phi2[1]], axis=1)], axis=0).astype(bf16)
    b2 = b_phi2.reshape(1, -1)
    nc_p = t_len // CMP_BLOCK
    comp_p = _compress(cmp_p.reshape(-1, GRP), pe, w1, b1, w2, b2, nb=64, out_dtype=f32)
    comp_all = _compress(cache_cmp.reshape(-1, GRP), pe, w1, b1, w2, b2, nb=64, out_dtype=bf16)

    o_p = _nsa_prompt(per_p(q_p), per_p(gl_p), comp_p, per_p(selb_p), per_p(winb_p),
                      _head_slopes(min(128, t_len)), tq=128, tk=512, tkw=128)
    bpp = PAGE_SIZE // CMP_BLOCK
    bidx = (page_table[:, :, None] * bpp + jnp.arange(bpp, dtype=i32)).reshape(bs, n_pages * bpp, 1)
    padrows = lambda v: jnp.pad(per_s(v), ((0, 0), (0, PAGE_SIZE - s_len), (0, 0)))
    o_s = _nsa_sample(page_table, per_s(q_s), per_s(gl_s), comp_all, bidx, padrows(selb_s),
                      state_win.reshape(bs, -1, kvw), padrows(winb_s), _head_slopes(s_len),
                      cache_sel.reshape(-1, PAGE_SIZE, kvw), pp=8)

    w_out1 = w_out_b[0].astype(bf16)
    y_p = _out_proj(o_p.reshape(bp * t_len, hd), z_p, xp1, mod_p[1], w_out1, row(ln_g[1]), row(ln_b[1]),
                    nb=1, tr=min(512, t_len))
    y_s = _out_proj(o_s.reshape(bs * s_len, hd), z_s, xs1, mod_s[1], w_out1, row(ln_g[1]), row(ln_b[1]),
                    nb=bs, tr=s_len)

    kv5 = lambda v, b: v.reshape(b, -1, N_KV, 2, HEAD_DIM)
    wb = state_win.shape[1]
    new_win_p = kv5(win_p, bp)[:, -min(WINDOW, t_len):]
    new_win_s = jnp.concatenate([state_win, kv5(win_s, bs)], axis=1)[:, -wb:]
    return (y_p, y_s, kv5(cmp_p, bp), kv5(sel_p, bp), new_win_p, new_h_p, new_conv_p,
            kv5(cmp_s, bs), kv5(sel_s, bs), new_win_s, new_h_s, new_conv_s)
```

```python
import functools
import math

import numpy as np
import jax
import jax.numpy as jnp
from jax import lax
from jax.experimental import pallas as pl
from jax.experimental.pallas import tpu as pltpu

f32 = jnp.float32
bf16 = jnp.bfloat16
i32 = jnp.int32

DEPTH = 2
N_RG_BLOCKS = 8
CONV_W = 4
RG_C = 8.0
N_HEADS = 16
HEAD_DIM = 64
N_KV = 4
HPG = N_HEADS // N_KV
CMP_BLOCK = 64
N_SEL = 16
WINDOW = 512
PAGE_SIZE = 128
DN_ALPHA = (2.0 * DEPTH) ** 0.25
LN_EPS = 1e-5
NEG = -1e30
FORCED = 1e6
REMOVED = -3e38
GRP = 2 * HEAD_DIM
QGRP = HPG * HEAD_DIM
VMEM_LIMIT_BYTES = 56 * 1024 * 1024


def _params(sem):
    return pltpu.CompilerParams(dimension_semantics=sem, vmem_limit_bytes=VMEM_LIMIT_BYTES)


def _dot(a, b):
    return jnp.dot(a, b, preferred_element_type=f32)


def _dot_nt(a, b):
    return lax.dot_general(a, b, (((1,), (1,)), ((), ())), preferred_element_type=f32)


def _sigmoid(x):
    return 1.0 / (1.0 + jnp.exp(-x))


def _silu(x):
    return x * _sigmoid(x)


def _log1p(e):
    u = 1.0 + e
    dlt = u - 1.0
    return jnp.where(dlt == 0.0, e, jnp.log(u) * (e / jnp.where(dlt == 0.0, 1.0, dlt)))


def _layer_norm(x, g, b):
    mu = jnp.mean(x, axis=-1, keepdims=True)
    xc = x - mu
    var = jnp.mean(xc * xc, axis=-1, keepdims=True)
    return xc * lax.rsqrt(var + LN_EPS) * g + b


def _ada_kernel(c_ref, w_ref, b_ref, o_ref):
    a = _silu(c_ref[...])
    o_ref[0] = jnp.dot(a, w_ref[0], preferred_element_type=f32,
                       precision=lax.Precision.HIGHEST) + b_ref[0]


def _ada_mod(c_all, w_ada, b_ada):
    rows, d = c_all.shape
    depth = w_ada.shape[0]
    return pl.pallas_call(
        _ada_kernel,
        grid=(depth, 3),
        in_specs=[
            pl.BlockSpec((rows, d), lambda l, n: (0, 0)),
            pl.BlockSpec((1, d, d), lambda l, n: (l, 0, n)),
            pl.BlockSpec((1, 1, d), lambda l, n: (l, 0, n)),
        ],
        out_specs=pl.BlockSpec((1, rows, d), lambda l, n: (l, 0, n)),
        out_shape=jax.ShapeDtypeStruct((depth, rows, 3 * d), f32),
        compiler_params=_params(("arbitrary", "arbitrary")),
        name="ada_mod",
    )(c_all, w_ada, b_ada.reshape(depth, 1, 3 * d))


def _rglru_gates(xc, wg_ref, br, bi, lam):
    xcb = xc.astype(bf16)
    rg = xc.shape[1] // N_RG_BLOCKS
    rs, is_ = [], []
    for n in range(N_RG_BLOCKS):
        g = _dot(xcb[:, n * rg:(n + 1) * rg], wg_ref[n])
        rs.append(g[:, :rg])
        is_.append(g[:, rg:])
    r = _sigmoid(jnp.concatenate(rs, axis=1) + br)
    i = _sigmoid(jnp.concatenate(is_, axis=1) + bi)
    nl = -lam
    softplus = jnp.maximum(nl, 0.0) + _log1p(jnp.exp(-jnp.abs(nl)))
    log_a = (-RG_C * softplus) * r
    a = jnp.exp(log_a)
    gain = jnp.sqrt(jnp.maximum(-jnp.tanh(log_a) * (a * a + 1.0), 0.0))
    b = gain * i * xc
    return a, b


def _rglru_prompt_kernel(x_ref, mod_ref, h0_ref, c0_ref, win_ref, cw_ref, cb_ref, wg_ref, br_ref,
                         bi_ref, lam_ref, wout_ref, lng_ref, lnb_ref,
                         xo_ref, hl_ref, cl_ref, xbuf, a_s, b_s, hs_s, hc, *, tc, d):
    t = pl.program_id(1)

    @pl.when(t == 0)
    def _():
        xbuf[0:8, :] = jnp.zeros((8, d), f32)
        xbuf[8 - (CONV_W - 1):8, :] = c0_ref[0]
        hc[...] = jnp.broadcast_to(h0_ref[0], (8, d))

    x = x_ref[0]
    mod = mod_ref[0]
    shift, scale, gate = mod[0:1], mod[1:2], mod[2:3]
    m = x * (1.0 + scale) + shift
    u = _dot(m.astype(bf16), win_ref[...])
    xb = u[:, :d]
    zg = u[:, d:]
    xbuf[8:8 + tc, :] = xb
    base = 8 - (CONV_W - 1)
    xc = cb_ref[...] + xbuf[base:base + tc, :] * cw_ref[0:1, :]
    for k in range(1, CONV_W):
        xc = xc + xbuf[base + k:base + k + tc, :] * cw_ref[k:k + 1, :]
    tail = xbuf[8 + tc - (CONV_W - 1):8 + tc, :]
    xbuf[base:8, :] = tail
    cl_ref[0] = tail

    a, b = _rglru_gates(xc, wg_ref, br_ref[...], bi_ref[...], lam_ref[...])

    rowi = lax.broadcasted_iota(i32, (tc, 1), 0) % 8
    for s in (1, 2, 4):
        ok = rowi >= s
        a_sh = pltpu.roll(a, s, 0)
        b_sh = pltpu.roll(b, s, 0)
        b = jnp.where(ok, a * b_sh + b, b)
        a = jnp.where(ok, a * a_sh, a)
    a_s[...] = a
    b_s[...] = b

    def tile_step(j, hprev):
        r0 = pl.multiple_of(j * 8, 8)
        ht = a_s[pl.ds(r0, 8), :] * hprev + b_s[pl.ds(r0, 8), :]
        hs_s[pl.ds(r0, 8), :] = ht
        return jnp.broadcast_to(ht[7:8, :], (8, d))

    hlast = lax.fori_loop(0, tc // 8, tile_step, hc[...])
    hc[...] = hlast
    hl_ref[0] = hlast[0:1, :]

    y = _dot((hs_s[...] * _silu(zg)).astype(bf16), wout_ref[...])
    xo_ref[0] = _layer_norm(DN_ALPHA * x + (1.0 + gate) * y, lng_ref[...], lnb_ref[...])


def _rglru_prompt(x, mod, h0, c0, w_in, conv_w, conv_b, w_gate, b_r, b_i, lam, w_out, ln_g, ln_b, tc):
    bsz, t, d = x.shape
    tc = min(tc, t)
    const2 = lambda b, i: (0, 0)
    const3 = lambda b, i: (0, 0, 0)
    perb = lambda b, i: (b, 0, 0)
    kern = functools.partial(_rglru_prompt_kernel, tc=tc, d=d)
    return pl.pallas_call(
        kern,
        grid=(bsz, t // tc),
        in_specs=[
            pl.BlockSpec((1, tc, d), lambda b, i: (b, i, 0)),
            pl.BlockSpec((1, 3, d), perb),
            pl.BlockSpec((1, 1, d), perb),
            pl.BlockSpec((1, CONV_W - 1, d), perb),
            pl.BlockSpec(w_in.shape, const2),
            pl.BlockSpec(conv_w.shape, const2),
            pl.BlockSpec((1, d), const2),
            pl.BlockSpec(w_gate.shape, const3),
            pl.BlockSpec((1, d), const2),
            pl.BlockSpec((1, d), const2),
            pl.BlockSpec((1, d), const2),
            pl.BlockSpec(w_out.shape, const2),
            pl.BlockSpec((1, d), const2),
            pl.BlockSpec((1, d), const2),
        ],
        out_specs=[
            pl.BlockSpec((1, tc, d), lambda b, i: (b, i, 0)),
            pl.BlockSpec((1, 1, d), perb),
            pl.BlockSpec((1, CONV_W - 1, d), perb),
        ],
        out_shape=[
            jax.ShapeDtypeStruct((bsz, t, d), f32),
            jax.ShapeDtypeStruct((bsz, 1, d), f32),
            jax.ShapeDtypeStruct((bsz, CONV_W - 1, d), f32),
        ],
        scratch_shapes=[
            pltpu.VMEM((tc + 8, d), f32),
            pltpu.VMEM((tc, d), f32),
            pltpu.VMEM((tc, d), f32),
            pltpu.VMEM((tc, d), f32),
            pltpu.VMEM((8, d), f32),
        ],
        compiler_params=_params(("arbitrary", "arbitrary")),
        name="rglru_prompt",
    )(x, mod, h0, c0, w_in, conv_w, conv_b, w_gate, b_r, b_i, lam, w_out, ln_g, ln_b)


def _rglru_sample_kernel(x_ref, mod_ref, h0_ref, c0_ref, win_ref, cw_ref, cb_ref, wg_ref, br_ref,
                         bi_ref, lam_ref, wout_ref, lng_ref, lnb_ref,
                         xo_ref, hl_ref, cl_ref, *, s_len, bsz, d):
    x = x_ref[...]
    mod = mod_ref[...]
    shift, scale, gate = mod[0:1], mod[1:2], mod[2:3]
    m = x * (1.0 + scale) + shift
    u = _dot(m.reshape(s_len * bsz, d).astype(bf16), win_ref[...])
    xb = u[:, :d].reshape(s_len, bsz, d)
    zg = u[:, d:]
    xp = jnp.concatenate([c0_ref[...], xb], axis=0)
    cw = cw_ref[...]
    xc = cb_ref[...] + xp[0:s_len] * cw[0:1]
    for k in range(1, CONV_W):
        xc = xc + xp[k:k + s_len] * cw[k:k + 1]
    cl_ref[...] = xp[s_len:s_len + CONV_W - 1]
    a, b = _rglru_gates(xc.reshape(s_len * bsz, d), wg_ref, br_ref[...], bi_ref[...], lam_ref[...])
    h = h0_ref[...]
    hs = []
    for s in range(s_len):
        h = a[s * bsz:(s + 1) * bsz] * h + b[s * bsz:(s + 1) * bsz]
        hs.append(h)
    hl_ref[...] = h
    hs = jnp.concatenate(hs, axis=0)
    y = _dot((hs * _silu(zg)).astype(bf16), wout_ref[...])
    xo = _layer_norm(DN_ALPHA * x + (1.0 + gate) * y.reshape(s_len, bsz, d), lng_ref[...], lnb_ref[...])
    xo_ref[...] = xo


def _rglru_sample(x_tm, mod_tm, h0, c0_tm, w_in, conv_w, conv_b, w_gate, b_r, b_i, lam, w_out, ln_g, ln_b):
    s_len, bsz, d = x_tm.shape
    kern = functools.partial(_rglru_sample_kernel, s_len=s_len, bsz=bsz, d=d)
    return pl.pallas_call(
        kern,
        out_shape=[
            jax.ShapeDtypeStruct((s_len, bsz, d), f32),
            jax.ShapeDtypeStruct((bsz, d), f32),
            jax.ShapeDtypeStruct((CONV_W - 1, bsz, d), f32),
        ],
        compiler_params=pltpu.CompilerParams(vmem_limit_bytes=VMEM_LIMIT_BYTES),
        name="rglru_sample",
    )(x_tm, mod_tm, h0, c0_tm, w_in, conv_w, conv_b, w_gate, b_r, b_i, lam, w_out, ln_g, ln_b)


def _proj_kernel(x_ref, mod_ref, wkv_ref, wq_ref, wz_ref, wg_ref, bg_ref,
                 cmp_ref, sel_ref, win_ref, selb_ref, winb_ref, q_ref, z_ref, gl_ref, *, nb, tr, d):
    x = x_ref[...]
    mod = mod_ref[...]
    m = x * (1.0 + mod[:, 1:2, :]) + mod[:, 0:1, :]
    xb = x.reshape(nb * tr, d).astype(bf16)
    mb = m.reshape(nb * tr, d).astype(bf16)
    kv = _dot(xb, wkv_ref[...])
    w = kv.shape[1] // 3
    sel = kv[:, w:2 * w]
    win = kv[:, 2 * w:]
    cmp_ref[...] = kv[:, :w]
    sel_ref[...] = sel
    win_ref[...] = win
    selb_ref[...] = sel.astype(bf16)
    winb_ref[...] = win.astype(bf16)
    q_ref[...] = (_dot(mb, wq_ref[...]) * (HEAD_DIM ** -0.5)).astype(bf16)
    z_ref[...] = _dot(mb, wz_ref[...])
    gl_ref[...] = _dot(mb, wg_ref[...]) + bg_ref[...]


def _proj(x, mod, w_kv, w_q, w_z, w_g, b_g, nb, tr):
    bsz, t, d = x.shape
    kvw = w_kv.shape[1] // 3
    hd = w_q.shape[1]
    gw = w_g.shape[1]
    nt = t // tr
    const2 = lambda b, i: (0, 0)
    blk = lambda width: pl.BlockSpec((nb * tr, width), lambda b, i: (b * nt + i, 0))
    out = lambda width, dt: jax.ShapeDtypeStruct((bsz * t, width), dt)
    kern = functools.partial(_proj_kernel, nb=nb, tr=tr, d=d)
    return pl.pallas_call(
        kern,
        grid=(bsz // nb, nt),
        in_specs=[
            pl.BlockSpec((nb, tr, d), lambda b, i: (b, i, 0)),
            pl.BlockSpec((nb, 3, d), lambda b, i: (b, 0, 0)),
            pl.BlockSpec(w_kv.shape, const2),
            pl.BlockSpec(w_q.shape, const2),
            pl.BlockSpec(w_z.shape, const2),
            pl.BlockSpec(w_g.shape, const2),
            pl.BlockSpec((1, gw), const2),
        ],
        out_specs=[blk(kvw), blk(kvw), blk(kvw), blk(kvw), blk(kvw), blk(hd), blk(hd), blk(gw)],
        out_shape=[out(kvw, f32), out(kvw, f32), out(kvw, f32), out(kvw, bf16), out(kvw, bf16),
                   out(hd, bf16), out(hd, f32), out(gw, f32)],
        compiler_params=_params(("arbitrary", "arbitrary")),
        name="nsa_proj",
    )(x, mod, w_kv, w_q, w_z, w_g, b_g)


def _proj_prompt_kernel(x_ref, mod_ref, wkvt_ref, wq_ref, wz_ref, wg_ref, bg_ref,
                        cmpt_ref, selt_ref, wint_ref, cmppg_ref, seltt_ref, wintt_ref, q_ref, z_ref, gl_ref,
                        *, tr):
    x = x_ref[0]
    mod = mod_ref[0]
    m = x * (1.0 + mod[1:2]) + mod[0:1]
    mb = m.astype(bf16)
    kvt = _dot_nt(wkvt_ref[...], x.astype(bf16))
    kvw = kvt.shape[0] // 3
    cmpt = kvt[:kvw]
    selt = kvt[kvw:2 * kvw]
    wint = kvt[2 * kvw:]
    cmpt_ref[0] = cmpt
    selt_ref[0] = selt
    wint_ref[0] = wint
    seltt_ref[0, 0] = selt.astype(bf16)
    winb = wint.astype(bf16)
    for k in range(tr // PAGE_SIZE):
        cmppg_ref[k] = cmpt[:, k * PAGE_SIZE:(k + 1) * PAGE_SIZE]
        wintt_ref[0, k] = winb[:, k * PAGE_SIZE:(k + 1) * PAGE_SIZE]
    q_ref[...] = (_dot(mb, wq_ref[...]) * (HEAD_DIM ** -0.5)).astype(bf16)
    z_ref[...] = _dot(mb, wz_ref[...])
    gl_ref[...] = _dot(mb, wg_ref[...]) + bg_ref[...]


def _proj_prompt(x, mod, w_kvt, w_q, w_z, w_g, b_g, tr):
    bsz, t, d = x.shape
    kvw = w_kvt.shape[0] // 3
    hd = w_q.shape[1]
    gw = w_g.shape[1]
    nt = t // tr
    ppt = tr // PAGE_SIZE
    const2 = lambda b, i: (0, 0)
    rows = lambda width: pl.BlockSpec((tr, width), lambda b, i: (b * nt + i, 0))
    tmin = pl.BlockSpec((1, kvw, tr), lambda b, i: (b, 0, i))
    kern = functools.partial(_proj_prompt_kernel, tr=tr)
    return pl.pallas_call(
        kern,
        grid=(bsz, nt),
        in_specs=[
            pl.BlockSpec((1, tr, d), lambda b, i: (b, i, 0)),
            pl.BlockSpec((1, 3, d), lambda b, i: (b, 0, 0)),
            pl.BlockSpec(w_kvt.shape, const2),
            pl.BlockSpec(w_q.shape, const2),
            pl.BlockSpec(w_z.shape, const2),
            pl.BlockSpec(w_g.shape, const2),
            pl.BlockSpec((1, gw), const2),
        ],
        out_specs=[
            tmin, tmin, tmin,
            pl.BlockSpec((ppt, kvw, PAGE_SIZE), lambda b, i: (b * nt + i, 0, 0)),
            pl.BlockSpec((1, 1, kvw, tr), lambda b, i: (b, i, 0, 0)),
            pl.BlockSpec((1, ppt, kvw, PAGE_SIZE), lambda b, i: (b, i, 0, 0)),
            rows(hd), rows(hd), rows(gw),
        ],
        out_shape=[
            jax.ShapeDtypeStruct((bsz, kvw, t), f32),
            jax.ShapeDtypeStruct((bsz, kvw, t), f32),
            jax.ShapeDtypeStruct((bsz, kvw, t), f32),
            jax.ShapeDtypeStruct((bsz * t // PAGE_SIZE, kvw, PAGE_SIZE), f32),
            jax.ShapeDtypeStruct((bsz, nt, kvw, tr), bf16),
            jax.ShapeDtypeStruct((bsz, t // PAGE_SIZE, kvw, PAGE_SIZE), bf16),
            jax.ShapeDtypeStruct((bsz * t, hd), bf16),
            jax.ShapeDtypeStruct((bsz * t, hd), f32),
            jax.ShapeDtypeStruct((bsz * t, gw), f32),
        ],
        compiler_params=_params(("arbitrary", "arbitrary")),
        name="nsa_proj_prompt",
    )(x, mod, w_kvt, w_q, w_z, w_g, b_g)


def _compress_kernel(x_ref, pet_ref, w1_ref, b1_ref, w2_ref, b2_ref, kc_ref, vc_ref, *, npg):
    page_rows = N_KV * GRP
    outs = []
    for c in range(2):
        acc = None
        for dd in range(HEAD_DIM):
            lhs = jnp.concatenate(
                [(x_ref[pl.ds((g * 2 + c) * HEAD_DIM + dd, npg, stride=page_rows), :]
                  + pet_ref[c, dd:dd + 1, :]).astype(bf16) for g in range(N_KV)], axis=0)
            part = _dot(lhs, w1_ref[c, dd])
            acc = part if acc is None else acc + part
        hid = _silu(acc + b1_ref[c])
        outs.append(_dot(hid.astype(bf16), w2_ref[c]) + b2_ref[c])
    for g in range(N_KV):
        kc_ref[g] = outs[0][g * npg:(g + 1) * npg, :].astype(kc_ref.dtype)
        vc_ref[g] = outs[1][g * npg:(g + 1) * npg, :].astype(vc_ref.dtype)


def _compress(x2d, pet, w1, b1, w2, b2, npg, out_dtype):
    rows, width = x2d.shape
    n_pages = rows // (N_KV * GRP)
    npg = min(npg, n_pages)
    while n_pages % npg:
        npg -= 8
    ow = (PAGE_SIZE // CMP_BLOCK) * HEAD_DIM
    const3 = lambda i: (0, 0, 0)
    kern = functools.partial(_compress_kernel, npg=npg)
    out = pl.BlockSpec((N_KV, npg, ow), lambda i: (0, i, 0))
    return pl.pallas_call(
        kern,
        grid=(n_pages // npg,),
        in_specs=[
            pl.BlockSpec((npg * N_KV * GRP, width), lambda i: (i, 0)),
            pl.BlockSpec(pet.shape, const3),
            pl.BlockSpec(w1.shape, lambda i: (0, 0, 0, 0)),
            pl.BlockSpec(b1.shape, const3),
            pl.BlockSpec(w2.shape, const3),
            pl.BlockSpec(b2.shape, const3),
        ],
        out_specs=[out, out],
        out_shape=[jax.ShapeDtypeStruct((N_KV, n_pages, ow), out_dtype)] * 2,
        compiler_params=_params(("arbitrary",)),
        name="nsa_compress",
    )(x2d, pet, w1, b1, w2, b2)


def _block_ids(n_pages, lanes, axis):
    shape = (1, lanes) if axis == 1 else (lanes, 1)
    r = lax.broadcasted_iota(i32, shape, axis)
    bpp = PAGE_SIZE // CMP_BLOCK
    perm = jnp.where(r < n_pages, bpp * r, bpp * (r - n_pages) + 1)
    return jnp.where(r < bpp * n_pages, perm, r)


def _unpage(x):
    return jnp.concatenate([x[:, :HEAD_DIM], x[:, HEAD_DIM:]], axis=0)


def _cmp_branch(q4, kc, vc, slope, t_col, nq, jrow):
    s = _dot_nt(q4, kc)
    c_end = (jrow + 1) * CMP_BLOCK - 1
    dist = t_col.astype(f32) - c_end.astype(f32)
    s = s - slope * dist
    mask = c_end <= t_col
    s = jnp.where(mask, s, NEG)
    e = jnp.exp(s - jnp.max(s, axis=-1, keepdims=True))
    p = e / jnp.sum(e, axis=-1, keepdims=True)
    p = jnp.where(mask, p, 0.0)
    o = _dot(p.astype(bf16), vc)
    imp = p[0:nq]
    for h in range(1, HPG):
        imp = imp + p[h * nq:(h + 1) * nq]
    return o, imp


def _select_blocks(imp, tq_col, n_sb, jr):
    nq, lanes = imp.shape
    cb = tq_col // CMP_BLOCK
    forced = (jr == 0) | (jr == cb) | (jr == cb - 1)
    causal = jr <= cb
    score = jnp.where(forced, FORCED, jnp.where(causal, imp, -1.0))
    score = jnp.where(jr < n_sb, score, REMOVED)
    jf = jr.astype(f32)
    sel = jnp.zeros((nq, lanes), f32)
    for _ in range(min(N_SEL, n_sb)):
        mx = jnp.max(score, axis=-1, keepdims=True)
        idx = jnp.min(jnp.where(score == mx, jf, 1e9), axis=-1, keepdims=True)
        hit = jf == idx
        sel = jnp.where(hit, 1.0, sel)
        score = jnp.where(hit, REMOVED, score)
    return sel


def _flash_update(carry, q4, kt, vt, slope, valid, pos_rel):
    m_i, l_i, acc = carry
    s = _dot(q4, kt) + slope * pos_rel
    s = jnp.where(valid, s, NEG)
    m_new = jnp.maximum(m_i, jnp.max(s, axis=-1, keepdims=True))
    alpha = jnp.exp(m_i - m_new)
    p = jnp.exp(s - m_new)
    l_new = alpha * l_i + jnp.sum(p, axis=-1, keepdims=True)
    acc = alpha * acc + _dot_nt(p.astype(bf16), vt)
    return m_new, l_new, acc


def _block_mask(selb, jcol, pos, reps):
    onehot = (jcol == pos // CMP_BLOCK).astype(bf16)
    msk = _dot(selb, onehot)
    return jnp.concatenate([msk] * reps, axis=0) > 0.5


def _gate_mix(gl, o_cmp, o_sel, o_win, nq):
    gs = _sigmoid(gl)
    outs = []
    for h in range(HPG):
        sl = slice(h * nq, (h + 1) * nq)
        outs.append(gs[:, 3 * h:3 * h + 1] * o_cmp[sl] + gs[:, 3 * h + 1:3 * h + 2] * o_sel[sl]
                    + gs[:, 3 * h + 2:3 * h + 3] * o_win[sl])
    return jnp.concatenate(outs, axis=1)


def _init_carry(rows):
    return (jnp.full((rows, 1), NEG, f32), jnp.zeros((rows, 1), f32), jnp.zeros((rows, HEAD_DIM), f32))


def _nsa_prompt_kernel(q_ref, gl_ref, kc_ref, vc_ref, ksel_ref, kwin_ref, slope_ref, o_ref,
                       *, t_len, tq, tk, tkw):
    qi = pl.program_id(2)
    t0 = qi * tq
    rows = HPG * tq
    q = q_ref[0]
    q4 = jnp.concatenate([q[:, h * HEAD_DIM:(h + 1) * HEAD_DIM] for h in range(HPG)], axis=0)
    slope = slope_ref[0]
    t_col = t0 + lax.broadcasted_iota(i32, (rows, 1), 0) % tq
    tq_col = t0 + lax.broadcasted_iota(i32, (tq, 1), 0)

    n_c = t_len // CMP_BLOCK
    n_pages = t_len // PAGE_SIZE
    jrow = _block_ids(n_pages, n_c, 1)
    jcol = _block_ids(n_pages, n_c, 0)
    kc = _unpage(kc_ref[0]).astype(bf16)
    vc = _unpage(vc_ref[0]).astype(bf16)
    o_cmp, imp = _cmp_branch(q4, kc, vc, slope, t_col, tq, jrow)
    selb = _select_blocks(imp, tq_col, n_c, jrow).astype(bf16)

    def sel_step(kt, carry):
        kv = ksel_ref[0, kt]
        pos = kt * tk + lax.broadcasted_iota(i32, (1, tk), 1)
        valid = _block_mask(selb, jcol, pos, HPG) & (pos <= t_col)
        return _flash_update(carry, q4, kv[:HEAD_DIM], kv[HEAD_DIM:], slope, valid, (pos - t0).astype(f32))

    n_tiles = (t0 + tq + tk - 1) // tk
    _, l_i, acc = lax.fori_loop(0, n_tiles, sel_step, _init_carry(rows))
    o_sel = acc / l_i

    def win_step(kw, carry):
        kv = kwin_ref[0, kw]
        pos = kw * tkw + lax.broadcasted_iota(i32, (1, tkw), 1)
        dpos = t_col - pos
        valid = (dpos >= 0) & (dpos <= WINDOW)
        return _flash_update(carry, q4, kv[:HEAD_DIM], kv[HEAD_DIM:], slope, valid, (pos - t0).astype(f32))

    first = jnp.maximum(t0 - WINDOW, 0) // tkw
    last = (t0 + tq - 1) // tkw
    _, l_i, acc = lax.fori_loop(first, last + 1, win_step, _init_carry(rows))
    o_win = acc / l_i

    o_ref[0] = _gate_mix(gl_ref[0], o_cmp, o_sel, o_win, tq)


def _nsa_prompt(q, gl, kc, vc, ksel, kwin, slopes, tq):
    bsz, t_len, _ = q.shape
    tq = min(tq, t_len)
    tk = ksel.shape[-1]
    tkw = kwin.shape[-1]
    npb = t_len // PAGE_SIZE
    kern = functools.partial(_nsa_prompt_kernel, t_len=t_len, tq=tq, tk=tk, tkw=tkw)
    cspec = pl.BlockSpec((1, npb, kc.shape[-1]), lambda b, g, i: (g, b, 0))
    return pl.pallas_call(
        kern,
        grid=(bsz, N_KV, t_len // tq),
        in_specs=[
            pl.BlockSpec((1, tq, QGRP), lambda b, g, i: (b, i, g)),
            pl.BlockSpec((1, tq, 128), lambda b, g, i: (b, i, g)),
            cspec, cspec,
            pl.BlockSpec((1, t_len // tk, GRP, tk), lambda b, g, i: (b, 0, g, 0)),
            pl.BlockSpec((1, t_len // tkw, GRP, tkw), lambda b, g, i: (b, 0, g, 0)),
            pl.BlockSpec((1, HPG * tq, 1), lambda b, g, i: (g, 0, 0)),
        ],
        out_specs=pl.BlockSpec((1, tq, QGRP), lambda b, g, i: (b, i, g)),
        out_shape=jax.ShapeDtypeStruct((bsz, t_len, N_KV * QGRP), f32),
        compiler_params=_params(("arbitrary", "arbitrary", "arbitrary")),
        name="nsa_prompt_attn",
    )(q, gl, kc, vc, ksel, kwin, slopes)


def _nsa_sample_kernel(pt_ref, q_ref, gl_ref, kcall_ref, vcall_ref, pid_ref, newsel_ref, swin_ref, nwin_ref,
                       slope_ref, *rest, s_len, past_len, pp, n_groups, lanes):
    page_refs = rest[:pp]
    o_ref = rest[pp]
    sel_s, ocmp_s, m_s, l_s, acc_s = rest[pp + 1:]
    j = pl.program_id(1)
    rows = HPG * s_len
    n_c = past_len // CMP_BLOCK
    npb = past_len // PAGE_SIZE
    jrow = _block_ids(npb, lanes, 1)
    jcol = _block_ids(npb, lanes, 0)
    q = q_ref[0].astype(f32)
    t_col = past_len + lax.broadcasted_iota(i32, (rows, 1), 0) % s_len
    tq_col = past_len + lax.broadcasted_iota(i32, (s_len, 1), 0)

    def q_of(g):
        return jnp.concatenate(
            [q[:, (g * HPG + h) * HEAD_DIM:(g * HPG + h + 1) * HEAD_DIM] for h in range(HPG)],
            axis=0).astype(bf16)

    @pl.when(j == 0)
    def _():
        n_all = kcall_ref.shape[1]
        gather = (lax.broadcasted_iota(i32, (npb, n_all), 1) == pid_ref[0]).astype(bf16)
        for g in range(N_KV):
            kc = _unpage(_dot(gather, kcall_ref[g])).astype(bf16)
            vc = _unpage(_dot(gather, vcall_ref[g])).astype(bf16)
            o_cmp, imp = _cmp_branch(q_of(g), kc, vc, slope_ref[g], t_col, s_len, jrow[:, :n_c])
            imp = jnp.concatenate([imp, jnp.zeros((s_len, lanes - n_c), f32)], axis=1)
            sel_s[g] = _select_blocks(imp, tq_col, n_c + 1, jrow)
            ocmp_s[g] = o_cmp
            m_s[g] = jnp.full((rows, 1), NEG, f32)
            l_s[g] = jnp.zeros((rows, 1), f32)
            acc_s[g] = jnp.zeros((rows, HEAD_DIM), f32)

    def split(keys, g):
        return (keys[g * GRP:g * GRP + HEAD_DIM].astype(bf16),
                keys[g * GRP + HEAD_DIM:(g + 1) * GRP].astype(bf16))

    def attend(g, keys, pos):
        carry = (m_s[g], l_s[g], acc_s[g])
        valid = _block_mask(sel_s[g].astype(bf16), jcol, pos, HPG) & (pos <= t_col)
        kt, vt = split(keys, g)
        m_new, l_new, acc = _flash_update(carry, q_of(g), kt, vt, slope_ref[g], valid,
                                          (pos - past_len).astype(f32))
        m_s[g] = m_new
        l_s[g] = l_new
        acc_s[g] = acc

    nk = pp * PAGE_SIZE
    keys = jnp.concatenate([r[0] for r in page_refs], axis=1)
    pos = j * nk + lax.broadcasted_iota(i32, (1, nk), 1)
    for g in range(N_KV):
        attend(g, keys, pos)

    @pl.when(j == n_groups - 1)
    def _():
        new_keys = newsel_ref[0]
        npos = past_len + lax.broadcasted_iota(i32, (1, new_keys.shape[1]), 1)
        outs = []
        gl = gl_ref[0]
        for g in range(N_KV):
            attend(g, new_keys, npos)
            o_sel = acc_s[g] / l_s[g]
            carry = _init_carry(rows)
            q4 = q_of(g)
            for wkeys, wbase in ((swin_ref[0], past_len - swin_ref.shape[2]), (nwin_ref[0], past_len)):
                wpos = wbase + lax.broadcasted_iota(i32, (1, wkeys.shape[1]), 1)
                dpos = t_col - wpos
                valid = (dpos >= 0) & (dpos <= WINDOW)
                kt, vt = split(wkeys, g)
                carry = _flash_update(carry, q4, kt, vt, slope_ref[g], valid, (wpos - past_len).astype(f32))
            o_win = carry[2] / carry[1]
            outs.append(_gate_mix(gl[:, g * 128:(g + 1) * 128], ocmp_s[g], o_sel, o_win, s_len))
        o_ref[0] = jnp.concatenate(outs, axis=1)


def _nsa_sample(page_table, q, gl, kc_all, vc_all, pids, new_sel, state_win, new_win, slopes, cache_sel, pp):
    bsz, s_len, hd = q.shape
    n_pages = page_table.shape[1]
    past_len = n_pages * PAGE_SIZE
    pp = min(pp, n_pages)
    n_groups = n_pages // pp
    n_c = past_len // CMP_BLOCK
    lanes = -(-(n_c + 2) // 128) * 128
    rows = HPG * s_len
    kvw = cache_sel.shape[1]
    perb = lambda b, j, pt: (b, 0, 0)

    def page_spec(i):
        return pl.BlockSpec((1, kvw, PAGE_SIZE), lambda b, j, pt: (pt[b, j * pp + i], 0, 0))

    kern = functools.partial(_nsa_sample_kernel, s_len=s_len, past_len=past_len, pp=pp,
                             n_groups=n_groups, lanes=lanes)
    grid_spec = pltpu.PrefetchScalarGridSpec(
        num_scalar_prefetch=1,
        grid=(bsz, n_groups),
        in_specs=[
            pl.BlockSpec((1, s_len, hd), perb),
            pl.BlockSpec((1, s_len, gl.shape[-1]), perb),
            pl.BlockSpec(kc_all.shape, lambda b, j, pt: (0, 0, 0)),
            pl.BlockSpec(vc_all.shape, lambda b, j, pt: (0, 0, 0)),
            pl.BlockSpec((1, n_pages, 1), perb),
            pl.BlockSpec((1,) + new_sel.shape[1:], perb),
            pl.BlockSpec((1,) + state_win.shape[1:], perb),
            pl.BlockSpec((1,) + new_win.shape[1:], perb),
            pl.BlockSpec(slopes.shape, lambda b, j, pt: (0, 0, 0)),
        ] + [page_spec(i) for i in range(pp)],
        out_specs=pl.BlockSpec((1, s_len, hd), perb),
        scratch_shapes=[
            pltpu.VMEM((N_KV, s_len, lanes), f32),
            pltpu.VMEM((N_KV, rows, HEAD_DIM), f32),
            pltpu.VMEM((N_KV, rows, 1), f32),
            pltpu.VMEM((N_KV, rows, 1), f32),
            pltpu.VMEM((N_KV, rows, HEAD_DIM), f32),
        ],
    )
    return pl.pallas_call(
        kern,
        grid_spec=grid_spec,
        out_shape=jax.ShapeDtypeStruct((bsz, s_len, hd), f32),
        compiler_params=_params(("arbitrary", "arbitrary")),
        name="nsa_sample_attn",
    )(page_table, q, gl, kc_all, vc_all, pids, new_sel, state_win, new_win, slopes, *([cache_sel] * pp))


def _out_kernel(o_ref, z_ref, x_ref, mod_ref, w_ref, lng_ref, lnb_ref, xo_ref, *, nb, tr, d):
    y = _dot((o_ref[...] * _silu(z_ref[...])).astype(bf16), w_ref[...]).reshape(nb, tr, d)
    gate = mod_ref[...][:, 2:3, :]
    xo_ref[...] = _layer_norm(DN_ALPHA * x_ref[...] + (1.0 + gate) * y, lng_ref[...], lnb_ref[...])


def _out_proj(o, z, x, mod, w_out, ln_g, ln_b, nb, tr):
    bsz, t, d = x.shape
    hd = o.shape[-1]
    nt = t // tr
    const2 = lambda b, i: (0, 0)
    kern = functools.partial(_out_kernel, nb=nb, tr=tr, d=d)
    return pl.pallas_call(
        kern,
        grid=(bsz // nb, nt),
        in_specs=[
            pl.BlockSpec((nb * tr, hd), lambda b, i: (b * nt + i, 0)),
            pl.BlockSpec((nb * tr, hd), lambda b, i: (b * nt + i, 0)),
            pl.BlockSpec((nb, tr, d), lambda b, i: (b, i, 0)),
            pl.BlockSpec((nb, 3, d), lambda b, i: (b, 0, 0)),
            pl.BlockSpec(w_out.shape, const2),
            pl.BlockSpec((1, d), const2),
            pl.BlockSpec((1, d), const2),
        ],
        out_specs=pl.BlockSpec((nb, tr, d), lambda b, i: (b, i, 0)),
        out_shape=jax.ShapeDtypeStruct((bsz, t, d), f32),
        compiler_params=_params(("arbitrary", "arbitrary")),
        name="nsa_out",
    )(o, z, x, mod, w_out, ln_g, ln_b)


def _head_slopes(nq):
    s = 2.0 ** (-8.0 * np.arange(1, N_HEADS + 1) / N_HEADS)
    s = np.repeat(s.reshape(N_KV, HPG), nq, axis=1)
    return jnp.asarray(s.reshape(N_KV, HPG * nq, 1), f32)


def kernel(x_prompt, x_sample, c_prompt, c_sample, state_h, state_conv, cache_cmp, cache_sel, state_win, page_table, w_ada, b_ada, ln_g, ln_b, w_in_a, conv_w_a, conv_b_a, w_r_a, b_r_a, w_i_a, b_i_a, lam_a, w_out_a, w_kv, phi_pe, w_phi1, b_phi1, w_phi2, b_phi2, w_in_b, b_gate_b, w_out_b):
    bp, t_len, d = x_prompt.shape
    bs, s_len, _ = x_sample.shape
    n_pages = page_table.shape[1]
    past_len = n_pages * PAGE_SIZE
    hd = N_HEADS * HEAD_DIM
    kvw = N_KV * GRP
    assert s_len <= CMP_BLOCK and t_len % 128 == 0 and w_ada.shape[0] == DEPTH == 2

    n_c = bp + bs
    pad = -n_c % 8
    c_all = jnp.concatenate([c_prompt, c_sample, jnp.zeros((pad, d), f32)], axis=0)
    mods = _ada_mod(c_all, w_ada, b_ada).reshape(DEPTH, n_c + pad, 3, d)
    mod_p = mods[:, :bp]
    mod_s = mods[:, bp:n_c]

    rg = d // N_RG_BLOCKS
    w_in0 = w_in_a[0].astype(bf16)
    w_gate = jnp.concatenate([w_r_a[0], w_i_a[0]], axis=-1).astype(bf16)
    w_out0 = w_out_a[0].astype(bf16)
    row = lambda v: v.reshape(1, -1)
    rg_args = (w_in0, conv_w_a[0], row(conv_b_a[0]), w_gate, row(b_r_a[0]), row(b_i_a[0]), row(lam_a[0]),
               w_out0, row(ln_g[0]), row(ln_b[0]))
    xp1, h_p, conv_p = _rglru_prompt(x_prompt, mod_p[0], jnp.zeros((bp, 1, d), f32),
                                     jnp.zeros((bp, CONV_W - 1, d), f32), *rg_args, tc=256)
    xs1_tm, h_s, conv_s_tm = _rglru_sample(
        x_sample.transpose(1, 0, 2), mod_s[0].transpose(1, 0, 2), state_h[0],
        state_conv[0].transpose(1, 0, 2), *rg_args)
    xs1 = xs1_tm.transpose(1, 0, 2)
    new_h_p = h_p.reshape(1, bp, d)
    new_conv_p = conv_p.reshape(1, bp, CONV_W - 1, d)
    new_h_s = h_s.reshape(1, bs, d)
    new_conv_s = conv_s_tm.transpose(1, 0, 2).reshape(1, bs, CONV_W - 1, d)

    w_b = w_in_b[0]
    w_q = w_b[:, :hd].astype(bf16)
    w_z = w_b[:, hd:2 * hd].astype(bf16)
    gpg = HPG * 3
    w_g = jnp.pad(w_b[:, 2 * hd:].reshape(d, N_KV, gpg), ((0, 0), (0, 0), (0, 128 - gpg)))
    w_g = w_g.reshape(d, N_KV * 128).astype(bf16)
    b_g = jnp.pad(b_gate_b[0].reshape(N_KV, gpg), ((0, 0), (0, 128 - gpg))).reshape(1, N_KV * 128)
    tmin = lambda v: v.transpose(0, 2, 3, 4, 1)
    cmpt_p, selt_p, wint_p, cmp_pages, selt_tiles, wint_tiles, q_p, z_p, gl_p = _proj_prompt(
        xp1, mod_p[1], w_kv.T.astype(bf16), w_q, w_z, w_g, b_g, tr=min(512, t_len))
    cmp_s, sel_s, win_s, selb_s, winb_s, q_s, z_s, gl_s = _proj(
        xs1, mod_s[1], w_kv.astype(bf16), w_q, w_z, w_g, b_g, nb=bs, tr=s_len)
    per_p = lambda v: v.reshape(bp, t_len, v.shape[-1])
    per_s = lambda v: v.reshape(bs, s_len, v.shape[-1])

    bpp = PAGE_SIZE // CMP_BLOCK
    dphi = w_phi2.shape[1]
    diag2 = lambda w: jnp.concatenate([jnp.concatenate([w, jnp.zeros_like(w)], axis=-1),
                                       jnp.concatenate([jnp.zeros_like(w), w], axis=-1)], axis=-2)
    pet = jnp.tile(phi_pe.transpose(1, 2, 0), (1, 1, bpp))
    w1 = diag2(w_phi1.transpose(0, 2, 1, 3)).astype(bf16)
    b1 = jnp.tile(b_phi1, (1, bpp)).reshape(2, 1, bpp * dphi)
    w2 = diag2(w_phi2).astype(bf16)
    b2 = jnp.tile(b_phi2, (1, bpp)).reshape(2, 1, bpp * HEAD_DIM)
    kc_p, vc_p = _compress(cmp_pages.reshape(-1, PAGE_SIZE), pet, w1, b1, w2, b2, npg=32, out_dtype=f32)
    kc_all, vc_all = _compress(tmin(cache_cmp).reshape(-1, PAGE_SIZE), pet, w1, b1, w2, b2,
                               npg=32, out_dtype=bf16)

    o_p = _nsa_prompt(per_p(q_p), per_p(gl_p), kc_p, vc_p, selt_tiles, wint_tiles,
                      _head_slopes(min(128, t_len)), tq=128)
    tpad = lambda v: jnp.pad(per_s(v), ((0, 0), (0, PAGE_SIZE - s_len), (0, 0))).transpose(0, 2, 1)
    o_s = _nsa_sample(page_table, per_s(q_s), per_s(gl_s), kc_all, vc_all, page_table.reshape(bs, n_pages, 1),
                      tpad(selb_s), tmin(state_win).reshape(bs, kvw, -1), tpad(winb_s), _head_slopes(s_len),
                      tmin(cache_sel).reshape(-1, kvw, PAGE_SIZE), pp=8)

    w_out1 = w_out_b[0].astype(bf16)
    y_p = _out_proj(o_p.reshape(bp * t_len, hd), z_p, xp1, mod_p[1], w_out1, row(ln_g[1]), row(ln_b[1]),
                    nb=1, tr=min(512, t_len))
    y_s = _out_proj(o_s.reshape(bs * s_len, hd), z_s, xs1, mod_s[1], w_out1, row(ln_g[1]), row(ln_b[1]),
                    nb=bs, tr=s_len)

    kv5 = lambda v: v.reshape(bs, s_len, N_KV, 2, HEAD_DIM)
    kv5t = lambda v: v.reshape(bp, N_KV, 2, HEAD_DIM, -1).transpose(0, 4, 1, 2, 3)
    wb = state_win.shape[1]
    new_win_p = kv5t(wint_p[:, :, -min(WINDOW, t_len):])
    new_win_s = jnp.concatenate([state_win, kv5(win_s)], axis=1)[:, -wb:]
    return (y_p, y_s, kv5t(cmpt_p), kv5t(selt_p), new_win_p, new_h_p, new_conv_p,
            kv5(cmp_s), kv5(sel_s), new_win_s, new_h_s, new_conv_s)
```

```python
import functools
import math

import numpy as np
import jax
import jax.numpy as jnp
from jax import lax
from jax.experimental import pallas as pl
from jax.experimental.pallas import tpu as pltpu

f32 = jnp.float32
bf16 = jnp.bfloat16
i32 = jnp.int32

DEPTH = 2
N_RG_BLOCKS = 8
CONV_W = 4
RG_C = 8.0
N_HEADS = 16
HEAD_DIM = 64
N_KV = 4
HPG = N_HEADS // N_KV
CMP_BLOCK = 64
N_SEL = 16
WINDOW = 512
PAGE_SIZE = 128
DN_ALPHA = (2.0 * DEPTH) ** 0.25
LN_EPS = 1e-5
NEG = -1e30
FORCED = 1e6
REMOVED = -3e38
GRP = 2 * HEAD_DIM
QGRP = HPG * HEAD_DIM
VMEM_LIMIT_BYTES = 56 * 1024 * 1024


def _params(sem):
    return pltpu.CompilerParams(dimension_semantics=sem, vmem_limit_bytes=VMEM_LIMIT_BYTES)


def _dot(a, b):
    return jnp.dot(a, b, preferred_element_type=f32)


def _dot_nt(a, b):
    return lax.dot_general(a, b, (((1,), (1,)), ((), ())), preferred_element_type=f32)


def _sigmoid(x):
    return 1.0 / (1.0 + jnp.exp(-x))


def _silu(x):
    return x * _sigmoid(x)


def _log1p(e):
    u = 1.0 + e
    dlt = u - 1.0
    return jnp.where(dlt == 0.0, e, jnp.log(u) * (e / jnp.where(dlt == 0.0, 1.0, dlt)))


def _layer_norm(x, g, b):
    mu = jnp.mean(x, axis=-1, keepdims=True)
    xc = x - mu
    var = jnp.mean(xc * xc, axis=-1, keepdims=True)
    return xc * lax.rsqrt(var + LN_EPS) * g + b


def _ada_kernel(c_ref, w_ref, b_ref, o_ref):
    a = _silu(c_ref[...])
    o_ref[0] = jnp.dot(a, w_ref[0], preferred_element_type=f32,
                       precision=lax.Precision.HIGHEST) + b_ref[0]


def _ada_mod(c_all, w_ada, b_ada):
    rows, d = c_all.shape
    depth = w_ada.shape[0]
    return pl.pallas_call(
        _ada_kernel,
        grid=(depth, 3),
        in_specs=[
            pl.BlockSpec((rows, d), lambda l, n: (0, 0)),
            pl.BlockSpec((1, d, d), lambda l, n: (l, 0, n)),
            pl.BlockSpec((1, 1, d), lambda l, n: (l, 0, n)),
        ],
        out_specs=pl.BlockSpec((1, rows, d), lambda l, n: (l, 0, n)),
        out_shape=jax.ShapeDtypeStruct((depth, rows, 3 * d), f32),
        compiler_params=_params(("arbitrary", "arbitrary")),
        name="ada_mod",
    )(c_all, w_ada, b_ada.reshape(depth, 1, 3 * d))


def _rglru_gates(xc, wg_ref, br, bi, lam):
    xcb = xc.astype(bf16)
    rg = xc.shape[1] // N_RG_BLOCKS
    rs, is_ = [], []
    for n in range(N_RG_BLOCKS):
        g = _dot(xcb[:, n * rg:(n + 1) * rg], wg_ref[n])
        rs.append(g[:, :rg])
        is_.append(g[:, rg:])
    r = _sigmoid(jnp.concatenate(rs, axis=1) + br)
    i = _sigmoid(jnp.concatenate(is_, axis=1) + bi)
    nl = -lam
    softplus = jnp.maximum(nl, 0.0) + _log1p(jnp.exp(-jnp.abs(nl)))
    log_a = (-RG_C * softplus) * r
    a = jnp.exp(log_a)
    gain = jnp.sqrt(jnp.maximum(-jnp.tanh(log_a) * (a * a + 1.0), 0.0))
    b = gain * i * xc
    return a, b


def _rglru_prompt_kernel(x_ref, mod_ref, h0_ref, c0_ref, win_ref, cw_ref, cb_ref, wg_ref, br_ref,
                         bi_ref, lam_ref, wout_ref, lng_ref, lnb_ref,
                         xo_ref, hl_ref, cl_ref, xbuf, a_s, b_s, hs_s, hc, *, tc, d):
    t = pl.program_id(1)

    @pl.when(t == 0)
    def _():
        xbuf[0:8, :] = jnp.zeros((8, d), f32)
        xbuf[8 - (CONV_W - 1):8, :] = c0_ref[0]
        hc[...] = jnp.broadcast_to(h0_ref[0], (8, d))

    x = x_ref[0]
    mod = mod_ref[0]
    shift, scale, gate = mod[0:1], mod[1:2], mod[2:3]
    m = x * (1.0 + scale) + shift
    u = _dot(m.astype(bf16), win_ref[...])
    xb = u[:, :d]
    zg = u[:, d:]
    xbuf[8:8 + tc, :] = xb
    base = 8 - (CONV_W - 1)
    xc = cb_ref[...] + xbuf[base:base + tc, :] * cw_ref[0:1, :]
    for k in range(1, CONV_W):
        xc = xc + xbuf[base + k:base + k + tc, :] * cw_ref[k:k + 1, :]
    tail = xbuf[8 + tc - (CONV_W - 1):8 + tc, :]
    xbuf[base:8, :] = tail
    cl_ref[0] = tail

    a, b = _rglru_gates(xc, wg_ref, br_ref[...], bi_ref[...], lam_ref[...])

    rowi = lax.broadcasted_iota(i32, (tc, 1), 0) % 8
    for s in (1, 2, 4):
        ok = rowi >= s
        a_sh = pltpu.roll(a, s, 0)
        b_sh = pltpu.roll(b, s, 0)
        b = jnp.where(ok, a * b_sh + b, b)
        a = jnp.where(ok, a * a_sh, a)
    a_s[...] = a
    b_s[...] = b

    def tile_step(j, hprev):
        r0 = pl.multiple_of(j * 8, 8)
        ht = a_s[pl.ds(r0, 8), :] * hprev + b_s[pl.ds(r0, 8), :]
        hs_s[pl.ds(r0, 8), :] = ht
        return jnp.broadcast_to(ht[7:8, :], (8, d))

    hlast = lax.fori_loop(0, tc // 8, tile_step, hc[...])
    hc[...] = hlast
    hl_ref[0] = hlast[0:1, :]

    y = _dot((hs_s[...] * _silu(zg)).astype(bf16), wout_ref[...])
    xo_ref[0] = _layer_norm(DN_ALPHA * x + (1.0 + gate) * y, lng_ref[...], lnb_ref[...])


def _rglru_prompt(x, mod, h0, c0, w_in, conv_w, conv_b, w_gate, b_r, b_i, lam, w_out, ln_g, ln_b, tc):
    bsz, t, d = x.shape
    tc = min(tc, t)
    const2 = lambda b, i: (0, 0)
    const3 = lambda b, i: (0, 0, 0)
    perb = lambda b, i: (b, 0, 0)
    kern = functools.partial(_rglru_prompt_kernel, tc=tc, d=d)
    return pl.pallas_call(
        kern,
        grid=(bsz, t // tc),
        in_specs=[
            pl.BlockSpec((1, tc, d), lambda b, i: (b, i, 0)),
            pl.BlockSpec((1, 3, d), perb),
            pl.BlockSpec((1, 1, d), perb),
            pl.BlockSpec((1, CONV_W - 1, d), perb),
            pl.BlockSpec(w_in.shape, const2),
            pl.BlockSpec(conv_w.shape, const2),
            pl.BlockSpec((1, d), const2),
            pl.BlockSpec(w_gate.shape, const3),
            pl.BlockSpec((1, d), const2),
            pl.BlockSpec((1, d), const2),
            pl.BlockSpec((1, d), const2),
            pl.BlockSpec(w_out.shape, const2),
            pl.BlockSpec((1, d), const2),
            pl.BlockSpec((1, d), const2),
        ],
        out_specs=[
            pl.BlockSpec((1, tc, d), lambda b, i: (b, i, 0)),
            pl.BlockSpec((1, 1, d), perb),
            pl.BlockSpec((1, CONV_W - 1, d), perb),
        ],
        out_shape=[
            jax.ShapeDtypeStruct((bsz, t, d), f32),
            jax.ShapeDtypeStruct((bsz, 1, d), f32),
            jax.ShapeDtypeStruct((bsz, CONV_W - 1, d), f32),
        ],
        scratch_shapes=[
            pltpu.VMEM((tc + 8, d), f32),
            pltpu.VMEM((tc, d), f32),
            pltpu.VMEM((tc, d), f32),
            pltpu.VMEM((tc, d), f32),
            pltpu.VMEM((8, d), f32),
        ],
        compiler_params=_params(("arbitrary", "arbitrary")),
        name="rglru_prompt",
    )(x, mod, h0, c0, w_in, conv_w, conv_b, w_gate, b_r, b_i, lam, w_out, ln_g, ln_b)


def _rglru_sample_kernel(x_ref, mod_ref, h0_ref, c0_ref, win_ref, cw_ref, cb_ref, wg_ref, br_ref,
                         bi_ref, lam_ref, wout_ref, lng_ref, lnb_ref,
                         xo_ref, hl_ref, cl_ref, *, s_len, bsz, d):
    x = x_ref[...]
    mod = mod_ref[...]
    shift, scale, gate = mod[0:1], mod[1:2], mod[2:3]
    m = x * (1.0 + scale) + shift
    u = _dot(m.reshape(s_len * bsz, d).astype(bf16), win_ref[...])
    xb = u[:, :d].reshape(s_len, bsz, d)
    zg = u[:, d:]
    xp = jnp.concatenate([c0_ref[...], xb], axis=0)
    cw = cw_ref[...]
    xc = cb_ref[...] + xp[0:s_len] * cw[0:1]
    for k in range(1, CONV_W):
        xc = xc + xp[k:k + s_len] * cw[k:k + 1]
    cl_ref[...] = xp[s_len:s_len + CONV_W - 1]
    a, b = _rglru_gates(xc.reshape(s_len * bsz, d), wg_ref, br_ref[...], bi_ref[...], lam_ref[...])
    h = h0_ref[...]
    hs = []
    for s in range(s_len):
        h = a[s * bsz:(s + 1) * bsz] * h + b[s * bsz:(s + 1) * bsz]
        hs.append(h)
    hl_ref[...] = h
    hs = jnp.concatenate(hs, axis=0)
    y = _dot((hs * _silu(zg)).astype(bf16), wout_ref[...])
    xo = _layer_norm(DN_ALPHA * x + (1.0 + gate) * y.reshape(s_len, bsz, d), lng_ref[...], lnb_ref[...])
    xo_ref[...] = xo


def _rglru_sample(x_tm, mod_tm, h0, c0_tm, w_in, conv_w, conv_b, w_gate, b_r, b_i, lam, w_out, ln_g, ln_b):
    s_len, bsz, d = x_tm.shape
    kern = functools.partial(_rglru_sample_kernel, s_len=s_len, bsz=bsz, d=d)
    return pl.pallas_call(
        kern,
        out_shape=[
            jax.ShapeDtypeStruct((s_len, bsz, d), f32),
            jax.ShapeDtypeStruct((bsz, d), f32),
            jax.ShapeDtypeStruct((CONV_W - 1, bsz, d), f32),
        ],
        compiler_params=pltpu.CompilerParams(vmem_limit_bytes=VMEM_LIMIT_BYTES),
        name="rglru_sample",
    )(x_tm, mod_tm, h0, c0_tm, w_in, conv_w, conv_b, w_gate, b_r, b_i, lam, w_out, ln_g, ln_b)


def _proj_kernel(x_ref, mod_ref, wkv_ref, wq_ref, wz_ref, wg_ref, bg_ref,
                 cmp_ref, sel_ref, win_ref, selb_ref, winb_ref, q_ref, z_ref, gl_ref, *, nb, tr, d):
    x = x_ref[...]
    mod = mod_ref[...]
    m = x * (1.0 + mod[:, 1:2, :]) + mod[:, 0:1, :]
    xb = x.reshape(nb * tr, d).astype(bf16)
    mb = m.reshape(nb * tr, d).astype(bf16)
    kv = _dot(xb, wkv_ref[...])
    w = kv.shape[1] // 3
    sel = kv[:, w:2 * w]
    win = kv[:, 2 * w:]
    cmp_ref[...] = kv[:, :w]
    sel_ref[...] = sel
    win_ref[...] = win
    selb_ref[...] = sel.astype(bf16)
    winb_ref[...] = win.astype(bf16)
    q_ref[...] = (_dot(mb, wq_ref[...]) * (HEAD_DIM ** -0.5)).astype(bf16)
    z_ref[...] = _dot(mb, wz_ref[...])
    gl_ref[...] = _dot(mb, wg_ref[...]) + bg_ref[...]


def _proj(x, mod, w_kv, w_q, w_z, w_g, b_g, nb, tr):
    bsz, t, d = x.shape
    kvw = w_kv.shape[1] // 3
    hd = w_q.shape[1]
    gw = w_g.shape[1]
    nt = t // tr
    const2 = lambda b, i: (0, 0)
    blk = lambda width: pl.BlockSpec((nb * tr, width), lambda b, i: (b * nt + i, 0))
    out = lambda width, dt: jax.ShapeDtypeStruct((bsz * t, width), dt)
    kern = functools.partial(_proj_kernel, nb=nb, tr=tr, d=d)
    return pl.pallas_call(
        kern,
        grid=(bsz // nb, nt),
        in_specs=[
            pl.BlockSpec((nb, tr, d), lambda b, i: (b, i, 0)),
            pl.BlockSpec((nb, 3, d), lambda b, i: (b, 0, 0)),
            pl.BlockSpec(w_kv.shape, const2),
            pl.BlockSpec(w_q.shape, const2),
            pl.BlockSpec(w_z.shape, const2),
            pl.BlockSpec(w_g.shape, const2),
            pl.BlockSpec((1, gw), const2),
        ],
        out_specs=[blk(kvw), blk(kvw), blk(kvw), blk(kvw), blk(kvw), blk(hd), blk(hd), blk(gw)],
        out_shape=[out(kvw, f32), out(kvw, f32), out(kvw, f32), out(kvw, bf16), out(kvw, bf16),
                   out(hd, bf16), out(hd, f32), out(gw, f32)],
        compiler_params=_params(("arbitrary", "arbitrary")),
        name="nsa_proj",
    )(x, mod, w_kv, w_q, w_z, w_g, b_g)


def _proj_prompt_kernel(x_ref, mod_ref, wkvt_ref, wq_ref, wz_ref, wg_ref, bg_ref, e_ref, prow_ref, ones_ref,
                        cmpt_ref, selt_ref, wint_ref, cmppg_ref, kaug_ref, vaug_ref, wintt_ref, q_ref, z_ref,
                        gl_ref, *, tr):
    x = x_ref[0]
    mod = mod_ref[0]
    m = x * (1.0 + mod[1:2]) + mod[0:1]
    mb = m.astype(bf16)
    kvt = _dot_nt(wkvt_ref[...], x.astype(bf16))
    kvw = kvt.shape[0] // 3
    cmpt = kvt[:kvw]
    selt = kvt[kvw:2 * kvw]
    wint = kvt[2 * kvw:]
    cmpt_ref[0] = cmpt
    selt_ref[0] = selt
    wint_ref[0] = wint
    selb = selt.astype(bf16)
    n_c = e_ref.shape[1]
    for g in range(N_KV):
        kaug_ref[0, g, 0, 0:n_c] = e_ref[0]
        kaug_ref[0, g, 0, n_c:n_c + HEAD_DIM] = selb[g * GRP:g * GRP + HEAD_DIM]
        kaug_ref[0, g, 0, n_c + HEAD_DIM:] = prow_ref[...]
        vaug_ref[0, g, 0, 0:HEAD_DIM] = selb[g * GRP + HEAD_DIM:(g + 1) * GRP]
        vaug_ref[0, g, 0, HEAD_DIM:] = ones_ref[...]
    winb = wint.astype(bf16)
    for k in range(tr // PAGE_SIZE):
        cmppg_ref[k] = cmpt[:, k * PAGE_SIZE:(k + 1) * PAGE_SIZE]
        wintt_ref[0, k] = winb[:, k * PAGE_SIZE:(k + 1) * PAGE_SIZE]
    q_ref[...] = (_dot(mb, wq_ref[...]) * (HEAD_DIM ** -0.5)).astype(bf16)
    z_ref[...] = _dot(mb, wz_ref[...])
    gl_ref[...] = _dot(mb, wg_ref[...]) + bg_ref[...]


def _proj_prompt(x, mod, w_kvt, w_q, w_z, w_g, b_g, e, prow, ones, tr):
    bsz, t, d = x.shape
    kvw = w_kvt.shape[0] // 3
    hd = w_q.shape[1]
    gw = w_g.shape[1]
    nt = t // tr
    ppt = tr // PAGE_SIZE
    ka_rows = e.shape[1] + HEAD_DIM + prow.shape[0]
    va_rows = HEAD_DIM + ones.shape[0]
    const2 = lambda b, i: (0, 0)
    rows = lambda width: pl.BlockSpec((tr, width), lambda b, i: (b * nt + i, 0))
    tmin = pl.BlockSpec((1, kvw, tr), lambda b, i: (b, 0, i))
    kern = functools.partial(_proj_prompt_kernel, tr=tr)
    return pl.pallas_call(
        kern,
        grid=(bsz, nt),
        in_specs=[
            pl.BlockSpec((1, tr, d), lambda b, i: (b, i, 0)),
            pl.BlockSpec((1, 3, d), lambda b, i: (b, 0, 0)),
            pl.BlockSpec(w_kvt.shape, const2),
            pl.BlockSpec(w_q.shape, const2),
            pl.BlockSpec(w_z.shape, const2),
            pl.BlockSpec(w_g.shape, const2),
            pl.BlockSpec((1, gw), const2),
            pl.BlockSpec((1,) + e.shape[1:], lambda b, i: (i, 0, 0)),
            pl.BlockSpec(prow.shape, const2),
            pl.BlockSpec(ones.shape, const2),
        ],
        out_specs=[
            tmin, tmin, tmin,
            pl.BlockSpec((ppt, kvw, PAGE_SIZE), lambda b, i: (b * nt + i, 0, 0)),
            pl.BlockSpec((1, N_KV, 1, ka_rows, tr), lambda b, i: (b, 0, i, 0, 0)),
            pl.BlockSpec((1, N_KV, 1, va_rows, tr), lambda b, i: (b, 0, i, 0, 0)),
            pl.BlockSpec((1, ppt, kvw, PAGE_SIZE), lambda b, i: (b, i, 0, 0)),
            rows(hd), rows(hd), rows(gw),
        ],
        out_shape=[
            jax.ShapeDtypeStruct((bsz, kvw, t), f32),
            jax.ShapeDtypeStruct((bsz, kvw, t), f32),
            jax.ShapeDtypeStruct((bsz, kvw, t), f32),
            jax.ShapeDtypeStruct((bsz * t // PAGE_SIZE, kvw, PAGE_SIZE), f32),
            jax.ShapeDtypeStruct((bsz, N_KV, nt, ka_rows, tr), bf16),
            jax.ShapeDtypeStruct((bsz, N_KV, nt, va_rows, tr), bf16),
            jax.ShapeDtypeStruct((bsz, t // PAGE_SIZE, kvw, PAGE_SIZE), bf16),
            jax.ShapeDtypeStruct((bsz * t, hd), bf16),
            jax.ShapeDtypeStruct((bsz * t, hd), f32),
            jax.ShapeDtypeStruct((bsz * t, gw), f32),
        ],
        compiler_params=_params(("arbitrary", "arbitrary")),
        name="nsa_proj_prompt",
    )(x, mod, w_kvt, w_q, w_z, w_g, b_g, e, prow, ones)


def _compress_kernel(x_ref, pet_ref, w1_ref, b1_ref, w2_ref, b2_ref, kc_ref, vc_ref, *, npg):
    page_rows = N_KV * GRP
    outs = []
    for c in range(2):
        acc = None
        for dd in range(HEAD_DIM):
            lhs = jnp.concatenate(
                [(x_ref[pl.ds((g * 2 + c) * HEAD_DIM + dd, npg, stride=page_rows), :]
                  + pet_ref[c, dd:dd + 1, :]).astype(bf16) for g in range(N_KV)], axis=0)
            part = _dot(lhs, w1_ref[c, dd])
            acc = part if acc is None else acc + part
        hid = _silu(acc + b1_ref[c])
        outs.append(_dot(hid.astype(bf16), w2_ref[c]) + b2_ref[c])
    for g in range(N_KV):
        kc_ref[g] = outs[0][g * npg:(g + 1) * npg, :].astype(kc_ref.dtype)
        vc_ref[g] = outs[1][g * npg:(g + 1) * npg, :].astype(vc_ref.dtype)


def _compress(x2d, pet, w1, b1, w2, b2, npg, out_dtype):
    rows, width = x2d.shape
    n_pages = rows // (N_KV * GRP)
    npg = min(npg, n_pages)
    while n_pages % npg:
        npg -= 8
    ow = (PAGE_SIZE // CMP_BLOCK) * HEAD_DIM
    const3 = lambda i: (0, 0, 0)
    kern = functools.partial(_compress_kernel, npg=npg)
    out = pl.BlockSpec((N_KV, npg, ow), lambda i: (0, i, 0))
    return pl.pallas_call(
        kern,
        grid=(n_pages // npg,),
        in_specs=[
            pl.BlockSpec((npg * N_KV * GRP, width), lambda i: (i, 0)),
            pl.BlockSpec(pet.shape, const3),
            pl.BlockSpec(w1.shape, lambda i: (0, 0, 0, 0)),
            pl.BlockSpec(b1.shape, const3),
            pl.BlockSpec(w2.shape, const3),
            pl.BlockSpec(b2.shape, const3),
        ],
        out_specs=[out, out],
        out_shape=[jax.ShapeDtypeStruct((N_KV, n_pages, ow), out_dtype)] * 2,
        compiler_params=_params(("arbitrary",)),
        name="nsa_compress",
    )(x2d, pet, w1, b1, w2, b2)


def _block_ids(n_pages, lanes, axis):
    shape = (1, lanes) if axis == 1 else (lanes, 1)
    r = lax.broadcasted_iota(i32, shape, axis)
    bpp = PAGE_SIZE // CMP_BLOCK
    perm = jnp.where(r < n_pages, bpp * r, bpp * (r - n_pages) + 1)
    return jnp.where(r < bpp * n_pages, perm, r)


def _unpage(x):
    return jnp.concatenate([x[:, :HEAD_DIM], x[:, HEAD_DIM:]], axis=0)


def _cmp_branch(q4, kc, vc, slope, t_col, nq, jrow):
    s = _dot_nt(q4, kc)
    c_end = (jrow + 1) * CMP_BLOCK - 1
    dist = t_col.astype(f32) - c_end.astype(f32)
    s = s - slope * dist
    mask = c_end <= t_col
    s = jnp.where(mask, s, NEG)
    e = jnp.exp(s - jnp.max(s, axis=-1, keepdims=True))
    p = e / jnp.sum(e, axis=-1, keepdims=True)
    p = jnp.where(mask, p, 0.0)
    o = _dot(p.astype(bf16), vc)
    imp = p[0:nq]
    for h in range(1, HPG):
        imp = imp + p[h * nq:(h + 1) * nq]
    return o, imp


def _select_blocks(imp, tq_col, n_sb, jr):
    nq, lanes = imp.shape
    cb = tq_col // CMP_BLOCK
    forced = (jr == 0) | (jr == cb) | (jr == cb - 1)
    causal = jr <= cb
    score = jnp.where(forced, FORCED, jnp.where(causal, imp, -1.0))
    score = jnp.where(jr < n_sb, score, REMOVED)
    jf = jr.astype(f32)
    sel = jnp.zeros((nq, lanes), f32)
    for _ in range(min(N_SEL, n_sb)):
        mx = jnp.max(score, axis=-1, keepdims=True)
        idx = jnp.min(jnp.where(score == mx, jf, 1e9), axis=-1, keepdims=True)
        hit = jf == idx
        sel = jnp.where(hit, 1.0, sel)
        score = jnp.where(hit, REMOVED, score)
    return sel


def _flash_update(carry, q4, kt, vt, slope, valid, pos_rel):
    m_i, l_i, acc = carry
    s = _dot(q4, kt) + slope * pos_rel
    s = jnp.where(valid, s, NEG)
    m_new = jnp.maximum(m_i, jnp.max(s, axis=-1, keepdims=True))
    alpha = jnp.exp(m_i - m_new)
    p = jnp.exp(s - m_new)
    l_new = alpha * l_i + jnp.sum(p, axis=-1, keepdims=True)
    acc = alpha * acc + _dot_nt(p.astype(bf16), vt)
    return m_new, l_new, acc


def _block_mask(selb, jcol, pos, reps):
    onehot = (jcol == pos // CMP_BLOCK).astype(bf16)
    msk = _dot(selb, onehot)
    return jnp.concatenate([msk] * reps, axis=0) > 0.5


def _gate_mix(gl, o_cmp, o_sel, o_win, nq):
    gs = _sigmoid(gl)
    outs = []
    for h in range(HPG):
        sl = slice(h * nq, (h + 1) * nq)
        outs.append(gs[:, 3 * h:3 * h + 1] * o_cmp[sl] + gs[:, 3 * h + 1:3 * h + 2] * o_sel[sl]
                    + gs[:, 3 * h + 2:3 * h + 3] * o_win[sl])
    return jnp.concatenate(outs, axis=1)


def _init_carry(rows):
    return (jnp.full((rows, 1), NEG, f32), jnp.zeros((rows, 1), f32), jnp.zeros((rows, HEAD_DIM), f32))


def _select_blocks_t(impt, tq_row, n_sb, jcol):
    n_c, nq = impt.shape
    cb = tq_row // CMP_BLOCK
    forced = (jcol == 0) | (jcol == cb) | (jcol == cb - 1)
    causal = jcol <= cb
    score = jnp.where(forced, FORCED, jnp.where(causal, impt, -1.0))
    score = jnp.where(jcol < n_sb, score, REMOVED)
    jf = jcol.astype(f32)
    sel = jnp.zeros((n_c, nq), f32)
    for _ in range(min(N_SEL, n_sb)):
        mx = jnp.max(score, axis=0, keepdims=True)
        idx = jnp.min(jnp.where(score == mx, jf, 1e9), axis=0, keepdims=True)
        hit = jf == idx
        sel = jnp.where(hit, 1.0, sel)
        score = jnp.where(hit, REMOVED, score)
    return sel


def _nsa_prompt_kernel(q_ref, gl_ref, kc_ref, vc_ref, kaug_ref, vaug_ref, kwin_ref, scol_ref, srow_ref, coef_ref,
                       wbias_ref, cbias_ref, eye_ref, o_ref, s0_s, s1_s, p0_s, p1_s, a0_s, a1_s, m_s, acc_s,
                       *, t_len, tq, tk, tkw):
    qi = pl.program_id(2)
    t0 = qi * tq
    rows = HPG * tq
    q = q_ref[0]
    q4 = jnp.concatenate([q[:, h * HEAD_DIM:(h + 1) * HEAD_DIM] for h in range(HPG)], axis=0)
    slope = scol_ref[0]
    t_col = t0 + lax.broadcasted_iota(i32, (rows, 1), 0) % tq

    n_c = t_len // CMP_BLOCK
    n_pages = t_len // PAGE_SIZE
    jrow = _block_ids(n_pages, n_c, 1)
    jcol = _block_ids(n_pages, n_c, 0)
    kc = _unpage(kc_ref[0]).astype(bf16)
    vc = _unpage(vc_ref[0]).astype(bf16)
    o_cmp, _ = _cmp_branch(q4, kc, vc, slope, t_col, tq, jrow)

    t_row = t0 + lax.broadcasted_iota(i32, (1, rows), 1) % tq
    c_end = (jcol + 1) * CMP_BLOCK - 1
    st = _dot_nt(kc, q4) - srow_ref[0] * (t_row.astype(f32) - c_end.astype(f32))
    maskt = c_end <= t_row
    st = jnp.where(maskt, st, NEG)
    et = jnp.exp(st - jnp.max(st, axis=0, keepdims=True))
    pt = et / jnp.sum(et, axis=0, keepdims=True)
    pt = jnp.where(maskt, pt, 0.0)
    impt = pt[:, 0:tq]
    for h in range(1, HPG):
        impt = impt + pt[:, h * tq:(h + 1) * tq]
    selt = _select_blocks_t(impt, t0 + lax.broadcasted_iota(i32, (1, tq), 1), n_c, jcol)
    sel = _dot_nt(eye_ref[...], selt.astype(bf16))
    selbias = ((sel - 1.0) * -NEG).astype(bf16)
    q_aug = jnp.concatenate([jnp.concatenate([selbias] * HPG, axis=0), q4, coef_ref[0]], axis=1)

    nslots = (WINDOW + tq) // tkw
    base = (t0 - WINDOW) // tkw
    kts, vts, pens = [], [], []
    for m in range(nslots):
        kw = base + m
        kvw = kwin_ref[0, jnp.maximum(kw, 0)]
        kts.append(kvw[:HEAD_DIM])
        vts.append(kvw[HEAD_DIM:])
        pens.append(jnp.broadcast_to(jnp.where(kw >= 0, 0.0, NEG).astype(f32), (1, tkw)))
    sw = _dot(q4, jnp.concatenate(kts, axis=1)) + wbias_ref[0] + jnp.concatenate(pens, axis=1)
    pw = jnp.exp(sw - jnp.max(sw, axis=-1, keepdims=True))
    o_win = _dot_nt(pw.astype(bf16), jnp.concatenate(vts, axis=1)) / jnp.sum(pw, axis=-1, keepdims=True)

    m_s[...] = jnp.full((rows, 1), NEG, f32)
    acc_s[...] = jnp.zeros(acc_s.shape, f32)

    def scores_into(dst, kt):
        dst[...] = _dot(q_aug, kaug_ref[0, 0, kt])

    def softmax_into(s, kt, p_dst, a_dst):
        c = slope * (kt * tk - t0).astype(f32)
        m_i = m_s[...]
        m_new = jnp.maximum(m_i, jnp.max(s, axis=-1, keepdims=True) + c)
        p_dst[...] = jnp.exp(s - (m_new - c)).astype(bf16)
        a_dst[...] = jnp.exp(m_i - m_new)
        m_s[...] = m_new

    def values_from(p_src, a_src, kt):
        acc_s[...] = a_src[...] * acc_s[...] + _dot_nt(p_src[...], vaug_ref[0, 0, kt])

    kd = t0 // tk
    npairs = kd // 2
    last_piped = jnp.maximum(2 * npairs - 1, 0)
    p1_s[...] = jnp.zeros(p1_s.shape, bf16)
    a1_s[...] = jnp.ones((rows, 1), f32)
    scores_into(s0_s, 0)

    def pair(j, carry):
        ka = 2 * j
        values_from(p1_s, a1_s, jnp.maximum(ka - 1, 0))
        scores_into(s1_s, ka + 1)
        softmax_into(s0_s[...], ka, p0_s, a0_s)
        values_from(p0_s, a0_s, ka)
        scores_into(s0_s, jnp.minimum(ka + 2, last_piped))
        softmax_into(s1_s[...], ka + 1, p1_s, a1_s)
        return carry

    lax.fori_loop(0, npairs, pair, 0)
    values_from(p1_s, a1_s, last_piped)

    def tile_plain(kt, bias):
        s = _dot(q_aug, kaug_ref[0, 0, kt])
        if bias is not None:
            s = s + bias
        softmax_into(s, kt, p0_s, a0_s)
        values_from(p0_s, a0_s, kt)

    @pl.when(kd % 2 == 1)
    def _():
        tile_plain(kd - 1, None)

    tile_plain(kd, jnp.concatenate([cbias_ref[(t0 % tk) // tq]] * HPG, axis=0))
    acc = acc_s[...]
    o_sel = acc[:, :HEAD_DIM] / acc[:, HEAD_DIM:HEAD_DIM + 1]

    o_ref[0] = _gate_mix(gl_ref[0], o_cmp, o_sel, o_win, tq)


def _prompt_consts(t_len, tq, tk, tkw):
    n_c = t_len // CMP_BLOCK
    n_pages = t_len // PAGE_SIZE
    rows = HPG * tq
    r = np.arange(n_c)
    bpp = PAGE_SIZE // CMP_BLOCK
    ids = np.where(r < n_pages, bpp * r, bpp * (r - n_pages) + 1)
    onehot = ids[:, None] == (np.arange(t_len) // CMP_BLOCK)[None, :]
    e = onehot.reshape(n_c, t_len // tk, tk).transpose(1, 0, 2)
    lane = np.arange(tk)
    prow = np.zeros((HEAD_DIM, tk), np.float32)
    prow[0:3] = lane // 256
    prow[3:6] = lane % 256
    slopes = (2.0 ** (-8.0 * np.arange(1, N_HEADS + 1) / N_HEADS)).astype(np.float32)
    srow = np.repeat(slopes.reshape(N_KV, HPG), tq, axis=1)
    s1 = srow.astype(jnp.bfloat16).astype(np.float32)
    s2 = (srow - s1).astype(jnp.bfloat16).astype(np.float32)
    s3 = (srow - s1 - s2).astype(jnp.bfloat16).astype(np.float32)
    coef = np.zeros((N_KV, rows, HEAD_DIM), np.float32)
    for k, piece in enumerate((s1, s2, s3)):
        coef[:, :, k] = 256.0 * piece
        coef[:, :, 3 + k] = piece
    a = np.tile(np.arange(tq), HPG)[:, None]
    nslots = (WINDOW + tq) // tkw
    dpos = a + WINDOW - np.arange(nslots * tkw)[None, :]
    band = np.where((dpos >= 0) & (dpos <= WINDOW), 0.0, NEG)
    wbias = -srow[:, :, None] * dpos[None] + band[None]
    off = np.arange(tk // tq)[:, None, None] * tq
    cbias = np.where(lane[None, None, :] <= off + np.arange(tq)[None, :, None], 0.0, NEG)
    return dict(
        scol=jnp.asarray(srow.reshape(N_KV, rows, 1), f32), srow=jnp.asarray(srow.reshape(N_KV, 1, rows), f32),
        coef=jnp.asarray(coef, bf16), e=jnp.asarray(e, bf16), prow=jnp.asarray(prow, bf16),
        ones=jnp.ones((16, tk), bf16), wbias=jnp.asarray(wbias, f32), cbias=jnp.asarray(cbias, f32),
        eye=jnp.eye(tq, dtype=bf16))


def _nsa_prompt(q, gl, kc, vc, kaug, vaug, kwin, c, tq):
    bsz, t_len, _ = q.shape
    tk = kaug.shape[-1]
    tkw = kwin.shape[-1]
    npb = t_len // PAGE_SIZE
    rows = HPG * tq
    kern = functools.partial(_nsa_prompt_kernel, t_len=t_len, tq=tq, tk=tk, tkw=tkw)
    augspec = lambda arr: pl.BlockSpec((1, 1) + arr.shape[2:], lambda b, g, i: (b, g, 0, 0, 0))
    cspec = pl.BlockSpec((1, npb, kc.shape[-1]), lambda b, g, i: (g, b, 0))
    perg = lambda arr: pl.BlockSpec((1,) + arr.shape[1:], lambda b, g, i: (g,) + (0,) * (arr.ndim - 1))
    whole = lambda arr: pl.BlockSpec(arr.shape, lambda b, g, i: (0,) * arr.ndim)
    return pl.pallas_call(
        kern,
        grid=(bsz, N_KV, t_len // tq),
        in_specs=[
            pl.BlockSpec((1, tq, QGRP), lambda b, g, i: (b, i, g)),
            pl.BlockSpec((1, tq, 128), lambda b, g, i: (b, i, g)),
            cspec, cspec,
            augspec(kaug), augspec(vaug),
            pl.BlockSpec((1, t_len // tkw, GRP, tkw), lambda b, g, i: (b, 0, g, 0)),
            perg(c["scol"]), perg(c["srow"]), perg(c["coef"]),
            perg(c["wbias"]), whole(c["cbias"]), whole(c["eye"]),
        ],
        out_specs=pl.BlockSpec((1, tq, QGRP), lambda b, g, i: (b, i, g)),
        out_shape=jax.ShapeDtypeStruct((bsz, t_len, N_KV * QGRP), f32),
        scratch_shapes=[
            pltpu.VMEM((rows, tk), f32), pltpu.VMEM((rows, tk), f32),
            pltpu.VMEM((rows, tk), bf16), pltpu.VMEM((rows, tk), bf16),
            pltpu.VMEM((rows, 1), f32), pltpu.VMEM((rows, 1), f32),
            pltpu.VMEM((rows, 1), f32), pltpu.VMEM((rows, vaug.shape[3]), f32),
        ],
        compiler_params=_params(("arbitrary", "arbitrary", "arbitrary")),
        name="nsa_prompt_attn",
    )(q, gl, kc, vc, kaug, vaug, kwin, c["scol"], c["srow"], c["coef"], c["wbias"], c["cbias"], c["eye"])


def _nsa_sample_kernel(pt_ref, q_ref, gl_ref, kcall_ref, vcall_ref, pid_ref, newsel_ref, swin_ref, nwin_ref,
                       slope_ref, *rest, s_len, past_len, pp, n_groups, lanes):
    page_refs = rest[:pp]
    o_ref = rest[pp]
    sel_s, ocmp_s, m_s, l_s, acc_s = rest[pp + 1:]
    j = pl.program_id(1)
    rows = HPG * s_len
    n_c = past_len // CMP_BLOCK
    npb = past_len // PAGE_SIZE
    jrow = _block_ids(npb, lanes, 1)
    jcol = _block_ids(npb, lanes, 0)
    q = q_ref[0].astype(f32)
    t_col = past_len + lax.broadcasted_iota(i32, (rows, 1), 0) % s_len
    tq_col = past_len + lax.broadcasted_iota(i32, (s_len, 1), 0)

    def q_of(g):
        return jnp.concatenate(
            [q[:, (g * HPG + h) * HEAD_DIM:(g * HPG + h + 1) * HEAD_DIM] for h in range(HPG)],
            axis=0).astype(bf16)

    @pl.when(j == 0)
    def _():
        n_all = kcall_ref.shape[1]
        gather = (lax.broadcasted_iota(i32, (npb, n_all), 1) == pid_ref[0]).astype(bf16)
        for g in range(N_KV):
            kc = _unpage(_dot(gather, kcall_ref[g])).astype(bf16)
            vc = _unpage(_dot(gather, vcall_ref[g])).astype(bf16)
            o_cmp, imp = _cmp_branch(q_of(g), kc, vc, slope_ref[g], t_col, s_len, jrow[:, :n_c])
            imp = jnp.concatenate([imp, jnp.zeros((s_len, lanes - n_c), f32)], axis=1)
            sel_s[g] = _select_blocks(imp, tq_col, n_c + 1, jrow)
            ocmp_s[g] = o_cmp
            m_s[g] = jnp.full((rows, 1), NEG, f32)
            l_s[g] = jnp.zeros((rows, 1), f32)
            acc_s[g] = jnp.zeros((rows, HEAD_DIM), f32)

    def split(keys, g):
        return (keys[g * GRP:g * GRP + HEAD_DIM].astype(bf16),
                keys[g * GRP + HEAD_DIM:(g + 1) * GRP].astype(bf16))

    def attend(g, keys, pos):
        carry = (m_s[g], l_s[g], acc_s[g])
        valid = _block_mask(sel_s[g].astype(bf16), jcol, pos, HPG) & (pos <= t_col)
        kt, vt = split(keys, g)
        m_new, l_new, acc = _flash_update(carry, q_of(g), kt, vt, slope_ref[g], valid,
                                          (pos - past_len).astype(f32))
        m_s[g] = m_new
        l_s[g] = l_new
        acc_s[g] = acc

    nk = pp * PAGE_SIZE
    keys = jnp.concatenate([r[0] for r in page_refs], axis=1)
    pos = j * nk + lax.broadcasted_iota(i32, (1, nk), 1)
    for g in range(N_KV):
        attend(g, keys, pos)

    @pl.when(j == n_groups - 1)
    def _():
        new_keys = newsel_ref[0]
        npos = past_len + lax.broadcasted_iota(i32, (1, new_keys.shape[1]), 1)
        outs = []
        gl = gl_ref[0]
        for g in range(N_KV):
            attend(g, new_keys, npos)
            o_sel = acc_s[g] / l_s[g]
            carry = _init_carry(rows)
            q4 = q_of(g)
            for wkeys, wbase in ((swin_ref[0], past_len - swin_ref.shape[2]), (nwin_ref[0], past_len)):
                wpos = wbase + lax.broadcasted_iota(i32, (1, wkeys.shape[1]), 1)
                dpos = t_col - wpos
                valid = (dpos >= 0) & (dpos <= WINDOW)
                kt, vt = split(wkeys, g)
                carry = _flash_update(carry, q4, kt, vt, slope_ref[g], valid, (wpos - past_len).astype(f32))
            o_win = carry[2] / carry[1]
            outs.append(_gate_mix(gl[:, g * 128:(g + 1) * 128], ocmp_s[g], o_sel, o_win, s_len))
        o_ref[0] = jnp.concatenate(outs, axis=1)


def _nsa_sample(page_table, q, gl, kc_all, vc_all, pids, new_sel, state_win, new_win, slopes, cache_sel, pp):
    bsz, s_len, hd = q.shape
    n_pages = page_table.shape[1]
    past_len = n_pages * PAGE_SIZE
    pp = min(pp, n_pages)
    n_groups = n_pages // pp
    n_c = past_len // CMP_BLOCK
    lanes = -(-(n_c + 2) // 128) * 128
    rows = HPG * s_len
    kvw = cache_sel.shape[1]
    perb = lambda b, j, pt: (b, 0, 0)

    def page_spec(i):
        return pl.BlockSpec((1, kvw, PAGE_SIZE), lambda b, j, pt: (pt[b, j * pp + i], 0, 0))

    kern = functools.partial(_nsa_sample_kernel, s_len=s_len, past_len=past_len, pp=pp,
                             n_groups=n_groups, lanes=lanes)
    grid_spec = pltpu.PrefetchScalarGridSpec(
        num_scalar_prefetch=1,
        grid=(bsz, n_groups),
        in_specs=[
            pl.BlockSpec((1, s_len, hd), perb),
            pl.BlockSpec((1, s_len, gl.shape[-1]), perb),
            pl.BlockSpec(kc_all.shape, lambda b, j, pt: (0, 0, 0)),
            pl.BlockSpec(vc_all.shape, lambda b, j, pt: (0, 0, 0)),
            pl.BlockSpec((1, n_pages, 1), perb),
            pl.BlockSpec((1,) + new_sel.shape[1:], perb),
            pl.BlockSpec((1,) + state_win.shape[1:], perb),
            pl.BlockSpec((1,) + new_win.shape[1:], perb),
            pl.BlockSpec(slopes.shape, lambda b, j, pt: (0, 0, 0)),
        ] + [page_spec(i) for i in range(pp)],
        out_specs=pl.BlockSpec((1, s_len, hd), perb),
        scratch_shapes=[
            pltpu.VMEM((N_KV, s_len, lanes), f32),
            pltpu.VMEM((N_KV, rows, HEAD_DIM), f32),
            pltpu.VMEM((N_KV, rows, 1), f32),
            pltpu.VMEM((N_KV, rows, 1), f32),
            pltpu.VMEM((N_KV, rows, HEAD_DIM), f32),
        ],
    )
    return pl.pallas_call(
        kern,
        grid_spec=grid_spec,
        out_shape=jax.ShapeDtypeStruct((bsz, s_len, hd), f32),
        compiler_params=_params(("arbitrary", "arbitrary")),
        name="nsa_sample_attn",
    )(page_table, q, gl, kc_all, vc_all, pids, new_sel, state_win, new_win, slopes, *([cache_sel] * pp))


def _out_kernel(o_ref, z_ref, x_ref, mod_ref, w_ref, lng_ref, lnb_ref, xo_ref, *, nb, tr, d):
    y = _dot((o_ref[...] * _silu(z_ref[...])).astype(bf16), w_ref[...]).reshape(nb, tr, d)
    gate = mod_ref[...][:, 2:3, :]
    xo_ref[...] = _layer_norm(DN_ALPHA * x_ref[...] + (1.0 + gate) * y, lng_ref[...], lnb_ref[...])


def _out_proj(o, z, x, mod, w_out, ln_g, ln_b, nb, tr):
    bsz, t, d = x.shape
    hd = o.shape[-1]
    nt = t // tr
    const2 = lambda b, i: (0, 0)
    kern = functools.partial(_out_kernel, nb=nb, tr=tr, d=d)
    return pl.pallas_call(
        kern,
        grid=(bsz // nb, nt),
        in_specs=[
            pl.BlockSpec((nb * tr, hd), lambda b, i: (b * nt + i, 0)),
            pl.BlockSpec((nb * tr, hd), lambda b, i: (b * nt + i, 0)),
            pl.BlockSpec((nb, tr, d), lambda b, i: (b, i, 0)),
            pl.BlockSpec((nb, 3, d), lambda b, i: (b, 0, 0)),
            pl.BlockSpec(w_out.shape, const2),
            pl.BlockSpec((1, d), const2),
            pl.BlockSpec((1, d), const2),
        ],
        out_specs=pl.BlockSpec((nb, tr, d), lambda b, i: (b, i, 0)),
        out_shape=jax.ShapeDtypeStruct((bsz, t, d), f32),
        compiler_params=_params(("arbitrary", "arbitrary")),
        name="nsa_out",
    )(o, z, x, mod, w_out, ln_g, ln_b)


def _head_slopes(nq):
    s = 2.0 ** (-8.0 * np.arange(1, N_HEADS + 1) / N_HEADS)
    s = np.repeat(s.reshape(N_KV, HPG), nq, axis=1)
    return jnp.asarray(s.reshape(N_KV, HPG * nq, 1), f32)


def kernel(x_prompt, x_sample, c_prompt, c_sample, state_h, state_conv, cache_cmp, cache_sel, state_win, page_table, w_ada, b_ada, ln_g, ln_b, w_in_a, conv_w_a, conv_b_a, w_r_a, b_r_a, w_i_a, b_i_a, lam_a, w_out_a, w_kv, phi_pe, w_phi1, b_phi1, w_phi2, b_phi2, w_in_b, b_gate_b, w_out_b):
    bp, t_len, d = x_prompt.shape
    bs, s_len, _ = x_sample.shape
    n_pages = page_table.shape[1]
    past_len = n_pages * PAGE_SIZE
    hd = N_HEADS * HEAD_DIM
    kvw = N_KV * GRP
    assert s_len <= CMP_BLOCK and t_len % 128 == 0 and w_ada.shape[0] == DEPTH == 2

    n_c = bp + bs
    pad = -n_c % 8
    c_all = jnp.concatenate([c_prompt, c_sample, jnp.zeros((pad, d), f32)], axis=0)
    mods = _ada_mod(c_all, w_ada, b_ada).reshape(DEPTH, n_c + pad, 3, d)
    mod_p = mods[:, :bp]
    mod_s = mods[:, bp:n_c]

    rg = d // N_RG_BLOCKS
    w_in0 = w_in_a[0].astype(bf16)
    w_gate = jnp.concatenate([w_r_a[0], w_i_a[0]], axis=-1).astype(bf16)
    w_out0 = w_out_a[0].astype(bf16)
    row = lambda v: v.reshape(1, -1)
    rg_args = (w_in0, conv_w_a[0], row(conv_b_a[0]), w_gate, row(b_r_a[0]), row(b_i_a[0]), row(lam_a[0]),
               w_out0, row(ln_g[0]), row(ln_b[0]))
    xp1, h_p, conv_p = _rglru_prompt(x_prompt, mod_p[0], jnp.zeros((bp, 1, d), f32),
                                     jnp.zeros((bp, CONV_W - 1, d), f32), *rg_args, tc=256)
    xs1_tm, h_s, conv_s_tm = _rglru_sample(
        x_sample.transpose(1, 0, 2), mod_s[0].transpose(1, 0, 2), state_h[0],
        state_conv[0].transpose(1, 0, 2), *rg_args)
    xs1 = xs1_tm.transpose(1, 0, 2)
    new_h_p = h_p.reshape(1, bp, d)
    new_conv_p = conv_p.reshape(1, bp, CONV_W - 1, d)
    new_h_s = h_s.reshape(1, bs, d)
    new_conv_s = conv_s_tm.transpose(1, 0, 2).reshape(1, bs, CONV_W - 1, d)

    w_b = w_in_b[0]
    w_q = w_b[:, :hd].astype(bf16)
    w_z = w_b[:, hd:2 * hd].astype(bf16)
    gpg = HPG * 3
    w_g = jnp.pad(w_b[:, 2 * hd:].reshape(d, N_KV, gpg), ((0, 0), (0, 0), (0, 128 - gpg)))
    w_g = w_g.reshape(d, N_KV * 128).astype(bf16)
    b_g = jnp.pad(b_gate_b[0].reshape(N_KV, gpg), ((0, 0), (0, 128 - gpg))).reshape(1, N_KV * 128)
    tmin = lambda v: v.transpose(0, 2, 3, 4, 1)
    tq = min(128, t_len)
    tk = min(512, t_len)
    consts = _prompt_consts(t_len, tq, tk, PAGE_SIZE)
    cmpt_p, selt_p, wint_p, cmp_pages, kaug_p, vaug_p, wint_tiles, q_p, z_p, gl_p = _proj_prompt(
        xp1, mod_p[1], w_kv.T.astype(bf16), w_q, w_z, w_g, b_g, consts["e"], consts["prow"], consts["ones"],
        tr=tk)
    cmp_s, sel_s, win_s, selb_s, winb_s, q_s, z_s, gl_s = _proj(
        xs1, mod_s[1], w_kv.astype(bf16), w_q, w_z, w_g, b_g, nb=bs, tr=s_len)
    per_p = lambda v: v.reshape(bp, t_len, v.shape[-1])
    per_s = lambda v: v.reshape(bs, s_len, v.shape[-1])

    bpp = PAGE_SIZE // CMP_BLOCK
    dphi = w_phi2.shape[1]
    diag2 = lambda w: jnp.concatenate([jnp.concatenate([w, jnp.zeros_like(w)], axis=-1),
                                       jnp.concatenate([jnp.zeros_like(w), w], axis=-1)], axis=-2)
    pet = jnp.tile(phi_pe.transpose(1, 2, 0), (1, 1, bpp))
    w1 = diag2(w_phi1.transpose(0, 2, 1, 3)).astype(bf16)
    b1 = jnp.tile(b_phi1, (1, bpp)).reshape(2, 1, bpp * dphi)
    w2 = diag2(w_phi2).astype(bf16)
    b2 = jnp.tile(b_phi2, (1, bpp)).reshape(2, 1, bpp * HEAD_DIM)
    kc_p, vc_p = _compress(cmp_pages.reshape(-1, PAGE_SIZE), pet, w1, b1, w2, b2, npg=32, out_dtype=f32)
    kc_all, vc_all = _compress(tmin(cache_cmp).reshape(-1, PAGE_SIZE), pet, w1, b1, w2, b2,
                               npg=32, out_dtype=bf16)

    o_p = _nsa_prompt(per_p(q_p), per_p(gl_p), kc_p, vc_p, kaug_p, vaug_p, wint_tiles, consts, tq=tq)
    tpad = lambda v: jnp.pad(per_s(v), ((0, 0), (0, PAGE_SIZE - s_len), (0, 0))).transpose(0, 2, 1)
    o_s = _nsa_sample(page_table, per_s(q_s), per_s(gl_s), kc_all, vc_all, page_table.reshape(bs, n_pages, 1),
                      tpad(selb_s), tmin(state_win).reshape(bs, kvw, -1), tpad(winb_s), _head_slopes(s_len),
                      tmin(cache_sel).reshape(-1, kvw, PAGE_SIZE), pp=8)

    w_out1 = w_out_b[0].astype(bf16)
    y_p = _out_proj(o_p.reshape(bp * t_len, hd), z_p, xp1, mod_p[1], w_out1, row(ln_g[1]), row(ln_b[1]),
                    nb=1, tr=min(512, t_len))
    y_s = _out_proj(o_s.reshape(bs * s_len, hd), z_s, xs1, mod_s[1], w_out1, row(ln_g[1]), row(ln_b[1]),
                    nb=bs, tr=s_len)

    kv5 = lambda v: v.reshape(bs, s_len, N_KV, 2, HEAD_DIM)
    kv5t = lambda v: v.reshape(bp, N_KV, 2, HEAD_DIM, -1).transpose(0, 4, 1, 2, 3)
    wb = state_win.shape[1]
    new_win_p = kv5t(wint_p[:, :, -min(WINDOW, t_len):])
    new_win_s = jnp.concatenate([state_win, kv5(win_s)], axis=1)[:, -wb:]
    return (y_p, y_s, kv5t(cmpt_p), kv5t(selt_p), new_win_p, new_h_p, new_conv_p,
            kv5(cmp_s), kv5(sel_s), new_win_s, new_h_s, new_conv_s)
```

```python
import functools
import math

import numpy as np
import jax
import jax.numpy as jnp
from jax import lax
from jax.experimental import pallas as pl
from jax.experimental.pallas import tpu as pltpu

f32 = jnp.float32
bf16 = jnp.bfloat16
i32 = jnp.int32

DEPTH = 2
N_RG_BLOCKS = 8
CONV_W = 4
RG_C = 8.0
N_HEADS = 16
HEAD_DIM = 64
N_KV = 4
HPG = N_HEADS // N_KV
CMP_BLOCK = 64
N_SEL = 16
WINDOW = 512
PAGE_SIZE = 128
DN_ALPHA = (2.0 * DEPTH) ** 0.25
LN_EPS = 1e-5
NEG = -1e30
FORCED = 1e6
REMOVED = -3e38
GRP = 2 * HEAD_DIM
QGRP = HPG * HEAD_DIM
VMEM_LIMIT_BYTES = 56 * 1024 * 1024


def _params(sem):
    return pltpu.CompilerParams(dimension_semantics=sem, vmem_limit_bytes=VMEM_LIMIT_BYTES)


def _dot(a, b):
    return jnp.dot(a, b, preferred_element_type=f32)


def _dot_nt(a, b):
    return lax.dot_general(a, b, (((1,), (1,)), ((), ())), preferred_element_type=f32)


def _sigmoid(x):
    return 1.0 / (1.0 + jnp.exp(-x))


def _silu(x):
    return x * _sigmoid(x)


def _log1p(e):
    u = 1.0 + e
    dlt = u - 1.0
    return jnp.where(dlt == 0.0, e, jnp.log(u) * (e / jnp.where(dlt == 0.0, 1.0, dlt)))


def _layer_norm(x, g, b):
    mu = jnp.mean(x, axis=-1, keepdims=True)
    xc = x - mu
    var = jnp.mean(xc * xc, axis=-1, keepdims=True)
    return xc * lax.rsqrt(var + LN_EPS) * g + b


def _ada_kernel(c_ref, w_ref, b_ref, o_ref):
    a = _silu(c_ref[...])
    o_ref[0] = jnp.dot(a, w_ref[0], preferred_element_type=f32,
                       precision=lax.Precision.HIGHEST) + b_ref[0]


def _ada_mod(c_all, w_ada, b_ada):
    rows, d = c_all.shape
    depth = w_ada.shape[0]
    return pl.pallas_call(
        _ada_kernel,
        grid=(depth, 3),
        in_specs=[
            pl.BlockSpec((rows, d), lambda l, n: (0, 0)),
            pl.BlockSpec((1, d, d), lambda l, n: (l, 0, n)),
            pl.BlockSpec((1, 1, d), lambda l, n: (l, 0, n)),
        ],
        out_specs=pl.BlockSpec((1, rows, d), lambda l, n: (l, 0, n)),
        out_shape=jax.ShapeDtypeStruct((depth, rows, 3 * d), f32),
        compiler_params=_params(("arbitrary", "arbitrary")),
        name="ada_mod",
    )(c_all, w_ada, b_ada.reshape(depth, 1, 3 * d))


def _rglru_gates(xc, wg_ref, br, bi, lam):
    xcb = xc.astype(bf16)
    rg = xc.shape[1] // N_RG_BLOCKS
    rs, is_ = [], []
    for n in range(N_RG_BLOCKS):
        g = _dot(xcb[:, n * rg:(n + 1) * rg], wg_ref[n])
        rs.append(g[:, :rg])
        is_.append(g[:, rg:])
    r = _sigmoid(jnp.concatenate(rs, axis=1) + br)
    i = _sigmoid(jnp.concatenate(is_, axis=1) + bi)
    nl = -lam
    softplus = jnp.maximum(nl, 0.0) + _log1p(jnp.exp(-jnp.abs(nl)))
    log_a = (-RG_C * softplus) * r
    a = jnp.exp(log_a)
    gain = jnp.sqrt(jnp.maximum(-jnp.tanh(log_a) * (a * a + 1.0), 0.0))
    b = gain * i * xc
    return a, b


def _rglru_prompt_kernel(x_ref, mod_ref, h0_ref, c0_ref, win_ref, cw_ref, cb_ref, wg_ref, br_ref,
                         bi_ref, lam_ref, wout_ref, lng_ref, lnb_ref,
                         xo_ref, hl_ref, cl_ref, xbuf, a_s, b_s, hs_s, hc, *, tc, d):
    t = pl.program_id(1)

    @pl.when(t == 0)
    def _():
        xbuf[0:8, :] = jnp.zeros((8, d), f32)
        xbuf[8 - (CONV_W - 1):8, :] = c0_ref[0]
        hc[...] = jnp.broadcast_to(h0_ref[0], (8, d))

    x = x_ref[0]
    mod = mod_ref[0]
    shift, scale, gate = mod[0:1], mod[1:2], mod[2:3]
    m = x * (1.0 + scale) + shift
    u = _dot(m.astype(bf16), win_ref[...])
    xb = u[:, :d]
    zg = u[:, d:]
    xbuf[8:8 + tc, :] = xb
    base = 8 - (CONV_W - 1)
    xc = cb_ref[...] + xbuf[base:base + tc, :] * cw_ref[0:1, :]
    for k in range(1, CONV_W):
        xc = xc + xbuf[base + k:base + k + tc, :] * cw_ref[k:k + 1, :]
    tail = xbuf[8 + tc - (CONV_W - 1):8 + tc, :]
    xbuf[base:8, :] = tail
    cl_ref[0] = tail

    a, b = _rglru_gates(xc, wg_ref, br_ref[...], bi_ref[...], lam_ref[...])

    rowi = lax.broadcasted_iota(i32, (tc, 1), 0) % 8
    for s in (1, 2, 4):
        ok = rowi >= s
        a_sh = pltpu.roll(a, s, 0)
        b_sh = pltpu.roll(b, s, 0)
        b = jnp.where(ok, a * b_sh + b, b)
        a = jnp.where(ok, a * a_sh, a)
    a_s[...] = a
    b_s[...] = b

    def tile_step(j, hprev):
        r0 = pl.multiple_of(j * 8, 8)
        ht = a_s[pl.ds(r0, 8), :] * hprev + b_s[pl.ds(r0, 8), :]
        hs_s[pl.ds(r0, 8), :] = ht
        return jnp.broadcast_to(ht[7:8, :], (8, d))

    hlast = lax.fori_loop(0, tc // 8, tile_step, hc[...])
    hc[...] = hlast
    hl_ref[0] = hlast[0:1, :]

    y = _dot((hs_s[...] * _silu(zg)).astype(bf16), wout_ref[...])
    xo_ref[0] = _layer_norm(DN_ALPHA * x + (1.0 + gate) * y, lng_ref[...], lnb_ref[...])


def _rglru_prompt(x, mod, h0, c0, w_in, conv_w, conv_b, w_gate, b_r, b_i, lam, w_out, ln_g, ln_b, tc):
    bsz, t, d = x.shape
    tc = min(tc, t)
    const2 = lambda b, i: (0, 0)
    const3 = lambda b, i: (0, 0, 0)
    perb = lambda b, i: (b, 0, 0)
    kern = functools.partial(_rglru_prompt_kernel, tc=tc, d=d)
    return pl.pallas_call(
        kern,
        grid=(bsz, t // tc),
        in_specs=[
            pl.BlockSpec((1, tc, d), lambda b, i: (b, i, 0)),
            pl.BlockSpec((1, 3, d), perb),
            pl.BlockSpec((1, 1, d), perb),
            pl.BlockSpec((1, CONV_W - 1, d), perb),
            pl.BlockSpec(w_in.shape, const2),
            pl.BlockSpec(conv_w.shape, const2),
            pl.BlockSpec((1, d), const2),
            pl.BlockSpec(w_gate.shape, const3),
            pl.BlockSpec((1, d), const2),
            pl.BlockSpec((1, d), const2),
            pl.BlockSpec((1, d), const2),
            pl.BlockSpec(w_out.shape, const2),
            pl.BlockSpec((1, d), const2),
            pl.BlockSpec((1, d), const2),
        ],
        out_specs=[
            pl.BlockSpec((1, tc, d), lambda b, i: (b, i, 0)),
            pl.BlockSpec((1, 1, d), perb),
            pl.BlockSpec((1, CONV_W - 1, d), perb),
        ],
        out_shape=[
            jax.ShapeDtypeStruct((bsz, t, d), f32),
            jax.ShapeDtypeStruct((bsz, 1, d), f32),
            jax.ShapeDtypeStruct((bsz, CONV_W - 1, d), f32),
        ],
        scratch_shapes=[
            pltpu.VMEM((tc + 8, d), f32),
            pltpu.VMEM((tc, d), f32),
            pltpu.VMEM((tc, d), f32),
            pltpu.VMEM((tc, d), f32),
            pltpu.VMEM((8, d), f32),
        ],
        compiler_params=_params(("arbitrary", "arbitrary")),
        name="rglru_prompt",
    )(x, mod, h0, c0, w_in, conv_w, conv_b, w_gate, b_r, b_i, lam, w_out, ln_g, ln_b)


def _rglru_sample_kernel(x_ref, mod_ref, h0_ref, c0_ref, win_ref, cw_ref, cb_ref, wg_ref, br_ref,
                         bi_ref, lam_ref, wout_ref, lng_ref, lnb_ref,
                         xo_ref, hl_ref, cl_ref, *, s_len, bsz, d):
    x = x_ref[...]
    mod = mod_ref[...]
    shift, scale, gate = mod[0:1], mod[1:2], mod[2:3]
    m = x * (1.0 + scale) + shift
    u = _dot(m.reshape(s_len * bsz, d).astype(bf16), win_ref[...])
    xb = u[:, :d].reshape(s_len, bsz, d)
    zg = u[:, d:]
    xp = jnp.concatenate([c0_ref[...], xb], axis=0)
    cw = cw_ref[...]
    xc = cb_ref[...] + xp[0:s_len] * cw[0:1]
    for k in range(1, CONV_W):
        xc = xc + xp[k:k + s_len] * cw[k:k + 1]
    cl_ref[...] = xp[s_len:s_len + CONV_W - 1]
    a, b = _rglru_gates(xc.reshape(s_len * bsz, d), wg_ref, br_ref[...], bi_ref[...], lam_ref[...])
    h = h0_ref[...]
    hs = []
    for s in range(s_len):
        h = a[s * bsz:(s + 1) * bsz] * h + b[s * bsz:(s + 1) * bsz]
        hs.append(h)
    hl_ref[...] = h
    hs = jnp.concatenate(hs, axis=0)
    y = _dot((hs * _silu(zg)).astype(bf16), wout_ref[...])
    xo = _layer_norm(DN_ALPHA * x + (1.0 + gate) * y.reshape(s_len, bsz, d), lng_ref[...], lnb_ref[...])
    xo_ref[...] = xo


def _rglru_sample(x_tm, mod_tm, h0, c0_tm, w_in, conv_w, conv_b, w_gate, b_r, b_i, lam, w_out, ln_g, ln_b):
    s_len, bsz, d = x_tm.shape
    kern = functools.partial(_rglru_sample_kernel, s_len=s_len, bsz=bsz, d=d)
    return pl.pallas_call(
        kern,
        out_shape=[
            jax.ShapeDtypeStruct((s_len, bsz, d), f32),
            jax.ShapeDtypeStruct((bsz, d), f32),
            jax.ShapeDtypeStruct((CONV_W - 1, bsz, d), f32),
        ],
        compiler_params=pltpu.CompilerParams(vmem_limit_bytes=VMEM_LIMIT_BYTES),
        name="rglru_sample",
    )(x_tm, mod_tm, h0, c0_tm, w_in, conv_w, conv_b, w_gate, b_r, b_i, lam, w_out, ln_g, ln_b)


def _proj_kernel(x_ref, mod_ref, wkv_ref, wq_ref, wz_ref, wg_ref, bg_ref,
                 cmp_ref, sel_ref, win_ref, selb_ref, winb_ref, q_ref, z_ref, gl_ref, *, nb, tr, d):
    x = x_ref[...]
    mod = mod_ref[...]
    m = x * (1.0 + mod[:, 1:2, :]) + mod[:, 0:1, :]
    xb = x.reshape(nb * tr, d).astype(bf16)
    mb = m.reshape(nb * tr, d).astype(bf16)
    kv = _dot(xb, wkv_ref[...])
    w = kv.shape[1] // 3
    sel = kv[:, w:2 * w]
    win = kv[:, 2 * w:]
    cmp_ref[...] = kv[:, :w]
    sel_ref[...] = sel
    win_ref[...] = win
    selb_ref[...] = sel.astype(bf16)
    winb_ref[...] = win.astype(bf16)
    q_ref[...] = (_dot(mb, wq_ref[...]) * (HEAD_DIM ** -0.5)).astype(bf16)
    z_ref[...] = _dot(mb, wz_ref[...])
    gl_ref[...] = _dot(mb, wg_ref[...]) + bg_ref[...]


def _proj(x, mod, w_kv, w_q, w_z, w_g, b_g, nb, tr):
    bsz, t, d = x.shape
    kvw = w_kv.shape[1] // 3
    hd = w_q.shape[1]
    gw = w_g.shape[1]
    nt = t // tr
    const2 = lambda b, i: (0, 0)
    blk = lambda width: pl.BlockSpec((nb * tr, width), lambda b, i: (b * nt + i, 0))
    out = lambda width, dt: jax.ShapeDtypeStruct((bsz * t, width), dt)
    kern = functools.partial(_proj_kernel, nb=nb, tr=tr, d=d)
    return pl.pallas_call(
        kern,
        grid=(bsz // nb, nt),
        in_specs=[
            pl.BlockSpec((nb, tr, d), lambda b, i: (b, i, 0)),
            pl.BlockSpec((nb, 3, d), lambda b, i: (b, 0, 0)),
            pl.BlockSpec(w_kv.shape, const2),
            pl.BlockSpec(w_q.shape, const2),
            pl.BlockSpec(w_z.shape, const2),
            pl.BlockSpec(w_g.shape, const2),
            pl.BlockSpec((1, gw), const2),
        ],
        out_specs=[blk(kvw), blk(kvw), blk(kvw), blk(kvw), blk(kvw), blk(hd), blk(hd), blk(gw)],
        out_shape=[out(kvw, f32), out(kvw, f32), out(kvw, f32), out(kvw, bf16), out(kvw, bf16),
                   out(hd, bf16), out(hd, f32), out(gw, f32)],
        compiler_params=_params(("arbitrary", "arbitrary")),
        name="nsa_proj",
    )(x, mod, w_kv, w_q, w_z, w_g, b_g)


def _proj_prompt_kernel(x_ref, mod_ref, wkvt_ref, wq_ref, wz_ref, wg_ref, bg_ref, e_ref, prow_ref, ones_ref,
                        cmpt_ref, selt_ref, wint_ref, cmppg_ref, kaug_ref, vaug_ref, wintt_ref, q_ref, z_ref,
                        gl_ref, *, tr):
    x = x_ref[0]
    mod = mod_ref[0]
    m = x * (1.0 + mod[1:2]) + mod[0:1]
    mb = m.astype(bf16)
    kvt = _dot_nt(wkvt_ref[...], x.astype(bf16))
    kvw = kvt.shape[0] // 3
    cmpt = kvt[:kvw]
    selt = kvt[kvw:2 * kvw]
    wint = kvt[2 * kvw:]
    cmpt_ref[0] = cmpt
    selt_ref[0] = selt
    wint_ref[0] = wint
    selb = selt.astype(bf16)
    n_c = e_ref.shape[1]
    for g in range(N_KV):
        kaug_ref[0, g, 0, 0:n_c] = e_ref[0]
        kaug_ref[0, g, 0, n_c:n_c + HEAD_DIM] = selb[g * GRP:g * GRP + HEAD_DIM]
        kaug_ref[0, g, 0, n_c + HEAD_DIM:] = prow_ref[...]
        vaug_ref[0, g, 0, 0:HEAD_DIM] = selb[g * GRP + HEAD_DIM:(g + 1) * GRP]
        vaug_ref[0, g, 0, HEAD_DIM:] = ones_ref[...]
    winb = wint.astype(bf16)
    for k in range(tr // PAGE_SIZE):
        cmppg_ref[k] = cmpt[:, k * PAGE_SIZE:(k + 1) * PAGE_SIZE]
        wintt_ref[0, k] = winb[:, k * PAGE_SIZE:(k + 1) * PAGE_SIZE]
    q_ref[...] = (_dot(mb, wq_ref[...]) * (HEAD_DIM ** -0.5)).astype(bf16)
    z_ref[...] = _dot(mb, wz_ref[...])
    gl_ref[...] = _dot(mb, wg_ref[...]) + bg_ref[...]


def _proj_prompt(x, mod, w_kvt, w_q, w_z, w_g, b_g, e, prow, ones, tr):
    bsz, t, d = x.shape
    kvw = w_kvt.shape[0] // 3
    hd = w_q.shape[1]
    gw = w_g.shape[1]
    nt = t // tr
    ppt = tr // PAGE_SIZE
    ka_rows = e.shape[1] + HEAD_DIM + prow.shape[0]
    va_rows = HEAD_DIM + ones.shape[0]
    const2 = lambda b, i: (0, 0)
    rows = lambda width: pl.BlockSpec((tr, width), lambda b, i: (b * nt + i, 0))
    tmin = pl.BlockSpec((1, kvw, tr), lambda b, i: (b, 0, i))
    kern = functools.partial(_proj_prompt_kernel, tr=tr)
    return pl.pallas_call(
        kern,
        grid=(bsz, nt),
        in_specs=[
            pl.BlockSpec((1, tr, d), lambda b, i: (b, i, 0)),
            pl.BlockSpec((1, 3, d), lambda b, i: (b, 0, 0)),
            pl.BlockSpec(w_kvt.shape, const2),
            pl.BlockSpec(w_q.shape, const2),
            pl.BlockSpec(w_z.shape, const2),
            pl.BlockSpec(w_g.shape, const2),
            pl.BlockSpec((1, gw), const2),
            pl.BlockSpec((1,) + e.shape[1:], lambda b, i: (i, 0, 0)),
            pl.BlockSpec(prow.shape, const2),
            pl.BlockSpec(ones.shape, const2),
        ],
        out_specs=[
            tmin, tmin, tmin,
            pl.BlockSpec((ppt, kvw, PAGE_SIZE), lambda b, i: (b * nt + i, 0, 0)),
            pl.BlockSpec((1, N_KV, 1, ka_rows, tr), lambda b, i: (b, 0, i, 0, 0)),
            pl.BlockSpec((1, N_KV, 1, va_rows, tr), lambda b, i: (b, 0, i, 0, 0)),
            pl.BlockSpec((1, ppt, kvw, PAGE_SIZE), lambda b, i: (b, i, 0, 0)),
            rows(hd), rows(hd), rows(gw),
        ],
        out_shape=[
            jax.ShapeDtypeStruct((bsz, kvw, t), f32),
            jax.ShapeDtypeStruct((bsz, kvw, t), f32),
            jax.ShapeDtypeStruct((bsz, kvw, t), f32),
            jax.ShapeDtypeStruct((bsz * t // PAGE_SIZE, kvw, PAGE_SIZE), f32),
            jax.ShapeDtypeStruct((bsz, N_KV, nt, ka_rows, tr), bf16),
            jax.ShapeDtypeStruct((bsz, N_KV, nt, va_rows, tr), bf16),
            jax.ShapeDtypeStruct((bsz, t // PAGE_SIZE, kvw, PAGE_SIZE), bf16),
            jax.ShapeDtypeStruct((bsz * t, hd), bf16),
            jax.ShapeDtypeStruct((bsz * t, hd), f32),
            jax.ShapeDtypeStruct((bsz * t, gw), f32),
        ],
        compiler_params=_params(("arbitrary", "arbitrary")),
        name="nsa_proj_prompt",
    )(x, mod, w_kvt, w_q, w_z, w_g, b_g, e, prow, ones)


def _compress_kernel(x_ref, pet_ref, w1_ref, b1_ref, w2_ref, b2_ref, kc_ref, vc_ref, *, npg):
    page_rows = N_KV * GRP

    def dim_rows(c, dd):
        return jnp.concatenate(
            [(x_ref[pl.ds((g * 2 + c) * HEAD_DIM + dd, npg, stride=page_rows), :]
              + pet_ref[c, dd:dd + 1, :]).astype(bf16) for g in range(N_KV)], axis=0)

    outs = []
    for c in range(2):
        acc = None
        for dp in range(HEAD_DIM // 2):
            lhs = jnp.concatenate([dim_rows(c, 2 * dp), dim_rows(c, 2 * dp + 1)], axis=1)
            part = _dot(lhs, w1_ref[c, dp])
            acc = part if acc is None else acc + part
        hid = _silu(acc + b1_ref[c])
        outs.append(_dot(hid.astype(bf16), w2_ref[c]) + b2_ref[c])
    for g in range(N_KV):
        kc_ref[g] = outs[0][g * npg:(g + 1) * npg, :].astype(kc_ref.dtype)
        vc_ref[g] = outs[1][g * npg:(g + 1) * npg, :].astype(vc_ref.dtype)


def _compress(x2d, pet, w1, b1, w2, b2, npg, out_dtype):
    rows, width = x2d.shape
    n_pages = rows // (N_KV * GRP)
    npg = min(npg, n_pages)
    while n_pages % npg:
        npg -= 8
    ow = (PAGE_SIZE // CMP_BLOCK) * HEAD_DIM
    const3 = lambda i: (0, 0, 0)
    kern = functools.partial(_compress_kernel, npg=npg)
    out = pl.BlockSpec((N_KV, npg, ow), lambda i: (0, i, 0))
    return pl.pallas_call(
        kern,
        grid=(n_pages // npg,),
        in_specs=[
            pl.BlockSpec((npg * N_KV * GRP, width), lambda i: (i, 0)),
            pl.BlockSpec(pet.shape, const3),
            pl.BlockSpec(w1.shape, lambda i: (0, 0, 0, 0), pipeline_mode=pl.Buffered(1)),
            pl.BlockSpec(b1.shape, const3),
            pl.BlockSpec(w2.shape, const3),
            pl.BlockSpec(b2.shape, const3),
        ],
        out_specs=[out, out],
        out_shape=[jax.ShapeDtypeStruct((N_KV, n_pages, ow), out_dtype)] * 2,
        compiler_params=_params(("arbitrary",)),
        name="nsa_compress",
    )(x2d, pet, w1, b1, w2, b2)


def _block_ids(n_pages, lanes, axis):
    shape = (1, lanes) if axis == 1 else (lanes, 1)
    r = lax.broadcasted_iota(i32, shape, axis)
    bpp = PAGE_SIZE // CMP_BLOCK
    perm = jnp.where(r < n_pages, bpp * r, bpp * (r - n_pages) + 1)
    return jnp.where(r < bpp * n_pages, perm, r)


def _unpage(x):
    return jnp.concatenate([x[:, :HEAD_DIM], x[:, HEAD_DIM:]], axis=0)


def _cmp_branch(q4, kc, vc, slope, t_col, nq, jrow):
    s = _dot_nt(q4, kc)
    c_end = (jrow + 1) * CMP_BLOCK - 1
    dist = t_col.astype(f32) - c_end.astype(f32)
    s = s - slope * dist
    mask = c_end <= t_col
    s = jnp.where(mask, s, NEG)
    e = jnp.exp(s - jnp.max(s, axis=-1, keepdims=True))
    p = e / jnp.sum(e, axis=-1, keepdims=True)
    p = jnp.where(mask, p, 0.0)
    o = _dot(p.astype(bf16), vc)
    imp = p[0:nq]
    for h in range(1, HPG):
        imp = imp + p[h * nq:(h + 1) * nq]
    return o, imp


def _select_blocks(imp, tq_col, n_sb, jr):
    nq, lanes = imp.shape
    cb = tq_col // CMP_BLOCK
    forced = (jr == 0) | (jr == cb) | (jr == cb - 1)
    causal = jr <= cb
    score = jnp.where(forced, FORCED, jnp.where(causal, imp, -1.0))
    score = jnp.where(jr < n_sb, score, REMOVED)
    jf = jr.astype(f32)
    sel = jnp.zeros((nq, lanes), f32)
    for _ in range(min(N_SEL, n_sb)):
        mx = jnp.max(score, axis=-1, keepdims=True)
        idx = jnp.min(jnp.where(score == mx, jf, 1e9), axis=-1, keepdims=True)
        hit = jf == idx
        sel = jnp.where(hit, 1.0, sel)
        score = jnp.where(hit, REMOVED, score)
    return sel


def _flash_update(carry, q4, kt, vt, slope, valid, pos_rel):
    m_i, l_i, acc = carry
    s = _dot(q4, kt) + slope * pos_rel
    s = jnp.where(valid, s, NEG)
    m_new = jnp.maximum(m_i, jnp.max(s, axis=-1, keepdims=True))
    alpha = jnp.exp(m_i - m_new)
    p = jnp.exp(s - m_new)
    l_new = alpha * l_i + jnp.sum(p, axis=-1, keepdims=True)
    acc = alpha * acc + _dot_nt(p.astype(bf16), vt)
    return m_new, l_new, acc


def _block_mask(selb, jcol, pos, reps):
    onehot = (jcol == pos // CMP_BLOCK).astype(bf16)
    msk = _dot(selb, onehot)
    return jnp.concatenate([msk] * reps, axis=0) > 0.5


def _gate_mix(gl, o_cmp, o_sel, o_win, nq):
    gs = _sigmoid(gl)
    outs = []
    for h in range(HPG):
        sl = slice(h * nq, (h + 1) * nq)
        outs.append(gs[:, 3 * h:3 * h + 1] * o_cmp[sl] + gs[:, 3 * h + 1:3 * h + 2] * o_sel[sl]
                    + gs[:, 3 * h + 2:3 * h + 3] * o_win[sl])
    return jnp.concatenate(outs, axis=1)


def _init_carry(rows):
    return (jnp.full((rows, 1), NEG, f32), jnp.zeros((rows, 1), f32), jnp.zeros((rows, HEAD_DIM), f32))


def _select_blocks_t(impt, tq_row, n_sb, jcol):
    n_c, nq = impt.shape
    cb = tq_row // CMP_BLOCK
    forced = (jcol == 0) | (jcol == cb) | (jcol == cb - 1)
    causal = jcol <= cb
    score = jnp.where(forced, FORCED, jnp.where(causal, impt, -1.0))
    score = jnp.where(jcol < n_sb, score, REMOVED)
    jf = jcol.astype(f32)
    sel = jnp.zeros((n_c, nq), f32)
    for _ in range(min(N_SEL, n_sb)):
        mx = jnp.max(score, axis=0, keepdims=True)
        idx = jnp.min(jnp.where(score == mx, jf, 1e9), axis=0, keepdims=True)
        hit = jf == idx
        sel = jnp.where(hit, 1.0, sel)
        score = jnp.where(hit, REMOVED, score)
    return sel


def _nsa_prompt_kernel(q_ref, gl_ref, kc_ref, vc_ref, kaug_ref, vaug_ref, kwin_ref, scol_ref, srow_ref, coef_ref,
                       wbias_ref, cbias_ref, eye_ref, o_ref, s0_s, s1_s, p0_s, p1_s, a0_s, a1_s, m_s, acc_s,
                       *, t_len, tq, tk, tkw):
    qi = pl.program_id(2)
    t0 = qi * tq
    rows = HPG * tq
    q = q_ref[0]
    q4 = jnp.concatenate([q[:, h * HEAD_DIM:(h + 1) * HEAD_DIM] for h in range(HPG)], axis=0)
    slope = scol_ref[0]
    t_col = t0 + lax.broadcasted_iota(i32, (rows, 1), 0) % tq

    n_c = t_len // CMP_BLOCK
    n_pages = t_len // PAGE_SIZE
    jrow = _block_ids(n_pages, n_c, 1)
    jcol = _block_ids(n_pages, n_c, 0)
    kc = _unpage(kc_ref[0]).astype(bf16)
    vc = _unpage(vc_ref[0]).astype(bf16)
    o_cmp, _ = _cmp_branch(q4, kc, vc, slope, t_col, tq, jrow)

    t_row = t0 + lax.broadcasted_iota(i32, (1, rows), 1) % tq
    c_end = (jcol + 1) * CMP_BLOCK - 1
    st = _dot_nt(kc, q4) - srow_ref[0] * (t_row.astype(f32) - c_end.astype(f32))
    maskt = c_end <= t_row
    st = jnp.where(maskt, st, NEG)
    et = jnp.exp(st - jnp.max(st, axis=0, keepdims=True))
    pt = et / jnp.sum(et, axis=0, keepdims=True)
    pt = jnp.where(maskt, pt, 0.0)
    impt = pt[:, 0:tq]
    for h in range(1, HPG):
        impt = impt + pt[:, h * tq:(h + 1) * tq]
    selt = _select_blocks_t(impt, t0 + lax.broadcasted_iota(i32, (1, tq), 1), n_c, jcol)
    sel = _dot_nt(eye_ref[...], selt.astype(bf16))
    selbias = ((sel - 1.0) * -NEG).astype(bf16)
    q_aug = jnp.concatenate([jnp.concatenate([selbias] * HPG, axis=0), q4, coef_ref[0]], axis=1)

    nslots = (WINDOW + tq) // tkw
    base = (t0 - WINDOW) // tkw
    kts, vts, pens = [], [], []
    for m in range(nslots):
        kw = base + m
        kvw = kwin_ref[0, jnp.maximum(kw, 0)]
        kts.append(kvw[:HEAD_DIM])
        vts.append(kvw[HEAD_DIM:])
        pens.append(jnp.broadcast_to(jnp.where(kw >= 0, 0.0, NEG).astype(f32), (1, tkw)))
    sw = _dot(q4, jnp.concatenate(kts, axis=1)) + wbias_ref[0] + jnp.concatenate(pens, axis=1)
    pw = jnp.exp(sw - jnp.max(sw, axis=-1, keepdims=True))
    o_win = _dot_nt(pw.astype(bf16), jnp.concatenate(vts, axis=1)) / jnp.sum(pw, axis=-1, keepdims=True)

    m_s[...] = jnp.full((rows, 1), NEG, f32)
    acc_s[...] = jnp.zeros(acc_s.shape, f32)

    def scores_into(dst, kt):
        dst[...] = _dot(q_aug, kaug_ref[0, 0, kt])

    def softmax_into(s, kt, p_dst, a_dst):
        c = slope * (kt * tk - t0).astype(f32)
        m_i = m_s[...]
        m_new = jnp.maximum(m_i, jnp.max(s, axis=-1, keepdims=True) + c)
        p_dst[...] = jnp.exp(s - (m_new - c)).astype(bf16)
        a_dst[...] = jnp.exp(m_i - m_new)
        m_s[...] = m_new

    def values_from(p_src, a_src, kt):
        acc_s[...] = a_src[...] * acc_s[...] + _dot_nt(p_src[...], vaug_ref[0, 0, kt])

    kd = t0 // tk
    npairs = kd // 2
    last_piped = jnp.maximum(2 * npairs - 1, 0)
    p1_s[...] = jnp.zeros(p1_s.shape, bf16)
    a1_s[...] = jnp.ones((rows, 1), f32)
    scores_into(s0_s, 0)

    def pair(j, carry):
        ka = 2 * j
        values_from(p1_s, a1_s, jnp.maximum(ka - 1, 0))
        scores_into(s1_s, ka + 1)
        softmax_into(s0_s[...], ka, p0_s, a0_s)
        values_from(p0_s, a0_s, ka)
        scores_into(s0_s, jnp.minimum(ka + 2, last_piped))
        softmax_into(s1_s[...], ka + 1, p1_s, a1_s)
        return carry

    lax.fori_loop(0, npairs, pair, 0)
    values_from(p1_s, a1_s, last_piped)

    def tile_plain(kt, bias):
        s = _dot(q_aug, kaug_ref[0, 0, kt])
        if bias is not None:
            s = s + bias
        softmax_into(s, kt, p0_s, a0_s)
        values_from(p0_s, a0_s, kt)

    @pl.when(kd % 2 == 1)
    def _():
        tile_plain(kd - 1, None)

    tile_plain(kd, jnp.concatenate([cbias_ref[(t0 % tk) // tq]] * HPG, axis=0))
    acc = acc_s[...]
    o_sel = acc[:, :HEAD_DIM] / acc[:, HEAD_DIM:HEAD_DIM + 1]

    o_ref[0] = _gate_mix(gl_ref[0], o_cmp, o_sel, o_win, tq)


def _prompt_consts(t_len, tq, tk, tkw):
    n_c = t_len // CMP_BLOCK
    n_pages = t_len // PAGE_SIZE
    rows = HPG * tq
    r = np.arange(n_c)
    bpp = PAGE_SIZE // CMP_BLOCK
    ids = np.where(r < n_pages, bpp * r, bpp * (r - n_pages) + 1)
    onehot = ids[:, None] == (np.arange(t_len) // CMP_BLOCK)[None, :]
    e = onehot.reshape(n_c, t_len // tk, tk).transpose(1, 0, 2)
    lane = np.arange(tk)
    prow = np.zeros((HEAD_DIM, tk), np.float32)
    prow[0:3] = lane // 256
    prow[3:6] = lane % 256
    slopes = (2.0 ** (-8.0 * np.arange(1, N_HEADS + 1) / N_HEADS)).astype(np.float32)
    srow = np.repeat(slopes.reshape(N_KV, HPG), tq, axis=1)
    s1 = srow.astype(jnp.bfloat16).astype(np.float32)
    s2 = (srow - s1).astype(jnp.bfloat16).astype(np.float32)
    s3 = (srow - s1 - s2).astype(jnp.bfloat16).astype(np.float32)
    coef = np.zeros((N_KV, rows, HEAD_DIM), np.float32)
    for k, piece in enumerate((s1, s2, s3)):
        coef[:, :, k] = 256.0 * piece
        coef[:, :, 3 + k] = piece
    a = np.tile(np.arange(tq), HPG)[:, None]
    nslots = (WINDOW + tq) // tkw
    dpos = a + WINDOW - np.arange(nslots * tkw)[None, :]
    band = np.where((dpos >= 0) & (dpos <= WINDOW), 0.0, NEG)
    wbias = -srow[:, :, None] * dpos[None] + band[None]
    off = np.arange(tk // tq)[:, None, None] * tq
    cbias = np.where(lane[None, None, :] <= off + np.arange(tq)[None, :, None], 0.0, NEG)
    return dict(
        scol=jnp.asarray(srow.reshape(N_KV, rows, 1), f32), srow=jnp.asarray(srow.reshape(N_KV, 1, rows), f32),
        coef=jnp.asarray(coef, bf16), e=jnp.asarray(e, bf16), prow=jnp.asarray(prow, bf16),
        ones=jnp.ones((16, tk), bf16), wbias=jnp.asarray(wbias, f32), cbias=jnp.asarray(cbias, f32),
        eye=jnp.eye(tq, dtype=bf16))


def _nsa_prompt(q, gl, kc, vc, kaug, vaug, kwin, c, tq):
    bsz, t_len, _ = q.shape
    tk = kaug.shape[-1]
    tkw = kwin.shape[-1]
    npb = t_len // PAGE_SIZE
    rows = HPG * tq
    kern = functools.partial(_nsa_prompt_kernel, t_len=t_len, tq=tq, tk=tk, tkw=tkw)
    augspec = lambda arr: pl.BlockSpec((1, 1) + arr.shape[2:], lambda b, g, i: (b, g, 0, 0, 0))
    cspec = pl.BlockSpec((1, npb, kc.shape[-1]), lambda b, g, i: (g, b, 0))
    perg = lambda arr: pl.BlockSpec((1,) + arr.shape[1:], lambda b, g, i: (g,) + (0,) * (arr.ndim - 1))
    whole = lambda arr: pl.BlockSpec(arr.shape, lambda b, g, i: (0,) * arr.ndim)
    return pl.pallas_call(
        kern,
        grid=(bsz, N_KV, t_len // tq),
        in_specs=[
            pl.BlockSpec((1, tq, QGRP), lambda b, g, i: (b, i, g)),
            pl.BlockSpec((1, tq, 128), lambda b, g, i: (b, i, g)),
            cspec, cspec,
            augspec(kaug), augspec(vaug),
            pl.BlockSpec((1, t_len // tkw, GRP, tkw), lambda b, g, i: (b, 0, g, 0)),
            perg(c["scol"]), perg(c["srow"]), perg(c["coef"]),
            perg(c["wbias"]), whole(c["cbias"]), whole(c["eye"]),
        ],
        out_specs=pl.BlockSpec((1, tq, QGRP), lambda b, g, i: (b, i, g)),
        out_shape=jax.ShapeDtypeStruct((bsz, t_len, N_KV * QGRP), f32),
        scratch_shapes=[
            pltpu.VMEM((rows, tk), f32), pltpu.VMEM((rows, tk), f32),
            pltpu.VMEM((rows, tk), bf16), pltpu.VMEM((rows, tk), bf16),
            pltpu.VMEM((rows, 1), f32), pltpu.VMEM((rows, 1), f32),
            pltpu.VMEM((rows, 1), f32), pltpu.VMEM((rows, vaug.shape[3]), f32),
        ],
        compiler_params=_params(("arbitrary", "arbitrary", "arbitrary")),
        name="nsa_prompt_attn",
    )(q, gl, kc, vc, kaug, vaug, kwin, c["scol"], c["srow"], c["coef"], c["wbias"], c["cbias"], c["eye"])


def _nsa_sample_kernel(pt_ref, q_ref, gl_ref, kcall_ref, vcall_ref, pid_ref, newsel_ref, swin_ref, nwin_ref,
                       slope_ref, *rest, s_len, past_len, pp, n_groups, lanes):
    page_refs = rest[:pp]
    o_ref = rest[pp]
    sel_s, ocmp_s, qbd_s, m_s, l_s, acc_s = rest[pp + 1:]
    j = pl.program_id(1)
    rows = HPG * s_len
    n_c = past_len // CMP_BLOCK
    npb = past_len // PAGE_SIZE
    jrow = _block_ids(npb, lanes, 1)
    jcol = _block_ids(npb, lanes, 0)
    q = q_ref[0].astype(f32)
    t_col = past_len + lax.broadcasted_iota(i32, (rows, 1), 0) % s_len
    tq_col = past_len + lax.broadcasted_iota(i32, (s_len, 1), 0)

    def q_of(g):
        return jnp.concatenate(
            [q[:, (g * HPG + h) * HEAD_DIM:(g * HPG + h + 1) * HEAD_DIM] for h in range(HPG)],
            axis=0).astype(bf16)

    groups = range(N_KV)
    all_rows = N_KV * rows
    slope_all = jnp.concatenate([slope_ref[g] for g in groups], axis=0)
    t_all = past_len + lax.broadcasted_iota(i32, (all_rows, 1), 0) % s_len

    @pl.when(j == 0)
    def _():
        n_all = kcall_ref.shape[1]
        gather = (lax.broadcasted_iota(i32, (npb, n_all), 1) == pid_ref[0]).astype(bf16)
        zeros = jnp.zeros((rows, HEAD_DIM), bf16)
        for g in groups:
            kc = _unpage(_dot(gather, kcall_ref[g])).astype(bf16)
            vc = _unpage(_dot(gather, vcall_ref[g])).astype(bf16)
            q4 = q_of(g)
            o_cmp, imp = _cmp_branch(q4, kc, vc, slope_ref[g], t_col, s_len, jrow[:, :n_c])
            imp = jnp.concatenate([imp, jnp.zeros((s_len, lanes - n_c), f32)], axis=1)
            sel_s[g * s_len:(g + 1) * s_len] = _select_blocks(imp, tq_col, n_c + 1, jrow)
            ocmp_s[g] = o_cmp
            qbd_s[g * rows:(g + 1) * rows] = jnp.concatenate([q4 if gg == g else zeros for gg in groups], axis=1)
        m_s[...] = jnp.full((all_rows, 1), NEG, f32)
        l_s[...] = jnp.zeros((all_rows, 1), f32)
        acc_s[...] = jnp.zeros(acc_s.shape, f32)

    def stack(keys):
        kt = jnp.concatenate([keys[g * GRP:g * GRP + HEAD_DIM] for g in groups], axis=0).astype(bf16)
        vt = jnp.concatenate([keys[g * GRP + HEAD_DIM:(g + 1) * GRP] for g in groups], axis=0).astype(bf16)
        return kt, vt

    def attend(keys, pos):
        onehot = (jcol == pos // CMP_BLOCK).astype(bf16)
        msk = _dot(sel_s[...].astype(bf16), onehot)
        msk = jnp.concatenate([msk[g * s_len:(g + 1) * s_len] for g in groups for _ in range(HPG)], axis=0)
        valid = (msk > 0.5) & (pos <= t_all)
        kt, vt = stack(keys)
        m_new, l_new, acc = _flash_update((m_s[...], l_s[...], acc_s[...]), qbd_s[...], kt, vt, slope_all,
                                          valid, (pos - past_len).astype(f32))
        m_s[...] = m_new
        l_s[...] = l_new
        acc_s[...] = acc

    nk = pp * PAGE_SIZE
    keys = jnp.concatenate([r[0] for r in page_refs], axis=1)
    attend(keys, j * nk + lax.broadcasted_iota(i32, (1, nk), 1))

    @pl.when(j == n_groups - 1)
    def _():
        new_keys = newsel_ref[0]
        attend(new_keys, past_len + lax.broadcasted_iota(i32, (1, new_keys.shape[1]), 1))
        o_sel = acc_s[...] / l_s[...]
        carry = (jnp.full((all_rows, 1), NEG, f32), jnp.zeros((all_rows, 1), f32),
                 jnp.zeros((all_rows, N_KV * HEAD_DIM), f32))
        for wkeys, wbase in ((swin_ref[0], past_len - swin_ref.shape[2]), (nwin_ref[0], past_len)):
            wpos = wbase + lax.broadcasted_iota(i32, (1, wkeys.shape[1]), 1)
            dpos = t_all - wpos
            valid = (dpos >= 0) & (dpos <= WINDOW)
            kt, vt = stack(wkeys)
            carry = _flash_update(carry, qbd_s[...], kt, vt, slope_all, valid, (wpos - past_len).astype(f32))
        o_win = carry[2] / carry[1]
        own = lambda o, g: o[g * rows:(g + 1) * rows, g * HEAD_DIM:(g + 1) * HEAD_DIM]
        gl = gl_ref[0]
        o_ref[0] = jnp.concatenate(
            [_gate_mix(gl[:, g * 128:(g + 1) * 128], ocmp_s[g], own(o_sel, g), own(o_win, g), s_len)
             for g in groups], axis=1)


def _nsa_sample(page_table, q, gl, kc_all, vc_all, pids, new_sel, state_win, new_win, slopes, cache_sel, pp):
    bsz, s_len, hd = q.shape
    n_pages = page_table.shape[1]
    past_len = n_pages * PAGE_SIZE
    pp = min(pp, n_pages)
    n_groups = n_pages // pp
    n_c = past_len // CMP_BLOCK
    lanes = -(-(n_c + 2) // 128) * 128
    rows = HPG * s_len
    kvw = cache_sel.shape[1]
    perb = lambda b, j, pt: (b, 0, 0)

    def page_spec(i):
        return pl.BlockSpec((1, kvw, PAGE_SIZE), lambda b, j, pt: (pt[b, j * pp + i], 0, 0))

    kern = functools.partial(_nsa_sample_kernel, s_len=s_len, past_len=past_len, pp=pp,
                             n_groups=n_groups, lanes=lanes)
    grid_spec = pltpu.PrefetchScalarGridSpec(
        num_scalar_prefetch=1,
        grid=(bsz, n_groups),
        in_specs=[
            pl.BlockSpec((1, s_len, hd), perb),
            pl.BlockSpec((1, s_len, gl.shape[-1]), perb),
            pl.BlockSpec(kc_all.shape, lambda b, j, pt: (0, 0, 0)),
            pl.BlockSpec(vc_all.shape, lambda b, j, pt: (0, 0, 0)),
            pl.BlockSpec((1, n_pages, 1), perb),
            pl.BlockSpec((1,) + new_sel.shape[1:], perb),
            pl.BlockSpec((1,) + state_win.shape[1:], perb),
            pl.BlockSpec((1,) + new_win.shape[1:], perb),
            pl.BlockSpec(slopes.shape, lambda b, j, pt: (0, 0, 0)),
        ] + [page_spec(i) for i in range(pp)],
        out_specs=pl.BlockSpec((1, s_len, hd), perb),
        scratch_shapes=[
            pltpu.VMEM((N_KV * s_len, lanes), f32),
            pltpu.VMEM((N_KV, rows, HEAD_DIM), f32),
            pltpu.VMEM((N_KV * rows, N_KV * HEAD_DIM), bf16),
            pltpu.VMEM((N_KV * rows, 1), f32),
            pltpu.VMEM((N_KV * rows, 1), f32),
            pltpu.VMEM((N_KV * rows, N_KV * HEAD_DIM), f32),
        ],
    )
    return pl.pallas_call(
        kern,
        grid_spec=grid_spec,
        out_shape=jax.ShapeDtypeStruct((bsz, s_len, hd), f32),
        compiler_params=_params(("arbitrary", "arbitrary")),
        name="nsa_sample_attn",
    )(page_table, q, gl, kc_all, vc_all, pids, new_sel, state_win, new_win, slopes, *([cache_sel] * pp))


def _out_kernel(o_ref, z_ref, x_ref, mod_ref, w_ref, lng_ref, lnb_ref, xo_ref, *, nb, tr, d):
    y = _dot((o_ref[...] * _silu(z_ref[...])).astype(bf16), w_ref[...]).reshape(nb, tr, d)
    gate = mod_ref[...][:, 2:3, :]
    xo_ref[...] = _layer_norm(DN_ALPHA * x_ref[...] + (1.0 + gate) * y, lng_ref[...], lnb_ref[...])


def _out_proj(o, z, x, mod, w_out, ln_g, ln_b, nb, tr):
    bsz, t, d = x.shape
    hd = o.shape[-1]
    nt = t // tr
    const2 = lambda b, i: (0, 0)
    kern = functools.partial(_out_kernel, nb=nb, tr=tr, d=d)
    return pl.pallas_call(
        kern,
        grid=(bsz // nb, nt),
        in_specs=[
            pl.BlockSpec((nb * tr, hd), lambda b, i: (b * nt + i, 0)),
            pl.BlockSpec((nb * tr, hd), lambda b, i: (b * nt + i, 0)),
            pl.BlockSpec((nb, tr, d), lambda b, i: (b, i, 0)),
            pl.BlockSpec((nb, 3, d), lambda b, i: (b, 0, 0)),
            pl.BlockSpec(w_out.shape, const2),
            pl.BlockSpec((1, d), const2),
            pl.BlockSpec((1, d), const2),
        ],
        out_specs=pl.BlockSpec((nb, tr, d), lambda b, i: (b, i, 0)),
        out_shape=jax.ShapeDtypeStruct((bsz, t, d), f32),
        compiler_params=_params(("arbitrary", "arbitrary")),
        name="nsa_out",
    )(o, z, x, mod, w_out, ln_g, ln_b)


def _head_slopes(nq):
    s = 2.0 ** (-8.0 * np.arange(1, N_HEADS + 1) / N_HEADS)
    s = np.repeat(s.reshape(N_KV, HPG), nq, axis=1)
    return jnp.asarray(s.reshape(N_KV, HPG * nq, 1), f32)


def kernel(x_prompt, x_sample, c_prompt, c_sample, state_h, state_conv, cache_cmp, cache_sel, state_win, page_table, w_ada, b_ada, ln_g, ln_b, w_in_a, conv_w_a, conv_b_a, w_r_a, b_r_a, w_i_a, b_i_a, lam_a, w_out_a, w_kv, phi_pe, w_phi1, b_phi1, w_phi2, b_phi2, w_in_b, b_gate_b, w_out_b):
    bp, t_len, d = x_prompt.shape
    bs, s_len, _ = x_sample.shape
    n_pages = page_table.shape[1]
    past_len = n_pages * PAGE_SIZE
    hd = N_HEADS * HEAD_DIM
    kvw = N_KV * GRP
    assert s_len <= CMP_BLOCK and t_len % 128 == 0 and w_ada.shape[0] == DEPTH == 2

    n_c = bp + bs
    pad = -n_c % 8
    c_all = jnp.concatenate([c_prompt, c_sample, jnp.zeros((pad, d), f32)], axis=0)
    mods = _ada_mod(c_all, w_ada, b_ada).reshape(DEPTH, n_c + pad, 3, d)
    mod_p = mods[:, :bp]
    mod_s = mods[:, bp:n_c]

    rg = d // N_RG_BLOCKS
    w_in0 = w_in_a[0].astype(bf16)
    w_gate = jnp.concatenate([w_r_a[0], w_i_a[0]], axis=-1).astype(bf16)
    w_out0 = w_out_a[0].astype(bf16)
    row = lambda v: v.reshape(1, -1)
    rg_args = (w_in0, conv_w_a[0], row(conv_b_a[0]), w_gate, row(b_r_a[0]), row(b_i_a[0]), row(lam_a[0]),
               w_out0, row(ln_g[0]), row(ln_b[0]))
    xp1, h_p, conv_p = _rglru_prompt(x_prompt, mod_p[0], jnp.zeros((bp, 1, d), f32),
                                     jnp.zeros((bp, CONV_W - 1, d), f32), *rg_args, tc=256)
    xs1_tm, h_s, conv_s_tm = _rglru_sample(
        x_sample.transpose(1, 0, 2), mod_s[0].transpose(1, 0, 2), state_h[0],
        state_conv[0].transpose(1, 0, 2), *rg_args)
    xs1 = xs1_tm.transpose(1, 0, 2)
    new_h_p = h_p.reshape(1, bp, d)
    new_conv_p = conv_p.reshape(1, bp, CONV_W - 1, d)
    new_h_s = h_s.reshape(1, bs, d)
    new_conv_s = conv_s_tm.transpose(1, 0, 2).reshape(1, bs, CONV_W - 1, d)

    w_b = w_in_b[0]
    w_q = w_b[:, :hd].astype(bf16)
    w_z = w_b[:, hd:2 * hd].astype(bf16)
    gpg = HPG * 3
    w_g = jnp.pad(w_b[:, 2 * hd:].reshape(d, N_KV, gpg), ((0, 0), (0, 0), (0, 128 - gpg)))
    w_g = w_g.reshape(d, N_KV * 128).astype(bf16)
    b_g = jnp.pad(b_gate_b[0].reshape(N_KV, gpg), ((0, 0), (0, 128 - gpg))).reshape(1, N_KV * 128)
    tmin = lambda v: v.transpose(0, 2, 3, 4, 1)
    tq = min(128, t_len)
    tk = min(512, t_len)
    consts = _prompt_consts(t_len, tq, tk, PAGE_SIZE)
    cmpt_p, selt_p, wint_p, cmp_pages, kaug_p, vaug_p, wint_tiles, q_p, z_p, gl_p = _proj_prompt(
        xp1, mod_p[1], w_kv.T.astype(bf16), w_q, w_z, w_g, b_g, consts["e"], consts["prow"], consts["ones"],
        tr=tk)
    cmp_s, sel_s, win_s, selb_s, winb_s, q_s, z_s, gl_s = _proj(
        xs1, mod_s[1], w_kv.astype(bf16), w_q, w_z, w_g, b_g, nb=bs, tr=s_len)
    per_p = lambda v: v.reshape(bp, t_len, v.shape[-1])
    per_s = lambda v: v.reshape(bs, s_len, v.shape[-1])

    bpp = PAGE_SIZE // CMP_BLOCK
    dphi = w_phi2.shape[1]
    diag2 = lambda w: jnp.concatenate([jnp.concatenate([w, jnp.zeros_like(w)], axis=-1),
                                       jnp.concatenate([jnp.zeros_like(w), w], axis=-1)], axis=-2)
    pet = jnp.tile(phi_pe.transpose(1, 2, 0), (1, 1, bpp))
    w1 = diag2(w_phi1.transpose(0, 2, 1, 3)).astype(bf16)
    w1 = w1.reshape(2, HEAD_DIM // 2, 2 * PAGE_SIZE, bpp * dphi)
    b1 = jnp.tile(b_phi1, (1, bpp)).reshape(2, 1, bpp * dphi)
    w2 = diag2(w_phi2).astype(bf16)
    b2 = jnp.tile(b_phi2, (1, bpp)).reshape(2, 1, bpp * HEAD_DIM)
    kc_p, vc_p = _compress(cmp_pages.reshape(-1, PAGE_SIZE), pet, w1, b1, w2, b2, npg=64, out_dtype=f32)
    kc_all, vc_all = _compress(tmin(cache_cmp).reshape(-1, PAGE_SIZE), pet, w1, b1, w2, b2,
                               npg=64, out_dtype=bf16)

    o_p = _nsa_prompt(per_p(q_p), per_p(gl_p), kc_p, vc_p, kaug_p, vaug_p, wint_tiles, consts, tq=tq)
    tpad = lambda v: jnp.pad(per_s(v), ((0, 0), (0, PAGE_SIZE - s_len), (0, 0))).transpose(0, 2, 1)
    o_s = _nsa_sample(page_table, per_s(q_s), per_s(gl_s), kc_all, vc_all, page_table.reshape(bs, n_pages, 1),
                      tpad(selb_s), tmin(state_win).reshape(bs, kvw, -1), tpad(winb_s), _head_slopes(s_len),
                      tmin(cache_sel).reshape(-1, kvw, PAGE_SIZE), pp=8)

    w_out1 = w_out_b[0].astype(bf16)
    y_p = _out_proj(o_p.reshape(bp * t_len, hd), z_p, xp1, mod_p[1], w_out1, row(ln_g[1]), row(ln_b[1]),
                    nb=1, tr=min(512, t_len))
    y_s = _out_proj(o_s.reshape(bs * s_len, hd), z_s, xs1, mod_s[1], w_out1, row(ln_g[1]), row(ln_b[1]),
                    nb=bs, tr=s_len)

    kv5 = lambda v: v.reshape(bs, s_len, N_KV, 2, HEAD_DIM)
    kv5t = lambda v: v.reshape(bp, N_KV, 2, HEAD_DIM, -1).transpose(0, 4, 1, 2, 3)
    wb = state_win.shape[1]
    new_win_p = kv5t(wint_p[:, :, -min(WINDOW, t_len):])
    new_win_s = jnp.concatenate([state_win, kv5(win_s)], axis=1)[:, -wb:]
    return (y_p, y_s, kv5t(cmpt_p), kv5t(selt_p), new_win_p, new_h_p, new_conv_p,
            kv5(cmp_s), kv5(sel_s), new_win_s, new_h_s, new_conv_s)
```

```python
import functools
import math

import numpy as np
import jax
import jax.numpy as jnp
from jax import lax
from jax.experimental import pallas as pl
from jax.experimental.pallas import tpu as pltpu

f32 = jnp.float32
bf16 = jnp.bfloat16
i32 = jnp.int32

DEPTH = 2
N_RG_BLOCKS = 8
CONV_W = 4
RG_C = 8.0
N_HEADS = 16
HEAD_DIM = 64
N_KV = 4
HPG = N_HEADS // N_KV
CMP_BLOCK = 64
N_SEL = 16
WINDOW = 512
PAGE_SIZE = 128
DN_ALPHA = (2.0 * DEPTH) ** 0.25
LN_EPS = 1e-5
NEG = -1e30
FORCED = 1e6
REMOVED = -3e38
GRP = 2 * HEAD_DIM
QGRP = HPG * HEAD_DIM
VMEM_LIMIT_BYTES = 56 * 1024 * 1024


def _params(sem):
    return pltpu.CompilerParams(dimension_semantics=sem, vmem_limit_bytes=VMEM_LIMIT_BYTES)


def _dot(a, b):
    return jnp.dot(a, b, preferred_element_type=f32)


def _dot_nt(a, b):
    return lax.dot_general(a, b, (((1,), (1,)), ((), ())), preferred_element_type=f32)


def _sigmoid(x):
    return 1.0 / (1.0 + jnp.exp(-x))


def _silu(x):
    return x * _sigmoid(x)


def _log1p(e):
    u = 1.0 + e
    dlt = u - 1.0
    return jnp.where(dlt == 0.0, e, jnp.log(u) * (e / jnp.where(dlt == 0.0, 1.0, dlt)))


def _layer_norm(x, g, b):
    mu = jnp.mean(x, axis=-1, keepdims=True)
    xc = x - mu
    var = jnp.mean(xc * xc, axis=-1, keepdims=True)
    return xc * lax.rsqrt(var + LN_EPS) * g + b


def _ada_kernel(c_ref, w_ref, b_ref, o_ref):
    a = _silu(c_ref[...])
    o_ref[0] = jnp.dot(a, w_ref[0], preferred_element_type=f32,
                       precision=lax.Precision.HIGHEST) + b_ref[0]


def _ada_mod(c_all, w_ada, b_ada):
    rows, d = c_all.shape
    depth = w_ada.shape[0]
    return pl.pallas_call(
        _ada_kernel,
        grid=(depth, 3),
        in_specs=[
            pl.BlockSpec((rows, d), lambda l, n: (0, 0)),
            pl.BlockSpec((1, d, d), lambda l, n: (l, 0, n)),
            pl.BlockSpec((1, 1, d), lambda l, n: (l, 0, n)),
        ],
        out_specs=pl.BlockSpec((1, rows, d), lambda l, n: (l, 0, n)),
        out_shape=jax.ShapeDtypeStruct((depth, rows, 3 * d), f32),
        compiler_params=_params(("arbitrary", "arbitrary")),
        name="ada_mod",
    )(c_all, w_ada, b_ada.reshape(depth, 1, 3 * d))


def _rglru_gates(xc, wg_ref, br, bi, lam):
    xcb = xc.astype(bf16)
    rg = xc.shape[1] // N_RG_BLOCKS
    rs, is_ = [], []
    for n in range(N_RG_BLOCKS):
        g = _dot(xcb[:, n * rg:(n + 1) * rg], wg_ref[n])
        rs.append(g[:, :rg])
        is_.append(g[:, rg:])
    r = _sigmoid(jnp.concatenate(rs, axis=1) + br)
    i = _sigmoid(jnp.concatenate(is_, axis=1) + bi)
    nl = -lam
    softplus = jnp.maximum(nl, 0.0) + _log1p(jnp.exp(-jnp.abs(nl)))
    log_a = (-RG_C * softplus) * r
    a = jnp.exp(log_a)
    gain = jnp.sqrt(jnp.maximum(-jnp.tanh(log_a) * (a * a + 1.0), 0.0))
    b = gain * i * xc
    return a, b


def _rglru_prompt_kernel(x_ref, mod_ref, h0_ref, c0_ref, win_ref, cw_ref, cb_ref, wg_ref, br_ref,
                         bi_ref, lam_ref, wout_ref, lng_ref, lnb_ref,
                         xo_ref, hl_ref, cl_ref, xbuf, a_s, b_s, hs_s, hc, *, tc, d):
    t = pl.program_id(1)

    @pl.when(t == 0)
    def _():
        xbuf[0:8, :] = jnp.zeros((8, d), f32)
        xbuf[8 - (CONV_W - 1):8, :] = c0_ref[0]
        hc[...] = jnp.broadcast_to(h0_ref[0], (8, d))

    x = x_ref[0]
    mod = mod_ref[0]
    shift, scale, gate = mod[0:1], mod[1:2], mod[2:3]
    m = x * (1.0 + scale) + shift
    u = _dot(m.astype(bf16), win_ref[...])
    xb = u[:, :d]
    zg = u[:, d:]
    xbuf[8:8 + tc, :] = xb
    base = 8 - (CONV_W - 1)
    xc = cb_ref[...] + xbuf[base:base + tc, :] * cw_ref[0:1, :]
    for k in range(1, CONV_W):
        xc = xc + xbuf[base + k:base + k + tc, :] * cw_ref[k:k + 1, :]
    tail = xbuf[8 + tc - (CONV_W - 1):8 + tc, :]
    xbuf[base:8, :] = tail
    cl_ref[0] = tail

    a, b = _rglru_gates(xc, wg_ref, br_ref[...], bi_ref[...], lam_ref[...])

    rowi = lax.broadcasted_iota(i32, (tc, 1), 0) % 8
    for s in (1, 2, 4):
        ok = rowi >= s
        a_sh = pltpu.roll(a, s, 0)
        b_sh = pltpu.roll(b, s, 0)
        b = jnp.where(ok, a * b_sh + b, b)
        a = jnp.where(ok, a * a_sh, a)
    a_s[...] = a
    b_s[...] = b

    def tile_step(j, hprev):
        r0 = pl.multiple_of(j * 8, 8)
        ht = a_s[pl.ds(r0, 8), :] * hprev + b_s[pl.ds(r0, 8), :]
        hs_s[pl.ds(r0, 8), :] = ht
        return jnp.broadcast_to(ht[7:8, :], (8, d))

    hlast = lax.fori_loop(0, tc // 8, tile_step, hc[...])
    hc[...] = hlast
    hl_ref[0] = hlast[0:1, :]

    y = _dot((hs_s[...] * _silu(zg)).astype(bf16), wout_ref[...])
    xo_ref[0] = _layer_norm(DN_ALPHA * x + (1.0 + gate) * y, lng_ref[...], lnb_ref[...])


def _rglru_prompt(x, mod, h0, c0, w_in, conv_w, conv_b, w_gate, b_r, b_i, lam, w_out, ln_g, ln_b, tc):
    bsz, t, d = x.shape
    tc = min(tc, t)
    const2 = lambda b, i: (0, 0)
    const3 = lambda b, i: (0, 0, 0)
    perb = lambda b, i: (b, 0, 0)
    kern = functools.partial(_rglru_prompt_kernel, tc=tc, d=d)
    return pl.pallas_call(
        kern,
        grid=(bsz, t // tc),
        in_specs=[
            pl.BlockSpec((1, tc, d), lambda b, i: (b, i, 0)),
            pl.BlockSpec((1, 3, d), perb),
            pl.BlockSpec((1, 1, d), perb),
            pl.BlockSpec((1, CONV_W - 1, d), perb),
            pl.BlockSpec(w_in.shape, const2),
            pl.BlockSpec(conv_w.shape, const2),
            pl.BlockSpec((1, d), const2),
            pl.BlockSpec(w_gate.shape, const3),
            pl.BlockSpec((1, d), const2),
            pl.BlockSpec((1, d), const2),
            pl.BlockSpec((1, d), const2),
            pl.BlockSpec(w_out.shape, const2),
            pl.BlockSpec((1, d), const2),
            pl.BlockSpec((1, d), const2),
        ],
        out_specs=[
            pl.BlockSpec((1, tc, d), lambda b, i: (b, i, 0)),
            pl.BlockSpec((1, 1, d), perb),
            pl.BlockSpec((1, CONV_W - 1, d), perb),
        ],
        out_shape=[
            jax.ShapeDtypeStruct((bsz, t, d), f32),
            jax.ShapeDtypeStruct((bsz, 1, d), f32),
            jax.ShapeDtypeStruct((bsz, CONV_W - 1, d), f32),
        ],
        scratch_shapes=[
            pltpu.VMEM((tc + 8, d), f32),
            pltpu.VMEM((tc, d), f32),
            pltpu.VMEM((tc, d), f32),
            pltpu.VMEM((tc, d), f32),
            pltpu.VMEM((8, d), f32),
        ],
        compiler_params=_params(("arbitrary", "arbitrary")),
        name="rglru_prompt",
    )(x, mod, h0, c0, w_in, conv_w, conv_b, w_gate, b_r, b_i, lam, w_out, ln_g, ln_b)


def _rglru_sample_kernel(x_ref, mod_ref, h0_ref, c0_ref, win_ref, cw_ref, cb_ref, wg_ref, br_ref,
                         bi_ref, lam_ref, wout_ref, lng_ref, lnb_ref,
                         xo_ref, hl_ref, cl_ref, *, s_len, bsz, d):
    x = x_ref[...]
    mod = mod_ref[...]
    shift, scale, gate = mod[0:1], mod[1:2], mod[2:3]
    m = x * (1.0 + scale) + shift
    u = _dot(m.reshape(s_len * bsz, d).astype(bf16), win_ref[...])
    xb = u[:, :d].reshape(s_len, bsz, d)
    zg = u[:, d:]
    xp = jnp.concatenate([c0_ref[...], xb], axis=0)
    cw = cw_ref[...]
    xc = cb_ref[...] + xp[0:s_len] * cw[0:1]
    for k in range(1, CONV_W):
        xc = xc + xp[k:k + s_len] * cw[k:k + 1]
    cl_ref[...] = xp[s_len:s_len + CONV_W - 1]
    a, b = _rglru_gates(xc.reshape(s_len * bsz, d), wg_ref, br_ref[...], bi_ref[...], lam_ref[...])
    h = h0_ref[...]
    hs = []
    for s in range(s_len):
        h = a[s * bsz:(s + 1) * bsz] * h + b[s * bsz:(s + 1) * bsz]
        hs.append(h)
    hl_ref[...] = h
    hs = jnp.concatenate(hs, axis=0)
    y = _dot((hs * _silu(zg)).astype(bf16), wout_ref[...])
    xo = _layer_norm(DN_ALPHA * x + (1.0 + gate) * y.reshape(s_len, bsz, d), lng_ref[...], lnb_ref[...])
    xo_ref[...] = xo


def _rglru_sample(x_tm, mod_tm, h0, c0_tm, w_in, conv_w, conv_b, w_gate, b_r, b_i, lam, w_out, ln_g, ln_b):
    s_len, bsz, d = x_tm.shape
    kern = functools.partial(_rglru_sample_kernel, s_len=s_len, bsz=bsz, d=d)
    return pl.pallas_call(
        kern,
        out_shape=[
            jax.ShapeDtypeStruct((s_len, bsz, d), f32),
            jax.ShapeDtypeStruct((bsz, d), f32),
            jax.ShapeDtypeStruct((CONV_W - 1, bsz, d), f32),
        ],
        compiler_params=pltpu.CompilerParams(vmem_limit_bytes=VMEM_LIMIT_BYTES),
        name="rglru_sample",
    )(x_tm, mod_tm, h0, c0_tm, w_in, conv_w, conv_b, w_gate, b_r, b_i, lam, w_out, ln_g, ln_b)


def _proj_kernel(x_ref, mod_ref, wkv_ref, wq_ref, wz_ref, wg_ref, bg_ref,
                 cmp_ref, sel_ref, win_ref, selb_ref, winb_ref, q_ref, z_ref, gl_ref, *, nb, tr, d):
    x = x_ref[...]
    mod = mod_ref[...]
    m = x * (1.0 + mod[:, 1:2, :]) + mod[:, 0:1, :]
    xb = x.reshape(nb * tr, d).astype(bf16)
    mb = m.reshape(nb * tr, d).astype(bf16)
    kv = _dot(xb, wkv_ref[...])
    w = kv.shape[1] // 3
    sel = kv[:, w:2 * w]
    win = kv[:, 2 * w:]
    cmp_ref[...] = kv[:, :w]
    sel_ref[...] = sel
    win_ref[...] = win
    selb_ref[...] = sel.astype(bf16)
    winb_ref[...] = win.astype(bf16)
    q_ref[...] = (_dot(mb, wq_ref[...]) * (HEAD_DIM ** -0.5)).astype(bf16)
    z_ref[...] = _dot(mb, wz_ref[...])
    gl_ref[...] = _dot(mb, wg_ref[...]) + bg_ref[...]


def _proj(x, mod, w_kv, w_q, w_z, w_g, b_g, nb, tr):
    bsz, t, d = x.shape
    kvw = w_kv.shape[1] // 3
    hd = w_q.shape[1]
    gw = w_g.shape[1]
    nt = t // tr
    const2 = lambda b, i: (0, 0)
    blk = lambda width: pl.BlockSpec((nb * tr, width), lambda b, i: (b * nt + i, 0))
    out = lambda width, dt: jax.ShapeDtypeStruct((bsz * t, width), dt)
    kern = functools.partial(_proj_kernel, nb=nb, tr=tr, d=d)
    return pl.pallas_call(
        kern,
        grid=(bsz // nb, nt),
        in_specs=[
            pl.BlockSpec((nb, tr, d), lambda b, i: (b, i, 0)),
            pl.BlockSpec((nb, 3, d), lambda b, i: (b, 0, 0)),
            pl.BlockSpec(w_kv.shape, const2),
            pl.BlockSpec(w_q.shape, const2),
            pl.BlockSpec(w_z.shape, const2),
            pl.BlockSpec(w_g.shape, const2),
            pl.BlockSpec((1, gw), const2),
        ],
        out_specs=[blk(kvw), blk(kvw), blk(kvw), blk(kvw), blk(kvw), blk(hd), blk(hd), blk(gw)],
        out_shape=[out(kvw, f32), out(kvw, f32), out(kvw, f32), out(kvw, bf16), out(kvw, bf16),
                   out(hd, bf16), out(hd, f32), out(gw, f32)],
        compiler_params=_params(("arbitrary", "arbitrary")),
        name="nsa_proj",
    )(x, mod, w_kv, w_q, w_z, w_g, b_g)


def _proj_prompt_kernel(x_ref, mod_ref, wkvt_ref, wq_ref, wz_ref, wg_ref, bg_ref, e_ref, prow_ref, ones_ref,
                        cmpt_ref, selt_ref, wint_ref, cmppg_ref, kaug_ref, vaug_ref, wintt_ref, q_ref, z_ref,
                        gl_ref, *, tr):
    x = x_ref[0]
    mod = mod_ref[0]
    m = x * (1.0 + mod[1:2]) + mod[0:1]
    mb = m.astype(bf16)
    kvt = _dot_nt(wkvt_ref[...], x.astype(bf16))
    kvw = kvt.shape[0] // 3
    cmpt = kvt[:kvw]
    selt = kvt[kvw:2 * kvw]
    wint = kvt[2 * kvw:]
    cmpt_ref[0] = cmpt
    selt_ref[0] = selt
    wint_ref[0] = wint
    selb = selt.astype(bf16)
    n_c = e_ref.shape[1]
    for g in range(N_KV):
        kaug_ref[0, g, 0, 0:n_c] = e_ref[0]
        kaug_ref[0, g, 0, n_c:n_c + HEAD_DIM] = selb[g * GRP:g * GRP + HEAD_DIM]
        kaug_ref[0, g, 0, n_c + HEAD_DIM:] = prow_ref[...]
        vaug_ref[0, g, 0, 0:HEAD_DIM] = selb[g * GRP + HEAD_DIM:(g + 1) * GRP]
        vaug_ref[0, g, 0, HEAD_DIM:] = ones_ref[...]
    winb = wint.astype(bf16)
    for k in range(tr // PAGE_SIZE):
        cmppg_ref[k] = cmpt[:, k * PAGE_SIZE:(k + 1) * PAGE_SIZE]
        wintt_ref[0, k] = winb[:, k * PAGE_SIZE:(k + 1) * PAGE_SIZE]
    q_ref[...] = (_dot(mb, wq_ref[...]) * (HEAD_DIM ** -0.5)).astype(bf16)
    z_ref[...] = _dot(mb, wz_ref[...])
    gl_ref[...] = _dot(mb, wg_ref[...]) + bg_ref[...]


def _proj_prompt(x, mod, w_kvt, w_q, w_z, w_g, b_g, e, prow, ones, tr):
    bsz, t, d = x.shape
    kvw = w_kvt.shape[0] // 3
    hd = w_q.shape[1]
    gw = w_g.shape[1]
    nt = t // tr
    ppt = tr // PAGE_SIZE
    ka_rows = e.shape[1] + HEAD_DIM + prow.shape[0]
    va_rows = HEAD_DIM + ones.shape[0]
    const2 = lambda b, i: (0, 0)
    rows = lambda width: pl.BlockSpec((tr, width), lambda b, i: (b * nt + i, 0))
    tmin = pl.BlockSpec((1, kvw, tr), lambda b, i: (b, 0, i))
    kern = functools.partial(_proj_prompt_kernel, tr=tr)
    return pl.pallas_call(
        kern,
        grid=(bsz, nt),
        in_specs=[
            pl.BlockSpec((1, tr, d), lambda b, i: (b, i, 0)),
            pl.BlockSpec((1, 3, d), lambda b, i: (b, 0, 0)),
            pl.BlockSpec(w_kvt.shape, const2),
            pl.BlockSpec(w_q.shape, const2),
            pl.BlockSpec(w_z.shape, const2),
            pl.BlockSpec(w_g.shape, const2),
            pl.BlockSpec((1, gw), const2),
            pl.BlockSpec((1,) + e.shape[1:], lambda b, i: (i, 0, 0)),
            pl.BlockSpec(prow.shape, const2),
            pl.BlockSpec(ones.shape, const2),
        ],
        out_specs=[
            tmin, tmin, tmin,
            pl.BlockSpec((ppt, kvw, PAGE_SIZE), lambda b, i: (b * nt + i, 0, 0)),
            pl.BlockSpec((1, N_KV, 1, ka_rows, tr), lambda b, i: (b, 0, i, 0, 0)),
            pl.BlockSpec((1, N_KV, 1, va_rows, tr), lambda b, i: (b, 0, i, 0, 0)),
            pl.BlockSpec((1, ppt, kvw, PAGE_SIZE), lambda b, i: (b, i, 0, 0)),
            rows(hd), rows(hd), rows(gw),
        ],
        out_shape=[
            jax.ShapeDtypeStruct((bsz, kvw, t), f32),
            jax.ShapeDtypeStruct((bsz, kvw, t), f32),
            jax.ShapeDtypeStruct((bsz, kvw, t), f32),
            jax.ShapeDtypeStruct((bsz * t // PAGE_SIZE, kvw, PAGE_SIZE), f32),
            jax.ShapeDtypeStruct((bsz, N_KV, nt, ka_rows, tr), bf16),
            jax.ShapeDtypeStruct((bsz, N_KV, nt, va_rows, tr), bf16),
            jax.ShapeDtypeStruct((bsz, t // PAGE_SIZE, kvw, PAGE_SIZE), bf16),
            jax.ShapeDtypeStruct((bsz * t, hd), bf16),
            jax.ShapeDtypeStruct((bsz * t, hd), f32),
            jax.ShapeDtypeStruct((bsz * t, gw), f32),
        ],
        compiler_params=_params(("arbitrary", "arbitrary")),
        name="nsa_proj_prompt",
    )(x, mod, w_kvt, w_q, w_z, w_g, b_g, e, prow, ones)


def _compress_kernel(x_ref, pet_ref, w1_ref, b1_ref, w2_ref, b2_ref, kc_ref, vc_ref, *, npg):
    page_rows = N_KV * GRP

    def dim_rows(c, dd):
        return jnp.concatenate(
            [(x_ref[pl.ds((g * 2 + c) * HEAD_DIM + dd, npg, stride=page_rows), :]
              + pet_ref[c, dd:dd + 1, :]).astype(bf16) for g in range(N_KV)], axis=0)

    outs = []
    for c in range(2):
        acc = None
        for dp in range(HEAD_DIM // 2):
            lhs = jnp.concatenate([dim_rows(c, 2 * dp), dim_rows(c, 2 * dp + 1)], axis=1)
            part = _dot(lhs, w1_ref[c, dp])
            acc = part if acc is None else acc + part
        hid = _silu(acc + b1_ref[c])
        outs.append(_dot(hid.astype(bf16), w2_ref[c]) + b2_ref[c])
    for g in range(N_KV):
        kc_ref[g] = outs[0][g * npg:(g + 1) * npg, :].astype(kc_ref.dtype)
        vc_ref[g] = outs[1][g * npg:(g + 1) * npg, :].astype(vc_ref.dtype)


def _compress(x2d, pet, w1, b1, w2, b2, npg, out_dtype):
    rows, width = x2d.shape
    n_pages = rows // (N_KV * GRP)
    npg = min(npg, n_pages)
    while n_pages % npg:
        npg -= 8
    ow = (PAGE_SIZE // CMP_BLOCK) * HEAD_DIM
    const3 = lambda i: (0, 0, 0)
    kern = functools.partial(_compress_kernel, npg=npg)
    out = pl.BlockSpec((N_KV, npg, ow), lambda i: (0, i, 0))
    return pl.pallas_call(
        kern,
        grid=(n_pages // npg,),
        in_specs=[
            pl.BlockSpec((npg * N_KV * GRP, width), lambda i: (i, 0)),
            pl.BlockSpec(pet.shape, const3),
            pl.BlockSpec(w1.shape, lambda i: (0, 0, 0, 0), pipeline_mode=pl.Buffered(1)),
            pl.BlockSpec(b1.shape, const3),
            pl.BlockSpec(w2.shape, const3),
            pl.BlockSpec(b2.shape, const3),
        ],
        out_specs=[out, out],
        out_shape=[jax.ShapeDtypeStruct((N_KV, n_pages, ow), out_dtype)] * 2,
        compiler_params=_params(("arbitrary",)),
        name="nsa_compress",
    )(x2d, pet, w1, b1, w2, b2)


def _block_ids(n_pages, lanes, axis):
    shape = (1, lanes) if axis == 1 else (lanes, 1)
    r = lax.broadcasted_iota(i32, shape, axis)
    bpp = PAGE_SIZE // CMP_BLOCK
    perm = jnp.where(r < n_pages, bpp * r, bpp * (r - n_pages) + 1)
    return jnp.where(r < bpp * n_pages, perm, r)


def _unpage(x):
    return jnp.concatenate([x[:, :HEAD_DIM], x[:, HEAD_DIM:]], axis=0)


def _cmp_branch(q4, kc, vc, slope, t_col, nq, jrow):
    s = _dot_nt(q4, kc)
    c_end = (jrow + 1) * CMP_BLOCK - 1
    dist = t_col.astype(f32) - c_end.astype(f32)
    s = s - slope * dist
    mask = c_end <= t_col
    s = jnp.where(mask, s, NEG)
    e = jnp.exp(s - jnp.max(s, axis=-1, keepdims=True))
    p = e / jnp.sum(e, axis=-1, keepdims=True)
    p = jnp.where(mask, p, 0.0)
    o = _dot(p.astype(bf16), vc)
    imp = p[0:nq]
    for h in range(1, HPG):
        imp = imp + p[h * nq:(h + 1) * nq]
    return o, imp


def _select_blocks(imp, tq_col, n_sb, jr):
    nq, lanes = imp.shape
    cb = tq_col // CMP_BLOCK
    forced = (jr == 0) | (jr == cb) | (jr == cb - 1)
    causal = jr <= cb
    score = jnp.where(forced, FORCED, jnp.where(causal, imp, -1.0))
    score = jnp.where(jr < n_sb, score, REMOVED)
    jf = jr.astype(f32)
    sel = jnp.zeros((nq, lanes), f32)
    for _ in range(min(N_SEL, n_sb)):
        mx = jnp.max(score, axis=-1, keepdims=True)
        idx = jnp.min(jnp.where(score == mx, jf, 1e9), axis=-1, keepdims=True)
        hit = jf == idx
        sel = jnp.where(hit, 1.0, sel)
        score = jnp.where(hit, REMOVED, score)
    return sel


def _flash_update(carry, q4, kt, vt, slope, valid, pos_rel):
    m_i, l_i, acc = carry
    s = _dot(q4, kt) + slope * pos_rel
    s = jnp.where(valid, s, NEG)
    m_new = jnp.maximum(m_i, jnp.max(s, axis=-1, keepdims=True))
    alpha = jnp.exp(m_i - m_new)
    p = jnp.exp(s - m_new)
    l_new = alpha * l_i + jnp.sum(p, axis=-1, keepdims=True)
    acc = alpha * acc + _dot_nt(p.astype(bf16), vt)
    return m_new, l_new, acc


def _block_mask(selb, jcol, pos, reps):
    onehot = (jcol == pos // CMP_BLOCK).astype(bf16)
    msk = _dot(selb, onehot)
    return jnp.concatenate([msk] * reps, axis=0) > 0.5


def _gate_mix(gl, o_cmp, o_sel, o_win, nq):
    gs = _sigmoid(gl)
    outs = []
    for h in range(HPG):
        sl = slice(h * nq, (h + 1) * nq)
        outs.append(gs[:, 3 * h:3 * h + 1] * o_cmp[sl] + gs[:, 3 * h + 1:3 * h + 2] * o_sel[sl]
                    + gs[:, 3 * h + 2:3 * h + 3] * o_win[sl])
    return jnp.concatenate(outs, axis=1)


def _init_carry(rows):
    return (jnp.full((rows, 1), NEG, f32), jnp.zeros((rows, 1), f32), jnp.zeros((rows, HEAD_DIM), f32))


def _select_blocks_t(impt, tq_row, n_sb, jcol):
    n_c, nq = impt.shape
    cb = tq_row // CMP_BLOCK
    forced = (jcol == 0) | (jcol == cb) | (jcol == cb - 1)
    causal = jcol <= cb
    score = jnp.where(forced, FORCED, jnp.where(causal, impt, -1.0))
    score = jnp.where(jcol < n_sb, score, REMOVED)
    jf = jcol.astype(f32)
    sel = jnp.zeros((n_c, nq), f32)
    for _ in range(min(N_SEL, n_sb)):
        mx = jnp.max(score, axis=0, keepdims=True)
        idx = jnp.min(jnp.where(score == mx, jf, 1e9), axis=0, keepdims=True)
        hit = jf == idx
        sel = jnp.where(hit, 1.0, sel)
        score = jnp.where(hit, REMOVED, score)
    return sel


def _nsa_prompt_kernel(q_ref, gl_ref, kc_ref, vc_ref, kaug_ref, vaug_ref, kwin_ref, scol_ref, srow_ref, coef_ref,
                       wbias_ref, cbias_ref, eye_ref, tilemap_ref, o_ref,
                       s0_s, s1_s, p0_s, p1_s, a0_s, a1_s, m_s, acc_s, tiles_s, *, t_len, tq, tk, tkw):
    qi = pl.program_id(2)
    t0 = qi * tq
    rows = HPG * tq
    q = q_ref[0]
    q4 = jnp.concatenate([q[:, h * HEAD_DIM:(h + 1) * HEAD_DIM] for h in range(HPG)], axis=0)
    slope = scol_ref[0]
    t_col = t0 + lax.broadcasted_iota(i32, (rows, 1), 0) % tq

    n_c = t_len // CMP_BLOCK
    n_pages = t_len // PAGE_SIZE
    jrow = _block_ids(n_pages, n_c, 1)
    jcol = _block_ids(n_pages, n_c, 0)
    kc = _unpage(kc_ref[0]).astype(bf16)
    vc = _unpage(vc_ref[0]).astype(bf16)
    o_cmp, _ = _cmp_branch(q4, kc, vc, slope, t_col, tq, jrow)

    t_row = t0 + lax.broadcasted_iota(i32, (1, rows), 1) % tq
    c_end = (jcol + 1) * CMP_BLOCK - 1
    st = _dot_nt(kc, q4) - srow_ref[0] * (t_row.astype(f32) - c_end.astype(f32))
    maskt = c_end <= t_row
    st = jnp.where(maskt, st, NEG)
    et = jnp.exp(st - jnp.max(st, axis=0, keepdims=True))
    pt = et / jnp.sum(et, axis=0, keepdims=True)
    pt = jnp.where(maskt, pt, 0.0)
    impt = pt[:, 0:tq]
    for h in range(1, HPG):
        impt = impt + pt[:, h * tq:(h + 1) * tq]
    selt = _select_blocks_t(impt, t0 + lax.broadcasted_iota(i32, (1, tq), 1), n_c, jcol)
    sel = _dot_nt(eye_ref[...], selt.astype(bf16))
    selbias = ((sel - 1.0) * -NEG).astype(bf16)
    q_aug = jnp.concatenate([jnp.concatenate([selbias] * HPG, axis=0), q4, coef_ref[0]], axis=1)

    nslots = (WINDOW + tq) // tkw
    base = (t0 - WINDOW) // tkw
    kts, vts, pens = [], [], []
    for m in range(nslots):
        kw = base + m
        kvw = kwin_ref[0, jnp.maximum(kw, 0)]
        kts.append(kvw[:HEAD_DIM])
        vts.append(kvw[HEAD_DIM:])
        pens.append(jnp.broadcast_to(jnp.where(kw >= 0, 0.0, NEG).astype(f32), (1, tkw)))
    sw = _dot(q4, jnp.concatenate(kts, axis=1)) + wbias_ref[0] + jnp.concatenate(pens, axis=1)
    pw = jnp.exp(sw - jnp.max(sw, axis=-1, keepdims=True))
    o_win = _dot_nt(pw.astype(bf16), jnp.concatenate(vts, axis=1)) / jnp.sum(pw, axis=-1, keepdims=True)

    m_s[...] = jnp.full((rows, 1), NEG, f32)
    acc_s[...] = jnp.zeros(acc_s.shape, f32)

    def scores_into(dst, kt):
        dst[...] = _dot(q_aug, kaug_ref[0, 0, kt])

    def softmax_into(s, kt, p_dst, a_dst):
        c = slope * (kt * tk - t0).astype(f32)
        m_i = m_s[...]
        m_new = jnp.maximum(m_i, jnp.max(s, axis=-1, keepdims=True) + c)
        p_dst[...] = jnp.exp(s - (m_new - c)).astype(bf16)
        a_dst[...] = jnp.exp(m_i - m_new)
        m_s[...] = m_new

    def values_from(p_src, a_src, kt):
        acc_s[...] = a_src[...] * acc_s[...] + _dot_nt(p_src[...], vaug_ref[0, 0, kt])

    kd = t0 // tk
    tile_hits = jnp.max(_dot(tilemap_ref[...], selt.astype(bf16)), axis=1, keepdims=True)
    n_act = jnp.int32(0)
    tiles_s[0] = 0
    for kt in range(t_len // tk):
        tiles_s[n_act] = kt
        hit = (jnp.max(tile_hits[kt:kt + 1, :]) > 0.5) & (kt < kd)
        n_act = n_act + hit.astype(i32)
    npairs = n_act // 2
    last_piped = jnp.maximum(2 * npairs - 1, 0)
    p1_s[...] = jnp.zeros(p1_s.shape, bf16)
    a1_s[...] = jnp.ones((rows, 1), f32)
    scores_into(s0_s, tiles_s[0])

    def pair(j, carry):
        ia = 2 * j
        ka = tiles_s[ia]
        kb = tiles_s[ia + 1]
        values_from(p1_s, a1_s, tiles_s[jnp.maximum(ia - 1, 0)])
        scores_into(s1_s, kb)
        softmax_into(s0_s[...], ka, p0_s, a0_s)
        values_from(p0_s, a0_s, ka)
        scores_into(s0_s, tiles_s[jnp.minimum(ia + 2, last_piped)])
        softmax_into(s1_s[...], kb, p1_s, a1_s)
        return carry

    lax.fori_loop(0, npairs, pair, 0)
    values_from(p1_s, a1_s, tiles_s[last_piped])

    def tile_plain(kt, bias):
        s = _dot(q_aug, kaug_ref[0, 0, kt])
        if bias is not None:
            s = s + bias
        softmax_into(s, kt, p0_s, a0_s)
        values_from(p0_s, a0_s, kt)

    @pl.when(n_act % 2 == 1)
    def _():
        tile_plain(tiles_s[jnp.maximum(n_act - 1, 0)], None)

    tile_plain(kd, jnp.concatenate([cbias_ref[(t0 % tk) // tq]] * HPG, axis=0))
    acc = acc_s[...]
    o_sel = acc[:, :HEAD_DIM] / acc[:, HEAD_DIM:HEAD_DIM + 1]

    o_ref[0] = _gate_mix(gl_ref[0], o_cmp, o_sel, o_win, tq)


def _prompt_consts(t_len, tq, tk, tkw):
    n_c = t_len // CMP_BLOCK
    n_pages = t_len // PAGE_SIZE
    rows = HPG * tq
    r = np.arange(n_c)
    bpp = PAGE_SIZE // CMP_BLOCK
    ids = np.where(r < n_pages, bpp * r, bpp * (r - n_pages) + 1)
    onehot = ids[:, None] == (np.arange(t_len) // CMP_BLOCK)[None, :]
    e = onehot.reshape(n_c, t_len // tk, tk).transpose(1, 0, 2)
    lane = np.arange(tk)
    prow = np.zeros((HEAD_DIM, tk), np.float32)
    prow[0:3] = lane // 256
    prow[3:6] = lane % 256
    slopes = (2.0 ** (-8.0 * np.arange(1, N_HEADS + 1) / N_HEADS)).astype(np.float32)
    srow = np.repeat(slopes.reshape(N_KV, HPG), tq, axis=1)
    s1 = srow.astype(jnp.bfloat16).astype(np.float32)
    s2 = (srow - s1).astype(jnp.bfloat16).astype(np.float32)
    s3 = (srow - s1 - s2).astype(jnp.bfloat16).astype(np.float32)
    coef = np.zeros((N_KV, rows, HEAD_DIM), np.float32)
    for k, piece in enumerate((s1, s2, s3)):
        coef[:, :, k] = 256.0 * piece
        coef[:, :, 3 + k] = piece
    a = np.tile(np.arange(tq), HPG)[:, None]
    nslots = (WINDOW + tq) // tkw
    dpos = a + WINDOW - np.arange(nslots * tkw)[None, :]
    band = np.where((dpos >= 0) & (dpos <= WINDOW), 0.0, NEG)
    wbias = -srow[:, :, None] * dpos[None] + band[None]
    off = np.arange(tk // tq)[:, None, None] * tq
    cbias = np.where(lane[None, None, :] <= off + np.arange(tq)[None, :, None], 0.0, NEG)
    n_tiles = t_len // tk
    tilemap = np.arange(-(-n_tiles // 16) * 16)[:, None] == (ids // (tk // CMP_BLOCK))[None, :]
    return dict(
        tilemap=jnp.asarray(tilemap, bf16),
        scol=jnp.asarray(srow.reshape(N_KV, rows, 1), f32), srow=jnp.asarray(srow.reshape(N_KV, 1, rows), f32),
        coef=jnp.asarray(coef, bf16), e=jnp.asarray(e, bf16), prow=jnp.asarray(prow, bf16),
        ones=jnp.ones((16, tk), bf16), wbias=jnp.asarray(wbias, f32), cbias=jnp.asarray(cbias, f32),
        eye=jnp.eye(tq, dtype=bf16))


def _nsa_prompt(q, gl, kc, vc, kaug, vaug, kwin, c, tq):
    bsz, t_len, _ = q.shape
    tk = kaug.shape[-1]
    tkw = kwin.shape[-1]
    npb = t_len // PAGE_SIZE
    rows = HPG * tq
    kern = functools.partial(_nsa_prompt_kernel, t_len=t_len, tq=tq, tk=tk, tkw=tkw)
    augspec = lambda arr: pl.BlockSpec((1, 1) + arr.shape[2:], lambda b, g, i: (b, g, 0, 0, 0))
    cspec = pl.BlockSpec((1, npb, kc.shape[-1]), lambda b, g, i: (g, b, 0))
    perg = lambda arr: pl.BlockSpec((1,) + arr.shape[1:], lambda b, g, i: (g,) + (0,) * (arr.ndim - 1))
    whole = lambda arr: pl.BlockSpec(arr.shape, lambda b, g, i: (0,) * arr.ndim)
    return pl.pallas_call(
        kern,
        grid=(bsz, N_KV, t_len // tq),
        in_specs=[
            pl.BlockSpec((1, tq, QGRP), lambda b, g, i: (b, i, g)),
            pl.BlockSpec((1, tq, 128), lambda b, g, i: (b, i, g)),
            cspec, cspec,
            augspec(kaug), augspec(vaug),
            pl.BlockSpec((1, t_len // tkw, GRP, tkw), lambda b, g, i: (b, 0, g, 0)),
            perg(c["scol"]), perg(c["srow"]), perg(c["coef"]),
            perg(c["wbias"]), whole(c["cbias"]), whole(c["eye"]), whole(c["tilemap"]),
        ],
        out_specs=pl.BlockSpec((1, tq, QGRP), lambda b, g, i: (b, i, g)),
        out_shape=jax.ShapeDtypeStruct((bsz, t_len, N_KV * QGRP), f32),
        scratch_shapes=[
            pltpu.VMEM((rows, tk), f32), pltpu.VMEM((rows, tk), f32),
            pltpu.VMEM((rows, tk), bf16), pltpu.VMEM((rows, tk), bf16),
            pltpu.VMEM((rows, 1), f32), pltpu.VMEM((rows, 1), f32),
            pltpu.VMEM((rows, 1), f32), pltpu.VMEM((rows, vaug.shape[3]), f32),
            pltpu.SMEM((t_len // tk + 8,), i32),
        ],
        compiler_params=_params(("arbitrary", "arbitrary", "arbitrary")),
        name="nsa_prompt_attn",
    )(q, gl, kc, vc, kaug, vaug, kwin, c["scol"], c["srow"], c["coef"], c["wbias"], c["cbias"], c["eye"],
      c["tilemap"])


def _nsa_sample_kernel(pt_ref, q_ref, gl_ref, kcall_ref, vcall_ref, pid_ref, newsel_ref, swin_ref, nwin_ref,
                       slope_ref, *rest, s_len, past_len, pp, n_groups, lanes):
    page_refs = rest[:pp]
    o_ref = rest[pp]
    sel_s, ocmp_s, qbd_s, m_s, l_s, acc_s = rest[pp + 1:]
    j = pl.program_id(1)
    rows = HPG * s_len
    n_c = past_len // CMP_BLOCK
    npb = past_len // PAGE_SIZE
    jrow = _block_ids(npb, lanes, 1)
    jcol = _block_ids(npb, lanes, 0)
    q = q_ref[0].astype(f32)
    t_col = past_len + lax.broadcasted_iota(i32, (rows, 1), 0) % s_len
    tq_col = past_len + lax.broadcasted_iota(i32, (s_len, 1), 0)

    def q_of(g):
        return jnp.concatenate(
            [q[:, (g * HPG + h) * HEAD_DIM:(g * HPG + h + 1) * HEAD_DIM] for h in range(HPG)],
            axis=0).astype(bf16)

    groups = range(N_KV)
    all_rows = N_KV * rows
    slope_all = jnp.concatenate([slope_ref[g] for g in groups], axis=0)
    t_all = past_len + lax.broadcasted_iota(i32, (all_rows, 1), 0) % s_len

    @pl.when(j == 0)
    def _():
        n_all = kcall_ref.shape[1]
        gather = (lax.broadcasted_iota(i32, (npb, n_all), 1) == pid_ref[0]).astype(bf16)
        zeros = jnp.zeros((rows, HEAD_DIM), bf16)
        for g in groups:
            kc = _unpage(_dot(gather, kcall_ref[g])).astype(bf16)
            vc = _unpage(_dot(gather, vcall_ref[g])).astype(bf16)
            q4 = q_of(g)
            o_cmp, imp = _cmp_branch(q4, kc, vc, slope_ref[g], t_col, s_len, jrow[:, :n_c])
            imp = jnp.concatenate([imp, jnp.zeros((s_len, lanes - n_c), f32)], axis=1)
            sel_s[g * s_len:(g + 1) * s_len] = _select_blocks(imp, tq_col, n_c + 1, jrow)
            ocmp_s[g] = o_cmp
            qbd_s[g * rows:(g + 1) * rows] = jnp.concatenate([q4 if gg == g else zeros for gg in groups], axis=1)
        m_s[...] = jnp.full((all_rows, 1), NEG, f32)
        l_s[...] = jnp.zeros((all_rows, 1), f32)
        acc_s[...] = jnp.zeros(acc_s.shape, f32)

    def stack(keys):
        kt = jnp.concatenate([keys[g * GRP:g * GRP + HEAD_DIM] for g in groups], axis=0).astype(bf16)
        vt = jnp.concatenate([keys[g * GRP + HEAD_DIM:(g + 1) * GRP] for g in groups], axis=0).astype(bf16)
        return kt, vt

    def attend(keys, pos):
        onehot = (jcol == pos // CMP_BLOCK).astype(bf16)
        msk = _dot(sel_s[...].astype(bf16), onehot)
        msk = jnp.concatenate([msk[g * s_len:(g + 1) * s_len] for g in groups for _ in range(HPG)], axis=0)
        valid = (msk > 0.5) & (pos <= t_all)
        kt, vt = stack(keys)
        m_new, l_new, acc = _flash_update((m_s[...], l_s[...], acc_s[...]), qbd_s[...], kt, vt, slope_all,
                                          valid, (pos - past_len).astype(f32))
        m_s[...] = m_new
        l_s[...] = l_new
        acc_s[...] = acc

    nk = pp * PAGE_SIZE
    keys = jnp.concatenate([r[0] for r in page_refs], axis=1)
    attend(keys, j * nk + lax.broadcasted_iota(i32, (1, nk), 1))

    @pl.when(j == n_groups - 1)
    def _():
        new_keys = newsel_ref[0]
        attend(new_keys, past_len + lax.broadcasted_iota(i32, (1, new_keys.shape[1]), 1))
        o_sel = acc_s[...] / l_s[...]
        carry = (jnp.full((all_rows, 1), NEG, f32), jnp.zeros((all_rows, 1), f32),
                 jnp.zeros((all_rows, N_KV * HEAD_DIM), f32))
        for wkeys, wbase in ((swin_ref[0], past_len - swin_ref.shape[2]), (nwin_ref[0], past_len)):
            wpos = wbase + lax.broadcasted_iota(i32, (1, wkeys.shape[1]), 1)
            dpos = t_all - wpos
            valid = (dpos >= 0) & (dpos <= WINDOW)
            kt, vt = stack(wkeys)
            carry = _flash_update(carry, qbd_s[...], kt, vt, slope_all, valid, (wpos - past_len).astype(f32))
        o_win = carry[2] / carry[1]
        own = lambda o, g: o[g * rows:(g + 1) * rows, g * HEAD_DIM:(g + 1) * HEAD_DIM]
        gl = gl_ref[0]
        o_ref[0] = jnp.concatenate(
            [_gate_mix(gl[:, g * 128:(g + 1) * 128], ocmp_s[g], own(o_sel, g), own(o_win, g), s_len)
             for g in groups], axis=1)


def _nsa_sample(page_table, q, gl, kc_all, vc_all, pids, new_sel, state_win, new_win, slopes, cache_sel, pp):
    bsz, s_len, hd = q.shape
    n_pages = page_table.shape[1]
    past_len = n_pages * PAGE_SIZE
    pp = min(pp, n_pages)
    n_groups = n_pages // pp
    n_c = past_len // CMP_BLOCK
    lanes = -(-(n_c + 2) // 128) * 128
    rows = HPG * s_len
    kvw = cache_sel.shape[1]
    perb = lambda b, j, pt: (b, 0, 0)

    def page_spec(i):
        return pl.BlockSpec((1, kvw, PAGE_SIZE), lambda b, j, pt: (pt[b, j * pp + i], 0, 0))

    kern = functools.partial(_nsa_sample_kernel, s_len=s_len, past_len=past_len, pp=pp,
                             n_groups=n_groups, lanes=lanes)
    grid_spec = pltpu.PrefetchScalarGridSpec(
        num_scalar_prefetch=1,
        grid=(bsz, n_groups),
        in_specs=[
            pl.BlockSpec((1, s_len, hd), perb),
            pl.BlockSpec((1, s_len, gl.shape[-1]), perb),
            pl.BlockSpec(kc_all.shape, lambda b, j, pt: (0, 0, 0)),
            pl.BlockSpec(vc_all.shape, lambda b, j, pt: (0, 0, 0)),
            pl.BlockSpec((1, n_pages, 1), perb),
            pl.BlockSpec((1,) + new_sel.shape[1:], perb),
            pl.BlockSpec((1,) + state_win.shape[1:], perb),
            pl.BlockSpec((1,) + new_win.shape[1:], perb),
            pl.BlockSpec(slopes.shape, lambda b, j, pt: (0, 0, 0)),
        ] + [page_spec(i) for i in range(pp)],
        out_specs=pl.BlockSpec((1, s_len, hd), perb),
        scratch_shapes=[
            pltpu.VMEM((N_KV * s_len, lanes), f32),
            pltpu.VMEM((N_KV, rows, HEAD_DIM), f32),
            pltpu.VMEM((N_KV * rows, N_KV * HEAD_DIM), bf16),
            pltpu.VMEM((N_KV * rows, 1), f32),
            pltpu.VMEM((N_KV * rows, 1), f32),
            pltpu.VMEM((N_KV * rows, N_KV * HEAD_DIM), f32),
        ],
    )
    return pl.pallas_call(
        kern,
        grid_spec=grid_spec,
        out_shape=jax.ShapeDtypeStruct((bsz, s_len, hd), f32),
        compiler_params=_params(("arbitrary", "arbitrary")),
        name="nsa_sample_attn",
    )(page_table, q, gl, kc_all, vc_all, pids, new_sel, state_win, new_win, slopes, *([cache_sel] * pp))


def _out_kernel(o_ref, z_ref, x_ref, mod_ref, w_ref, lng_ref, lnb_ref, xo_ref, *, nb, tr, d):
    y = _dot((o_ref[...] * _silu(z_ref[...])).astype(bf16), w_ref[...]).reshape(nb, tr, d)
    gate = mod_ref[...][:, 2:3, :]
    xo_ref[...] = _layer_norm(DN_ALPHA * x_ref[...] + (1.0 + gate) * y, lng_ref[...], lnb_ref[...])


def _out_proj(o, z, x, mod, w_out, ln_g, ln_b, nb, tr):
    bsz, t, d = x.shape
    hd = o.shape[-1]
    nt = t // tr
    const2 = lambda b, i: (0, 0)
    kern = functools.partial(_out_kernel, nb=nb, tr=tr, d=d)
    return pl.pallas_call(
        kern,
        grid=(bsz // nb, nt),
        in_specs=[
            pl.BlockSpec((nb * tr, hd), lambda b, i: (b * nt + i, 0)),
            pl.BlockSpec((nb * tr, hd), lambda b, i: (b * nt + i, 0)),
            pl.BlockSpec((nb, tr, d), lambda b, i: (b, i, 0)),
            pl.BlockSpec((nb, 3, d), lambda b, i: (b, 0, 0)),
            pl.BlockSpec(w_out.shape, const2),
            pl.BlockSpec((1, d), const2),
            pl.BlockSpec((1, d), const2),
        ],
        out_specs=pl.BlockSpec((nb, tr, d), lambda b, i: (b, i, 0)),
        out_shape=jax.ShapeDtypeStruct((bsz, t, d), f32),
        compiler_params=_params(("arbitrary", "arbitrary")),
        name="nsa_out",
    )(o, z, x, mod, w_out, ln_g, ln_b)


def _head_slopes(nq):
    s = 2.0 ** (-8.0 * np.arange(1, N_HEADS + 1) / N_HEADS)
    s = np.repeat(s.reshape(N_KV, HPG), nq, axis=1)
    return jnp.asarray(s.reshape(N_KV, HPG * nq, 1), f32)


def kernel(x_prompt, x_sample, c_prompt, c_sample, state_h, state_conv, cache_cmp, cache_sel, state_win, page_table, w_ada, b_ada, ln_g, ln_b, w_in_a, conv_w_a, conv_b_a, w_r_a, b_r_a, w_i_a, b_i_a, lam_a, w_out_a, w_kv, phi_pe, w_phi1, b_phi1, w_phi2, b_phi2, w_in_b, b_gate_b, w_out_b):
    bp, t_len, d = x_prompt.shape
    bs, s_len, _ = x_sample.shape
    n_pages = page_table.shape[1]
    past_len = n_pages * PAGE_SIZE
    hd = N_HEADS * HEAD_DIM
    kvw = N_KV * GRP
    assert s_len <= CMP_BLOCK and t_len % 128 == 0 and w_ada.shape[0] == DEPTH == 2

    n_c = bp + bs
    pad = -n_c % 8
    c_all = jnp.concatenate([c_prompt, c_sample, jnp.zeros((pad, d), f32)], axis=0)
    mods = _ada_mod(c_all, w_ada, b_ada).reshape(DEPTH, n_c + pad, 3, d)
    mod_p = mods[:, :bp]
    mod_s = mods[:, bp:n_c]

    rg = d // N_RG_BLOCKS
    w_in0 = w_in_a[0].astype(bf16)
    w_gate = jnp.concatenate([w_r_a[0], w_i_a[0]], axis=-1).astype(bf16)
    w_out0 = w_out_a[0].astype(bf16)
    row = lambda v: v.reshape(1, -1)
    rg_args = (w_in0, conv_w_a[0], row(conv_b_a[0]), w_gate, row(b_r_a[0]), row(b_i_a[0]), row(lam_a[0]),
               w_out0, row(ln_g[0]), row(ln_b[0]))
    xp1, h_p, conv_p = _rglru_prompt(x_prompt, mod_p[0], jnp.zeros((bp, 1, d), f32),
                                     jnp.zeros((bp, CONV_W - 1, d), f32), *rg_args, tc=256)
    xs1_tm, h_s, conv_s_tm = _rglru_sample(
        x_sample.transpose(1, 0, 2), mod_s[0].transpose(1, 0, 2), state_h[0],
        state_conv[0].transpose(1, 0, 2), *rg_args)
    xs1 = xs1_tm.transpose(1, 0, 2)
    new_h_p = h_p.reshape(1, bp, d)
    new_conv_p = conv_p.reshape(1, bp, CONV_W - 1, d)
    new_h_s = h_s.reshape(1, bs, d)
    new_conv_s = conv_s_tm.transpose(1, 0, 2).reshape(1, bs, CONV_W - 1, d)

    w_b = w_in_b[0]
    w_q = w_b[:, :hd].astype(bf16)
    w_z = w_b[:, hd:2 * hd].astype(bf16)
    gpg = HPG * 3
    w_g = jnp.pad(w_b[:, 2 * hd:].reshape(d, N_KV, gpg), ((0, 0), (0, 0), (0, 128 - gpg)))
    w_g = w_g.reshape(d, N_KV * 128).astype(bf16)
    b_g = jnp.pad(b_gate_b[0].reshape(N_KV, gpg), ((0, 0), (0, 128 - gpg))).reshape(1, N_KV * 128)
    tmin = lambda v: v.transpose(0, 2, 3, 4, 1)
    tq = min(128, t_len)
    tk = min(512, t_len)
    consts = _prompt_consts(t_len, tq, tk, PAGE_SIZE)
    cmpt_p, selt_p, wint_p, cmp_pages, kaug_p, vaug_p, wint_tiles, q_p, z_p, gl_p = _proj_prompt(
        xp1, mod_p[1], w_kv.T.astype(bf16), w_q, w_z, w_g, b_g, consts["e"], consts["prow"], consts["ones"],
        tr=tk)
    cmp_s, sel_s, win_s, selb_s, winb_s, q_s, z_s, gl_s = _proj(
        xs1, mod_s[1], w_kv.astype(bf16), w_q, w_z, w_g, b_g, nb=bs, tr=s_len)
    per_p = lambda v: v.reshape(bp, t_len, v.shape[-1])
    per_s = lambda v: v.reshape(bs, s_len, v.shape[-1])

    bpp = PAGE_SIZE // CMP_BLOCK
    dphi = w_phi2.shape[1]
    diag2 = lambda w: jnp.concatenate([jnp.concatenate([w, jnp.zeros_like(w)], axis=-1),
                                       jnp.concatenate([jnp.zeros_like(w), w], axis=-1)], axis=-2)
    pet = jnp.tile(phi_pe.transpose(1, 2, 0), (1, 1, bpp))
    w1 = diag2(w_phi1.transpose(0, 2, 1, 3)).astype(bf16)
    w1 = w1.reshape(2, HEAD_DIM // 2, 2 * PAGE_SIZE, bpp * dphi)
    b1 = jnp.tile(b_phi1, (1, bpp)).reshape(2, 1, bpp * dphi)
    w2 = diag2(w_phi2).astype(bf16)
    b2 = jnp.tile(b_phi2, (1, bpp)).reshape(2, 1, bpp * HEAD_DIM)
    kc_p, vc_p = _compress(cmp_pages.reshape(-1, PAGE_SIZE), pet, w1, b1, w2, b2, npg=64, out_dtype=f32)
    kc_all, vc_all = _compress(tmin(cache_cmp).reshape(-1, PAGE_SIZE), pet, w1, b1, w2, b2,
                               npg=64, out_dtype=bf16)

    o_p = _nsa_prompt(per_p(q_p), per_p(gl_p), kc_p, vc_p, kaug_p, vaug_p, wint_tiles, consts, tq=tq)
    tpad = lambda v: jnp.pad(per_s(v), ((0, 0), (0, PAGE_SIZE - s_len), (0, 0))).transpose(0, 2, 1)
    o_s = _nsa_sample(page_table, per_s(q_s), per_s(gl_s), kc_all, vc_all, page_table.reshape(bs, n_pages, 1),
                      tpad(selb_s), tmin(state_win).reshape(bs, kvw, -1), tpad(winb_s), _head_slopes(s_len),
                      tmin(cache_sel).reshape(-1, kvw, PAGE_SIZE), pp=8)

    w_out1 = w_out_b[0].astype(bf16)
    y_p = _out_proj(o_p.reshape(bp * t_len, hd), z_p, xp1, mod_p[1], w_out1, row(ln_g[1]), row(ln_b[1]),
                    nb=1, tr=min(512, t_len))
    y_s = _out_proj(o_s.reshape(bs * s_len, hd), z_s, xs1, mod_s[1], w_out1, row(ln_g[1]), row(ln_b[1]),
                    nb=bs, tr=s_len)

    kv5 = lambda v: v.reshape(bs, s_len, N_KV, 2, HEAD_DIM)
    kv5t = lambda v: v.reshape(bp, N_KV, 2, HEAD_DIM, -1).transpose(0, 4, 1, 2, 3)
    wb = state_win.shape[1]
    new_win_p = kv5t(wint_p[:, :, -min(WINDOW, t_len):])
    new_win_s = jnp.concatenate([state_win, kv5(win_s)], axis=1)[:, -wb:]
    return (y_p, y_s, kv5t(cmpt_p), kv5t(selt_p), new_win_p, new_h_p, new_conv_p,
            kv5(cmp_s), kv5(sel_s), new_win_s, new_h_s, new_conv_s)
```

```python
import functools
import math

import numpy as np
import jax
import jax.numpy as jnp
from jax import lax
from jax.experimental import pallas as pl
from jax.experimental.pallas import tpu as pltpu

f32 = jnp.float32
bf16 = jnp.bfloat16
i32 = jnp.int32

DEPTH = 2
N_RG_BLOCKS = 8
CONV_W = 4
RG_C = 8.0
N_HEADS = 16
HEAD_DIM = 64
N_KV = 4
HPG = N_HEADS // N_KV
CMP_BLOCK = 64
N_SEL = 16
WINDOW = 512
PAGE_SIZE = 128
DN_ALPHA = (2.0 * DEPTH) ** 0.25
LN_EPS = 1e-5
NEG = -1e30
FORCED = 1e6
REMOVED = -3e38
GRP = 2 * HEAD_DIM
QGRP = HPG * HEAD_DIM
VMEM_LIMIT_BYTES = 56 * 1024 * 1024


def _params(sem):
    return pltpu.CompilerParams(dimension_semantics=sem, vmem_limit_bytes=VMEM_LIMIT_BYTES)


def _dot(a, b):
    return jnp.dot(a, b, preferred_element_type=f32)


def _dot_nt(a, b):
    return lax.dot_general(a, b, (((1,), (1,)), ((), ())), preferred_element_type=f32)


def _sigmoid(x):
    return 1.0 / (1.0 + jnp.exp(-x))


def _silu(x):
    return x * _sigmoid(x)


def _log1p(e):
    u = 1.0 + e
    dlt = u - 1.0
    return jnp.where(dlt == 0.0, e, jnp.log(u) * (e / jnp.where(dlt == 0.0, 1.0, dlt)))


def _layer_norm(x, g, b):
    mu = jnp.mean(x, axis=-1, keepdims=True)
    xc = x - mu
    var = jnp.mean(xc * xc, axis=-1, keepdims=True)
    return xc * lax.rsqrt(var + LN_EPS) * g + b


def _ada_kernel(c_ref, w_ref, b_ref, o_ref):
    a = _silu(c_ref[...])
    o_ref[0] = jnp.dot(a, w_ref[0], preferred_element_type=f32,
                       precision=lax.Precision.HIGHEST) + b_ref[0]


def _ada_mod(c_all, w_ada, b_ada):
    rows, d = c_all.shape
    depth = w_ada.shape[0]
    return pl.pallas_call(
        _ada_kernel,
        grid=(depth, 3),
        in_specs=[
            pl.BlockSpec((rows, d), lambda l, n: (0, 0)),
            pl.BlockSpec((1, d, d), lambda l, n: (l, 0, n)),
            pl.BlockSpec((1, 1, d), lambda l, n: (l, 0, n)),
        ],
        out_specs=pl.BlockSpec((1, rows, d), lambda l, n: (l, 0, n)),
        out_shape=jax.ShapeDtypeStruct((depth, rows, 3 * d), f32),
        compiler_params=_params(("arbitrary", "arbitrary")),
        name="ada_mod",
    )(c_all, w_ada, b_ada.reshape(depth, 1, 3 * d))


def _rglru_gates(xc, wg_ref, br, bi, lam):
    xcb = xc.astype(bf16)
    rg = xc.shape[1] // N_RG_BLOCKS
    rs, is_ = [], []
    for n in range(N_RG_BLOCKS):
        g = _dot(xcb[:, n * rg:(n + 1) * rg], wg_ref[n])
        rs.append(g[:, :rg])
        is_.append(g[:, rg:])
    r = _sigmoid(jnp.concatenate(rs, axis=1) + br)
    i = _sigmoid(jnp.concatenate(is_, axis=1) + bi)
    nl = -lam
    softplus = jnp.maximum(nl, 0.0) + _log1p(jnp.exp(-jnp.abs(nl)))
    log_a = (-RG_C * softplus) * r
    a = jnp.exp(log_a)
    gain = jnp.sqrt(jnp.maximum(-jnp.tanh(log_a) * (a * a + 1.0), 0.0))
    b = gain * i * xc
    return a, b


def _rglru_prompt_kernel(x_ref, mod_ref, h0_ref, c0_ref, win_ref, cw_ref, cb_ref, wg_ref, br_ref,
                         bi_ref, lam_ref, wout_ref, lng_ref, lnb_ref,
                         xo_ref, hl_ref, cl_ref, xbuf, a_s, b_s, hs_s, hc, *, tc, d):
    t = pl.program_id(1)

    @pl.when(t == 0)
    def _():
        xbuf[0:8, :] = jnp.zeros((8, d), f32)
        xbuf[8 - (CONV_W - 1):8, :] = c0_ref[0]
        hc[...] = jnp.broadcast_to(h0_ref[0], (8, d))

    x = x_ref[0]
    mod = mod_ref[0]
    shift, scale, gate = mod[0:1], mod[1:2], mod[2:3]
    m = x * (1.0 + scale) + shift
    u = _dot(m.astype(bf16), win_ref[...])
    xb = u[:, :d]
    zg = u[:, d:]
    xbuf[8:8 + tc, :] = xb
    base = 8 - (CONV_W - 1)
    xc = cb_ref[...] + xbuf[base:base + tc, :] * cw_ref[0:1, :]
    for k in range(1, CONV_W):
        xc = xc + xbuf[base + k:base + k + tc, :] * cw_ref[k:k + 1, :]
    tail = xbuf[8 + tc - (CONV_W - 1):8 + tc, :]
    xbuf[base:8, :] = tail
    cl_ref[0] = tail

    a, b = _rglru_gates(xc, wg_ref, br_ref[...], bi_ref[...], lam_ref[...])

    rowi = lax.broadcasted_iota(i32, (tc, 1), 0) % 8
    for s in (1, 2, 4):
        ok = rowi >= s
        a_sh = pltpu.roll(a, s, 0)
        b_sh = pltpu.roll(b, s, 0)
        b = jnp.where(ok, a * b_sh + b, b)
        a = jnp.where(ok, a * a_sh, a)
    a_s[...] = a
    b_s[...] = b

    def tile_step(j, hprev):
        r0 = pl.multiple_of(j * 8, 8)
        ht = a_s[pl.ds(r0, 8), :] * hprev + b_s[pl.ds(r0, 8), :]
        hs_s[pl.ds(r0, 8), :] = ht
        return jnp.broadcast_to(ht[7:8, :], (8, d))

    hlast = lax.fori_loop(0, tc // 8, tile_step, hc[...])
    hc[...] = hlast
    hl_ref[0] = hlast[0:1, :]

    y = _dot((hs_s[...] * _silu(zg)).astype(bf16), wout_ref[...])
    xo_ref[0] = _layer_norm(DN_ALPHA * x + (1.0 + gate) * y, lng_ref[...], lnb_ref[...])


def _rglru_prompt(x, mod, h0, c0, w_in, conv_w, conv_b, w_gate, b_r, b_i, lam, w_out, ln_g, ln_b, tc):
    bsz, t, d = x.shape
    tc = min(tc, t)
    const2 = lambda b, i: (0, 0)
    const3 = lambda b, i: (0, 0, 0)
    perb = lambda b, i: (b, 0, 0)
    kern = functools.partial(_rglru_prompt_kernel, tc=tc, d=d)
    return pl.pallas_call(
        kern,
        grid=(bsz, t // tc),
        in_specs=[
            pl.BlockSpec((1, tc, d), lambda b, i: (b, i, 0)),
            pl.BlockSpec((1, 3, d), perb),
            pl.BlockSpec((1, 1, d), perb),
            pl.BlockSpec((1, CONV_W - 1, d), perb),
            pl.BlockSpec(w_in.shape, const2),
            pl.BlockSpec(conv_w.shape, const2),
            pl.BlockSpec((1, d), const2),
            pl.BlockSpec(w_gate.shape, const3),
            pl.BlockSpec((1, d), const2),
            pl.BlockSpec((1, d), const2),
            pl.BlockSpec((1, d), const2),
            pl.BlockSpec(w_out.shape, const2),
            pl.BlockSpec((1, d), const2),
            pl.BlockSpec((1, d), const2),
        ],
        out_specs=[
            pl.BlockSpec((1, tc, d), lambda b, i: (b, i, 0)),
            pl.BlockSpec((1, 1, d), perb),
            pl.BlockSpec((1, CONV_W - 1, d), perb),
        ],
        out_shape=[
            jax.ShapeDtypeStruct((bsz, t, d), f32),
            jax.ShapeDtypeStruct((bsz, 1, d), f32),
            jax.ShapeDtypeStruct((bsz, CONV_W - 1, d), f32),
        ],
        scratch_shapes=[
            pltpu.VMEM((tc + 8, d), f32),
            pltpu.VMEM((tc, d), f32),
            pltpu.VMEM((tc, d), f32),
            pltpu.VMEM((tc, d), f32),
            pltpu.VMEM((8, d), f32),
        ],
        compiler_params=_params(("arbitrary", "arbitrary")),
        name="rglru_prompt",
    )(x, mod, h0, c0, w_in, conv_w, conv_b, w_gate, b_r, b_i, lam, w_out, ln_g, ln_b)


def _rglru_sample_kernel(x_ref, mod_ref, h0_ref, c0_ref, win_ref, cw_ref, cb_ref, wg_ref, br_ref,
                         bi_ref, lam_ref, wout_ref, lng_ref, lnb_ref,
                         xo_ref, hl_ref, cl_ref, *, s_len, bsz, d):
    x = x_ref[...]
    mod = mod_ref[...]
    shift, scale, gate = mod[0:1], mod[1:2], mod[2:3]
    m = x * (1.0 + scale) + shift
    u = _dot(m.reshape(s_len * bsz, d).astype(bf16), win_ref[...])
    xb = u[:, :d].reshape(s_len, bsz, d)
    zg = u[:, d:]
    xp = jnp.concatenate([c0_ref[...], xb], axis=0)
    cw = cw_ref[...]
    xc = cb_ref[...] + xp[0:s_len] * cw[0:1]
    for k in range(1, CONV_W):
        xc = xc + xp[k:k + s_len] * cw[k:k + 1]
    cl_ref[...] = xp[s_len:s_len + CONV_W - 1]
    a, b = _rglru_gates(xc.reshape(s_len * bsz, d), wg_ref, br_ref[...], bi_ref[...], lam_ref[...])
    h = h0_ref[...]
    hs = []
    for s in range(s_len):
        h = a[s * bsz:(s + 1) * bsz] * h + b[s * bsz:(s + 1) * bsz]
        hs.append(h)
    hl_ref[...] = h
    hs = jnp.concatenate(hs, axis=0)
    y = _dot((hs * _silu(zg)).astype(bf16), wout_ref[...])
    xo = _layer_norm(DN_ALPHA * x + (1.0 + gate) * y.reshape(s_len, bsz, d), lng_ref[...], lnb_ref[...])
    xo_ref[...] = xo


def _rglru_sample(x_tm, mod_tm, h0, c0_tm, w_in, conv_w, conv_b, w_gate, b_r, b_i, lam, w_out, ln_g, ln_b):
    s_len, bsz, d = x_tm.shape
    kern = functools.partial(_rglru_sample_kernel, s_len=s_len, bsz=bsz, d=d)
    return pl.pallas_call(
        kern,
        out_shape=[
            jax.ShapeDtypeStruct((s_len, bsz, d), f32),
            jax.ShapeDtypeStruct((bsz, d), f32),
            jax.ShapeDtypeStruct((CONV_W - 1, bsz, d), f32),
        ],
        compiler_params=pltpu.CompilerParams(vmem_limit_bytes=VMEM_LIMIT_BYTES),
        name="rglru_sample",
    )(x_tm, mod_tm, h0, c0_tm, w_in, conv_w, conv_b, w_gate, b_r, b_i, lam, w_out, ln_g, ln_b)


def _proj_kernel(x_ref, mod_ref, wkv_ref, wq_ref, wz_ref, wg_ref, bg_ref,
                 cmp_ref, sel_ref, win_ref, selb_ref, winb_ref, q_ref, z_ref, gl_ref, *, nb, tr, d):
    x = x_ref[...]
    mod = mod_ref[...]
    m = x * (1.0 + mod[:, 1:2, :]) + mod[:, 0:1, :]
    xb = x.reshape(nb * tr, d).astype(bf16)
    mb = m.reshape(nb * tr, d).astype(bf16)
    kv = _dot(xb, wkv_ref[...])
    w = kv.shape[1] // 3
    sel = kv[:, w:2 * w]
    win = kv[:, 2 * w:]
    cmp_ref[...] = kv[:, :w]
    sel_ref[...] = sel
    win_ref[...] = win
    selb_ref[...] = sel.astype(bf16)
    winb_ref[...] = win.astype(bf16)
    q_ref[...] = (_dot(mb, wq_ref[...]) * (HEAD_DIM ** -0.5)).astype(bf16)
    z_ref[...] = _dot(mb, wz_ref[...])
    gl_ref[...] = _dot(mb, wg_ref[...]) + bg_ref[...]


def _proj(x, mod, w_kv, w_q, w_z, w_g, b_g, nb, tr):
    bsz, t, d = x.shape
    kvw = w_kv.shape[1] // 3
    hd = w_q.shape[1]
    gw = w_g.shape[1]
    nt = t // tr
    const2 = lambda b, i: (0, 0)
    blk = lambda width: pl.BlockSpec((nb * tr, width), lambda b, i: (b * nt + i, 0))
    out = lambda width, dt: jax.ShapeDtypeStruct((bsz * t, width), dt)
    kern = functools.partial(_proj_kernel, nb=nb, tr=tr, d=d)
    return pl.pallas_call(
        kern,
        grid=(bsz // nb, nt),
        in_specs=[
            pl.BlockSpec((nb, tr, d), lambda b, i: (b, i, 0)),
            pl.BlockSpec((nb, 3, d), lambda b, i: (b, 0, 0)),
            pl.BlockSpec(w_kv.shape, const2),
            pl.BlockSpec(w_q.shape, const2),
            pl.BlockSpec(w_z.shape, const2),
            pl.BlockSpec(w_g.shape, const2),
            pl.BlockSpec((1, gw), const2),
        ],
        out_specs=[blk(kvw), blk(kvw), blk(kvw), blk(kvw), blk(kvw), blk(hd), blk(hd), blk(gw)],
        out_shape=[out(kvw, f32), out(kvw, f32), out(kvw, f32), out(kvw, bf16), out(kvw, bf16),
                   out(hd, bf16), out(hd, f32), out(gw, f32)],
        compiler_params=_params(("arbitrary", "arbitrary")),
        name="nsa_proj",
    )(x, mod, w_kv, w_q, w_z, w_g, b_g)


def _proj_prompt_kernel(x_ref, mod_ref, wkvt_ref, wq_ref, wz_ref, wg_ref, bg_ref, e_ref, prow_ref, ones_ref,
                        cmpt_ref, selt_ref, wint_ref, cmppg_ref, kaug_ref, vaug_ref, wintt_ref, q_ref, z_ref,
                        gl_ref, *, tr):
    x = x_ref[0]
    mod = mod_ref[0]
    m = x * (1.0 + mod[1:2]) + mod[0:1]
    mb = m.astype(bf16)
    kvt = _dot_nt(wkvt_ref[...], x.astype(bf16))
    kvw = kvt.shape[0] // 3
    cmpt = kvt[:kvw]
    selt = kvt[kvw:2 * kvw]
    wint = kvt[2 * kvw:]
    cmpt_ref[0] = cmpt
    selt_ref[0] = selt
    wint_ref[0] = wint
    selb = selt.astype(bf16)
    n_c = e_ref.shape[1]
    for g in range(N_KV):
        kaug_ref[0, g, 0, 0:n_c] = e_ref[0]
        kaug_ref[0, g, 0, n_c:n_c + HEAD_DIM] = selb[g * GRP:g * GRP + HEAD_DIM]
        kaug_ref[0, g, 0, n_c + HEAD_DIM:] = prow_ref[...]
        vaug_ref[0, g, 0, 0:HEAD_DIM] = selb[g * GRP + HEAD_DIM:(g + 1) * GRP]
        vaug_ref[0, g, 0, HEAD_DIM:] = ones_ref[...]
    winb = wint.astype(bf16)
    for k in range(tr // PAGE_SIZE):
        cmppg_ref[k] = cmpt[:, k * PAGE_SIZE:(k + 1) * PAGE_SIZE]
        wintt_ref[0, k] = winb[:, k * PAGE_SIZE:(k + 1) * PAGE_SIZE]
    q_ref[...] = (_dot(mb, wq_ref[...]) * (HEAD_DIM ** -0.5)).astype(bf16)
    z_ref[...] = _dot(mb, wz_ref[...])
    gl_ref[...] = _dot(mb, wg_ref[...]) + bg_ref[...]


def _proj_prompt(x, mod, w_kvt, w_q, w_z, w_g, b_g, e, prow, ones, tr):
    bsz, t, d = x.shape
    kvw = w_kvt.shape[0] // 3
    hd = w_q.shape[1]
    gw = w_g.shape[1]
    nt = t // tr
    ppt = tr // PAGE_SIZE
    ka_rows = e.shape[1] + HEAD_DIM + prow.shape[0]
    va_rows = HEAD_DIM + ones.shape[0]
    const2 = lambda b, i: (0, 0)
    rows = lambda width: pl.BlockSpec((tr, width), lambda b, i: (b * nt + i, 0))
    tmin = pl.BlockSpec((1, kvw, tr), lambda b, i: (b, 0, i))
    kern = functools.partial(_proj_prompt_kernel, tr=tr)
    return pl.pallas_call(
        kern,
        grid=(bsz, nt),
        in_specs=[
            pl.BlockSpec((1, tr, d), lambda b, i: (b, i, 0)),
            pl.BlockSpec((1, 3, d), lambda b, i: (b, 0, 0)),
            pl.BlockSpec(w_kvt.shape, const2),
            pl.BlockSpec(w_q.shape, const2),
            pl.BlockSpec(w_z.shape, const2),
            pl.BlockSpec(w_g.shape, const2),
            pl.BlockSpec((1, gw), const2),
            pl.BlockSpec((1,) + e.shape[1:], lambda b, i: (i, 0, 0)),
            pl.BlockSpec(prow.shape, const2),
            pl.BlockSpec(ones.shape, const2),
        ],
        out_specs=[
            tmin, tmin, tmin,
            pl.BlockSpec((ppt, kvw, PAGE_SIZE), lambda b, i: (b * nt + i, 0, 0)),
            pl.BlockSpec((1, N_KV, 1, ka_rows, tr), lambda b, i: (b, 0, i, 0, 0)),
            pl.BlockSpec((1, N_KV, 1, va_rows, tr), lambda b, i: (b, 0, i, 0, 0)),
            pl.BlockSpec((1, ppt, kvw, PAGE_SIZE), lambda b, i: (b, i, 0, 0)),
            rows(hd), rows(hd), rows(gw),
        ],
        out_shape=[
            jax.ShapeDtypeStruct((bsz, kvw, t), f32),
            jax.ShapeDtypeStruct((bsz, kvw, t), f32),
            jax.ShapeDtypeStruct((bsz, kvw, t), f32),
            jax.ShapeDtypeStruct((bsz * t // PAGE_SIZE, kvw, PAGE_SIZE), f32),
            jax.ShapeDtypeStruct((bsz, N_KV, nt, ka_rows, tr), bf16),
            jax.ShapeDtypeStruct((bsz, N_KV, nt, va_rows, tr), bf16),
            jax.ShapeDtypeStruct((bsz, t // PAGE_SIZE, kvw, PAGE_SIZE), bf16),
            jax.ShapeDtypeStruct((bsz * t, hd), bf16),
            jax.ShapeDtypeStruct((bsz * t, hd), f32),
            jax.ShapeDtypeStruct((bsz * t, gw), f32),
        ],
        compiler_params=_params(("arbitrary", "arbitrary")),
        name="nsa_proj_prompt",
    )(x, mod, w_kvt, w_q, w_z, w_g, b_g, e, prow, ones)


def _compress_kernel(x_ref, pet_ref, w1_ref, b1_ref, w2_ref, b2_ref, kc_ref, vc_ref, *, npg):
    page_rows = N_KV * GRP

    def dim_rows(c, dd):
        return jnp.concatenate(
            [(x_ref[pl.ds((g * 2 + c) * HEAD_DIM + dd, npg, stride=page_rows), :]
              + pet_ref[c, dd:dd + 1, :]).astype(bf16) for g in range(N_KV)], axis=0)

    outs = []
    for c in range(2):
        acc = None
        for dp in range(HEAD_DIM // 2):
            lhs = jnp.concatenate([dim_rows(c, 2 * dp), dim_rows(c, 2 * dp + 1)], axis=1)
            part = _dot(lhs, w1_ref[c, dp])
            acc = part if acc is None else acc + part
        hid = _silu(acc + b1_ref[c])
        outs.append(_dot(hid.astype(bf16), w2_ref[c]) + b2_ref[c])
    for g in range(N_KV):
        kc_ref[g] = outs[0][g * npg:(g + 1) * npg, :].astype(kc_ref.dtype)
        vc_ref[g] = outs[1][g * npg:(g + 1) * npg, :].astype(vc_ref.dtype)


def _compress(x2d, pet, w1, b1, w2, b2, npg, out_dtype):
    rows, width = x2d.shape
    n_pages = rows // (N_KV * GRP)
    npg = min(npg, n_pages)
    while n_pages % npg:
        npg -= 8
    ow = (PAGE_SIZE // CMP_BLOCK) * HEAD_DIM
    const3 = lambda i: (0, 0, 0)
    kern = functools.partial(_compress_kernel, npg=npg)
    out = pl.BlockSpec((N_KV, npg, ow), lambda i: (0, i, 0))
    return pl.pallas_call(
        kern,
        grid=(n_pages // npg,),
        in_specs=[
            pl.BlockSpec((npg * N_KV * GRP, width), lambda i: (i, 0)),
            pl.BlockSpec(pet.shape, const3),
            pl.BlockSpec(w1.shape, lambda i: (0, 0, 0, 0), pipeline_mode=pl.Buffered(1)),
            pl.BlockSpec(b1.shape, const3),
            pl.BlockSpec(w2.shape, const3),
            pl.BlockSpec(b2.shape, const3),
        ],
        out_specs=[out, out],
        out_shape=[jax.ShapeDtypeStruct((N_KV, n_pages, ow), out_dtype)] * 2,
        compiler_params=_params(("arbitrary",)),
        name="nsa_compress",
    )(x2d, pet, w1, b1, w2, b2)


def _block_ids(n_pages, lanes, axis):
    shape = (1, lanes) if axis == 1 else (lanes, 1)
    r = lax.broadcasted_iota(i32, shape, axis)
    bpp = PAGE_SIZE // CMP_BLOCK
    perm = jnp.where(r < n_pages, bpp * r, bpp * (r - n_pages) + 1)
    return jnp.where(r < bpp * n_pages, perm, r)


def _unpage(x):
    return jnp.concatenate([x[:, :HEAD_DIM], x[:, HEAD_DIM:]], axis=0)


def _cmp_branch(q4, kc, vc, slope, t_col, nq, jrow):
    s = _dot_nt(q4, kc)
    c_end = (jrow + 1) * CMP_BLOCK - 1
    dist = t_col.astype(f32) - c_end.astype(f32)
    s = s - slope * dist
    mask = c_end <= t_col
    s = jnp.where(mask, s, NEG)
    e = jnp.exp(s - jnp.max(s, axis=-1, keepdims=True))
    p = e / jnp.sum(e, axis=-1, keepdims=True)
    p = jnp.where(mask, p, 0.0)
    o = _dot(p.astype(bf16), vc)
    imp = p[0:nq]
    for h in range(1, HPG):
        imp = imp + p[h * nq:(h + 1) * nq]
    return o, imp


def _select_blocks(imp, tq_col, n_sb, jr):
    nq, lanes = imp.shape
    cb = tq_col // CMP_BLOCK
    forced = (jr == 0) | (jr == cb) | (jr == cb - 1)
    causal = jr <= cb
    score = jnp.where(forced, FORCED, jnp.where(causal, imp, -1.0))
    score = jnp.where(jr < n_sb, score, REMOVED)
    jf = jr.astype(f32)
    sel = jnp.zeros((nq, lanes), f32)
    for _ in range(min(N_SEL, n_sb)):
        mx = jnp.max(score, axis=-1, keepdims=True)
        idx = jnp.min(jnp.where(score == mx, jf, 1e9), axis=-1, keepdims=True)
        hit = jf == idx
        sel = jnp.where(hit, 1.0, sel)
        score = jnp.where(hit, REMOVED, score)
    return sel


def _flash_update(carry, q4, kt, vt, slope, valid, pos_rel):
    m_i, l_i, acc = carry
    s = _dot(q4, kt) + slope * pos_rel
    s = jnp.where(valid, s, NEG)
    m_new = jnp.maximum(m_i, jnp.max(s, axis=-1, keepdims=True))
    alpha = jnp.exp(m_i - m_new)
    p = jnp.exp(s - m_new)
    l_new = alpha * l_i + jnp.sum(p, axis=-1, keepdims=True)
    acc = alpha * acc + _dot_nt(p.astype(bf16), vt)
    return m_new, l_new, acc


def _block_mask(selb, jcol, pos, reps):
    onehot = (jcol == pos // CMP_BLOCK).astype(bf16)
    msk = _dot(selb, onehot)
    return jnp.concatenate([msk] * reps, axis=0) > 0.5


def _gate_mix(gl, o_cmp, o_sel, o_win, nq):
    gs = _sigmoid(gl)
    outs = []
    for h in range(HPG):
        sl = slice(h * nq, (h + 1) * nq)
        outs.append(gs[:, 3 * h:3 * h + 1] * o_cmp[sl] + gs[:, 3 * h + 1:3 * h + 2] * o_sel[sl]
                    + gs[:, 3 * h + 2:3 * h + 3] * o_win[sl])
    return jnp.concatenate(outs, axis=1)


def _init_carry(rows):
    return (jnp.full((rows, 1), NEG, f32), jnp.zeros((rows, 1), f32), jnp.zeros((rows, HEAD_DIM), f32))


def _select_blocks_t(impt, tq_row, n_sb, jcol, fillers=()):
    n_c, nq = impt.shape
    cb = tq_row // CMP_BLOCK
    forced = (jcol == 0) | (jcol == cb) | (jcol == cb - 1)
    causal = jcol <= cb
    score = jnp.where(forced, REMOVED, jnp.where(causal, impt, -1.0))
    score = jnp.where(jcol < n_sb, score, REMOVED)
    jf = jcol.astype(f32)
    sel = jnp.where(forced & (jcol < n_sb), 1.0, 0.0) + jnp.zeros((n_c, nq), f32)
    rounds = max(min(N_SEL, n_sb) - 3, 0)
    fillers = list(fillers)
    every = max(rounds // max(len(fillers), 1), 1)
    for it in range(rounds):
        mx = jnp.max(score, axis=0, keepdims=True)
        idx = jnp.min(jnp.where(score == mx, jf, 1e9), axis=0, keepdims=True)
        hit = jf == idx
        sel = jnp.where(hit, 1.0, sel)
        score = jnp.where(hit, REMOVED, score)
        if fillers and it % every == 0:
            fillers.pop(0)()
    for f in fillers:
        f()
    return sel


def _nsa_prompt_kernel(q_ref, gl_ref, kc_ref, vc_ref, kaug_ref, vaug_ref, kwin_ref, scol_ref, srow_ref, coef_ref,
                       wbias_ref, cbias_ref, eye_ref, tilemap_ref, o_ref,
                       s0_s, s1_s, p0_s, p1_s, a0_s, a1_s, m_s, acc_s, tiles_s, *, t_len, tq, tk, tkw):
    qi = pl.program_id(2)
    t0 = qi * tq
    rows = HPG * tq
    q = q_ref[0]
    q4 = jnp.concatenate([q[:, h * HEAD_DIM:(h + 1) * HEAD_DIM] for h in range(HPG)], axis=0)
    slope = scol_ref[0]
    t_col = t0 + lax.broadcasted_iota(i32, (rows, 1), 0) % tq

    n_c = t_len // CMP_BLOCK
    n_pages = t_len // PAGE_SIZE
    jrow = _block_ids(n_pages, n_c, 1)
    jcol = _block_ids(n_pages, n_c, 0)
    kc = _unpage(kc_ref[0]).astype(bf16)
    vc = _unpage(vc_ref[0]).astype(bf16)
    o_cmp_box = []

    def cmp_output():
        o_cmp_box.append(_cmp_branch(q4, kc, vc, slope, t_col, tq, jrow)[0])

    def init_state():
        m_s[...] = jnp.full((rows, 1), NEG, f32)
        acc_s[...] = jnp.zeros(acc_s.shape, f32)
        p1_s[...] = jnp.zeros(p1_s.shape, bf16)
        a1_s[...] = jnp.ones((rows, 1), f32)

    nslots = (WINDOW + tq) // tkw
    base = (t0 - WINDOW) // tkw
    kts, vts, pens = [], [], []
    for m in range(nslots):
        kw = base + m
        kvw = kwin_ref[0, jnp.maximum(kw, 0)]
        kts.append(kvw[:HEAD_DIM])
        vts.append(kvw[HEAD_DIM:])
        pens.append(jnp.broadcast_to(jnp.where(kw >= 0, 0.0, NEG).astype(f32), (1, tkw)))
    sw = _dot(q4, jnp.concatenate(kts, axis=1))
    pen = jnp.concatenate(pens, axis=1)
    pw_h, lw_h = [], []

    def window_softmax(h):
        def run():
            s_h = sw[h * tq:(h + 1) * tq] + wbias_ref[0, h * tq:(h + 1) * tq] + pen
            p_h = jnp.exp(s_h - jnp.max(s_h, axis=-1, keepdims=True))
            pw_h.append(p_h.astype(bf16))
            lw_h.append(jnp.sum(p_h, axis=-1, keepdims=True))
        return run

    t_row = t0 + lax.broadcasted_iota(i32, (1, rows), 1) % tq
    c_end = (jcol + 1) * CMP_BLOCK - 1
    st = _dot_nt(kc, q4) - srow_ref[0] * (t_row.astype(f32) - c_end.astype(f32))
    maskt = c_end <= t_row
    st = jnp.where(maskt, st, NEG)
    et = jnp.exp(st - jnp.max(st, axis=0, keepdims=True))
    pt = et / jnp.sum(et, axis=0, keepdims=True)
    pt = jnp.where(maskt, pt, 0.0)
    impt = pt[:, 0:tq]
    for h in range(1, HPG):
        impt = impt + pt[:, h * tq:(h + 1) * tq]
    selt = _select_blocks_t(impt, t0 + lax.broadcasted_iota(i32, (1, tq), 1), n_c, jcol,
                            fillers=[window_softmax(h) for h in range(HPG)] + [cmp_output, init_state])
    o_cmp = o_cmp_box[0]
    sel = _dot_nt(eye_ref[...], selt.astype(bf16))
    selbias = ((sel - 1.0) * -NEG).astype(bf16)
    q_aug = jnp.concatenate([jnp.concatenate([selbias] * HPG, axis=0), q4, coef_ref[0]], axis=1)


    def scores_into(dst, kt):
        dst[...] = _dot(q_aug, kaug_ref[0, 0, kt])

    def softmax_into(s, kt, p_dst, a_dst):
        c = slope * (kt * tk - t0).astype(f32)
        m_i = m_s[...]
        m_new = jnp.maximum(m_i, jnp.max(s, axis=-1, keepdims=True) + c)
        p_dst[...] = jnp.exp(s - (m_new - c)).astype(bf16)
        a_dst[...] = jnp.exp(m_i - m_new)
        m_s[...] = m_new

    def values_from(p_src, a_src, kt):
        acc_s[...] = a_src[...] * acc_s[...] + _dot_nt(p_src[...], vaug_ref[0, 0, kt])

    kd = t0 // tk
    tile_hits = jnp.max(_dot(tilemap_ref[...], selt.astype(bf16)), axis=1, keepdims=True)
    n_act = jnp.int32(0)
    tiles_s[0] = 0
    for kt in range(t_len // tk):
        tiles_s[n_act] = kt
        hit = (jnp.max(tile_hits[kt:kt + 1, :]) > 0.5) & (kt < kd)
        n_act = n_act + hit.astype(i32)
    npairs = n_act // 2
    last_piped = jnp.maximum(2 * npairs - 1, 0)
    scores_into(s0_s, tiles_s[0])

    def pair(j, carry):
        ia = 2 * j
        ka = tiles_s[ia]
        kb = tiles_s[ia + 1]
        values_from(p1_s, a1_s, tiles_s[jnp.maximum(ia - 1, 0)])
        scores_into(s1_s, kb)
        softmax_into(s0_s[...], ka, p0_s, a0_s)
        values_from(p0_s, a0_s, ka)
        scores_into(s0_s, tiles_s[jnp.minimum(ia + 2, last_piped)])
        softmax_into(s1_s[...], kb, p1_s, a1_s)
        return carry

    lax.fori_loop(0, npairs, pair, 0)
    values_from(p1_s, a1_s, tiles_s[last_piped])

    def tile_plain(kt, bias):
        s = _dot(q_aug, kaug_ref[0, 0, kt])
        if bias is not None:
            s = s + bias
        softmax_into(s, kt, p0_s, a0_s)
        values_from(p0_s, a0_s, kt)

    @pl.when(n_act % 2 == 1)
    def _():
        tile_plain(tiles_s[jnp.maximum(n_act - 1, 0)], None)

    s_d = _dot(q_aug, kaug_ref[0, 0, kd]) + jnp.concatenate([cbias_ref[(t0 % tk) // tq]] * HPG, axis=0)
    o_win = (_dot_nt(jnp.concatenate(pw_h, axis=0), jnp.concatenate(vts, axis=1))
             / jnp.concatenate(lw_h, axis=0))
    softmax_into(s_d, kd, p0_s, a0_s)
    values_from(p0_s, a0_s, kd)
    acc = acc_s[...]
    o_sel = acc[:, :HEAD_DIM] / acc[:, HEAD_DIM:HEAD_DIM + 1]

    o_ref[0] = _gate_mix(gl_ref[0], o_cmp, o_sel, o_win, tq)


def _prompt_consts(t_len, tq, tk, tkw):
    n_c = t_len // CMP_BLOCK
    n_pages = t_len // PAGE_SIZE
    rows = HPG * tq
    r = np.arange(n_c)
    bpp = PAGE_SIZE // CMP_BLOCK
    ids = np.where(r < n_pages, bpp * r, bpp * (r - n_pages) + 1)
    onehot = ids[:, None] == (np.arange(t_len) // CMP_BLOCK)[None, :]
    e = onehot.reshape(n_c, t_len // tk, tk).transpose(1, 0, 2)
    lane = np.arange(tk)
    prow = np.zeros((HEAD_DIM, tk), np.float32)
    prow[0:3] = lane // 256
    prow[3:6] = lane % 256
    slopes = (2.0 ** (-8.0 * np.arange(1, N_HEADS + 1) / N_HEADS)).astype(np.float32)
    srow = np.repeat(slopes.reshape(N_KV, HPG), tq, axis=1)
    s1 = srow.astype(jnp.bfloat16).astype(np.float32)
    s2 = (srow - s1).astype(jnp.bfloat16).astype(np.float32)
    s3 = (srow - s1 - s2).astype(jnp.bfloat16).astype(np.float32)
    coef = np.zeros((N_KV, rows, HEAD_DIM), np.float32)
    for k, piece in enumerate((s1, s2, s3)):
        coef[:, :, k] = 256.0 * piece
        coef[:, :, 3 + k] = piece
    a = np.tile(np.arange(tq), HPG)[:, None]
    nslots = (WINDOW + tq) // tkw
    dpos = a + WINDOW - np.arange(nslots * tkw)[None, :]
    band = np.where((dpos >= 0) & (dpos <= WINDOW), 0.0, NEG)
    wbias = -srow[:, :, None] * dpos[None] + band[None]
    off = np.arange(tk // tq)[:, None, None] * tq
    cbias = np.where(lane[None, None, :] <= off + np.arange(tq)[None, :, None], 0.0, NEG)
    n_tiles = t_len // tk
    tilemap = np.arange(-(-n_tiles // 16) * 16)[:, None] == (ids // (tk // CMP_BLOCK))[None, :]
    return dict(
        tilemap=jnp.asarray(tilemap, bf16),
        scol=jnp.asarray(srow.reshape(N_KV, rows, 1), f32), srow=jnp.asarray(srow.reshape(N_KV, 1, rows), f32),
        coef=jnp.asarray(coef, bf16), e=jnp.asarray(e, bf16), prow=jnp.asarray(prow, bf16),
        ones=jnp.ones((16, tk), bf16), wbias=jnp.asarray(wbias, f32), cbias=jnp.asarray(cbias, f32),
        eye=jnp.eye(tq, dtype=bf16))


def _nsa_prompt(q, gl, kc, vc, kaug, vaug, kwin, c, tq):
    bsz, t_len, _ = q.shape
    tk = kaug.shape[-1]
    tkw = kwin.shape[-1]
    npb = t_len // PAGE_SIZE
    rows = HPG * tq
    kern = functools.partial(_nsa_prompt_kernel, t_len=t_len, tq=tq, tk=tk, tkw=tkw)
    augspec = lambda arr: pl.BlockSpec((1, 1) + arr.shape[2:], lambda b, g, i: (b, g, 0, 0, 0))
    cspec = pl.BlockSpec((1, npb, kc.shape[-1]), lambda b, g, i: (g, b, 0))
    perg = lambda arr: pl.BlockSpec((1,) + arr.shape[1:], lambda b, g, i: (g,) + (0,) * (arr.ndim - 1))
    whole = lambda arr: pl.BlockSpec(arr.shape, lambda b, g, i: (0,) * arr.ndim)
    return pl.pallas_call(
        kern,
        grid=(bsz, N_KV, t_len // tq),
        in_specs=[
            pl.BlockSpec((1, tq, QGRP), lambda b, g, i: (b, i, g)),
            pl.BlockSpec((1, tq, 128), lambda b, g, i: (b, i, g)),
            cspec, cspec,
            augspec(kaug), augspec(vaug),
            pl.BlockSpec((1, t_len // tkw, GRP, tkw), lambda b, g, i: (b, 0, g, 0)),
            perg(c["scol"]), perg(c["srow"]), perg(c["coef"]),
            perg(c["wbias"]), whole(c["cbias"]), whole(c["eye"]), whole(c["tilemap"]),
        ],
        out_specs=pl.BlockSpec((1, tq, QGRP), lambda b, g, i: (b, i, g)),
        out_shape=jax.ShapeDtypeStruct((bsz, t_len, N_KV * QGRP), f32),
        scratch_shapes=[
            pltpu.VMEM((rows, tk), f32), pltpu.VMEM((rows, tk), f32),
            pltpu.VMEM((rows, tk), bf16), pltpu.VMEM((rows, tk), bf16),
            pltpu.VMEM((rows, 1), f32), pltpu.VMEM((rows, 1), f32),
            pltpu.VMEM((rows, 1), f32), pltpu.VMEM((rows, vaug.shape[3]), f32),
            pltpu.SMEM((t_len // tk + 8,), i32),
        ],
        compiler_params=_params(("arbitrary", "arbitrary", "arbitrary")),
        name="nsa_prompt_attn",
    )(q, gl, kc, vc, kaug, vaug, kwin, c["scol"], c["srow"], c["coef"], c["wbias"], c["cbias"], c["eye"],
      c["tilemap"])


def _nsa_sample_kernel(pt_ref, q_ref, gl_ref, kcall_ref, vcall_ref, pid_ref, newsel_ref, swin_ref, nwin_ref,
                       slope_ref, *rest, s_len, past_len, pp, n_groups, lanes):
    page_refs = rest[:pp]
    o_ref = rest[pp]
    sel_s, ocmp_s, qbd_s, m_s, l_s, acc_s = rest[pp + 1:]
    j = pl.program_id(1)
    rows = HPG * s_len
    n_c = past_len // CMP_BLOCK
    npb = past_len // PAGE_SIZE
    jrow = _block_ids(npb, lanes, 1)
    jcol = _block_ids(npb, lanes, 0)
    q = q_ref[0].astype(f32)
    t_col = past_len + lax.broadcasted_iota(i32, (rows, 1), 0) % s_len
    tq_col = past_len + lax.broadcasted_iota(i32, (s_len, 1), 0)

    def q_of(g):
        return jnp.concatenate(
            [q[:, (g * HPG + h) * HEAD_DIM:(g * HPG + h + 1) * HEAD_DIM] for h in range(HPG)],
            axis=0).astype(bf16)

    groups = range(N_KV)
    all_rows = N_KV * rows
    slope_all = jnp.concatenate([slope_ref[g] for g in groups], axis=0)
    t_all = past_len + lax.broadcasted_iota(i32, (all_rows, 1), 0) % s_len

    @pl.when(j == 0)
    def _():
        n_all = kcall_ref.shape[1]
        gather = (lax.broadcasted_iota(i32, (npb, n_all), 1) == pid_ref[0]).astype(bf16)
        zeros = jnp.zeros((rows, HEAD_DIM), bf16)
        for g in groups:
            kc = _unpage(_dot(gather, kcall_ref[g])).astype(bf16)
            vc = _unpage(_dot(gather, vcall_ref[g])).astype(bf16)
            q4 = q_of(g)
            o_cmp, imp = _cmp_branch(q4, kc, vc, slope_ref[g], t_col, s_len, jrow[:, :n_c])
            imp = jnp.concatenate([imp, jnp.zeros((s_len, lanes - n_c), f32)], axis=1)
            sel_s[g * s_len:(g + 1) * s_len] = _select_blocks(imp, tq_col, n_c + 1, jrow)
            ocmp_s[g] = o_cmp
            qbd_s[g * rows:(g + 1) * rows] = jnp.concatenate([q4 if gg == g else zeros for gg in groups], axis=1)
        m_s[...] = jnp.full((all_rows, 1), NEG, f32)
        l_s[...] = jnp.zeros((all_rows, 1), f32)
        acc_s[...] = jnp.zeros(acc_s.shape, f32)

    def stack(keys):
        kt = jnp.concatenate([keys[g * GRP:g * GRP + HEAD_DIM] for g in groups], axis=0).astype(bf16)
        vt = jnp.concatenate([keys[g * GRP + HEAD_DIM:(g + 1) * GRP] for g in groups], axis=0).astype(bf16)
        return kt, vt

    def attend(keys, pos):
        onehot = (jcol == pos // CMP_BLOCK).astype(bf16)
        msk = _dot(sel_s[...].astype(bf16), onehot)
        msk = jnp.concatenate([msk[g * s_len:(g + 1) * s_len] for g in groups for _ in range(HPG)], axis=0)
        valid = (msk > 0.5) & (pos <= t_all)
        kt, vt = stack(keys)
        m_new, l_new, acc = _flash_update((m_s[...], l_s[...], acc_s[...]), qbd_s[...], kt, vt, slope_all,
                                          valid, (pos - past_len).astype(f32))
        m_s[...] = m_new
        l_s[...] = l_new
        acc_s[...] = acc

    nk = pp * PAGE_SIZE
    keys = jnp.concatenate([r[0] for r in page_refs], axis=1)
    attend(keys, j * nk + lax.broadcasted_iota(i32, (1, nk), 1))

    @pl.when(j == n_groups - 1)
    def _():
        new_keys = newsel_ref[0]
        attend(new_keys, past_len + lax.broadcasted_iota(i32, (1, new_keys.shape[1]), 1))
        o_sel = acc_s[...] / l_s[...]
        carry = (jnp.full((all_rows, 1), NEG, f32), jnp.zeros((all_rows, 1), f32),
                 jnp.zeros((all_rows, N_KV * HEAD_DIM), f32))
        for wkeys, wbase in ((swin_ref[0], past_len - swin_ref.shape[2]), (nwin_ref[0], past_len)):
            wpos = wbase + lax.broadcasted_iota(i32, (1, wkeys.shape[1]), 1)
            dpos = t_all - wpos
            valid = (dpos >= 0) & (dpos <= WINDOW)
            kt, vt = stack(wkeys)
            carry = _flash_update(carry, qbd_s[...], kt, vt, slope_all, valid, (wpos - past_len).astype(f32))
        o_win = carry[2] / carry[1]
        own = lambda o, g: o[g * rows:(g + 1) * rows, g * HEAD_DIM:(g + 1) * HEAD_DIM]
        gl = gl_ref[0]
        o_ref[0] = jnp.concatenate(
            [_gate_mix(gl[:, g * 128:(g + 1) * 128], ocmp_s[g], own(o_sel, g), own(o_win, g), s_len)
             for g in groups], axis=1)


def _nsa_sample(page_table, q, gl, kc_all, vc_all, pids, new_sel, state_win, new_win, slopes, cache_sel, pp):
    bsz, s_len, hd = q.shape
    n_pages = page_table.shape[1]
    past_len = n_pages * PAGE_SIZE
    pp = min(pp, n_pages)
    n_groups = n_pages // pp
    n_c = past_len // CMP_BLOCK
    lanes = -(-(n_c + 2) // 128) * 128
    rows = HPG * s_len
    kvw = cache_sel.shape[1]
    perb = lambda b, j, pt: (b, 0, 0)

    def page_spec(i):
        return pl.BlockSpec((1, kvw, PAGE_SIZE), lambda b, j, pt: (pt[b, j * pp + i], 0, 0))

    kern = functools.partial(_nsa_sample_kernel, s_len=s_len, past_len=past_len, pp=pp,
                             n_groups=n_groups, lanes=lanes)
    grid_spec = pltpu.PrefetchScalarGridSpec(
        num_scalar_prefetch=1,
        grid=(bsz, n_groups),
        in_specs=[
            pl.BlockSpec((1, s_len, hd), perb),
            pl.BlockSpec((1, s_len, gl.shape[-1]), perb),
            pl.BlockSpec(kc_all.shape, lambda b, j, pt: (0, 0, 0)),
            pl.BlockSpec(vc_all.shape, lambda b, j, pt: (0, 0, 0)),
            pl.BlockSpec((1, n_pages, 1), perb),
            pl.BlockSpec((1,) + new_sel.shape[1:], perb),
            pl.BlockSpec((1,) + state_win.shape[1:], perb),
            pl.BlockSpec((1,) + new_win.shape[1:], perb),
            pl.BlockSpec(slopes.shape, lambda b, j, pt: (0, 0, 0)),
        ] + [page_spec(i) for i in range(pp)],
        out_specs=pl.BlockSpec((1, s_len, hd), perb),
        scratch_shapes=[
            pltpu.VMEM((N_KV * s_len, lanes), f32),
            pltpu.VMEM((N_KV, rows, HEAD_DIM), f32),
            pltpu.VMEM((N_KV * rows, N_KV * HEAD_DIM), bf16),
            pltpu.VMEM((N_KV * rows, 1), f32),
            pltpu.VMEM((N_KV * rows, 1), f32),
            pltpu.VMEM((N_KV * rows, N_KV * HEAD_DIM), f32),
        ],
    )
    return pl.pallas_call(
        kern,
        grid_spec=grid_spec,
        out_shape=jax.ShapeDtypeStruct((bsz, s_len, hd), f32),
        compiler_params=_params(("arbitrary", "arbitrary")),
        name="nsa_sample_attn",
    )(page_table, q, gl, kc_all, vc_all, pids, new_sel, state_win, new_win, slopes, *([cache_sel] * pp))


def _out_kernel(o_ref, z_ref, x_ref, mod_ref, w_ref, lng_ref, lnb_ref, xo_ref, *, nb, tr, d):
    y = _dot((o_ref[...] * _silu(z_ref[...])).astype(bf16), w_ref[...]).reshape(nb, tr, d)
    gate = mod_ref[...][:, 2:3, :]
    xo_ref[...] = _layer_norm(DN_ALPHA * x_ref[...] + (1.0 + gate) * y, lng_ref[...], lnb_ref[...])


def _out_proj(o, z, x, mod, w_out, ln_g, ln_b, nb, tr):
    bsz, t, d = x.shape
    hd = o.shape[-1]
    nt = t // tr
    const2 = lambda b, i: (0, 0)
    kern = functools.partial(_out_kernel, nb=nb, tr=tr, d=d)
    return pl.pallas_call(
        kern,
        grid=(bsz // nb, nt),
        in_specs=[
            pl.BlockSpec((nb * tr, hd), lambda b, i: (b * nt + i, 0)),
            pl.BlockSpec((nb * tr, hd), lambda b, i: (b * nt + i, 0)),
            pl.BlockSpec((nb, tr, d), lambda b, i: (b, i, 0)),
            pl.BlockSpec((nb, 3, d), lambda b, i: (b, 0, 0)),
            pl.BlockSpec(w_out.shape, const2),
            pl.BlockSpec((1, d), const2),
            pl.BlockSpec((1, d), const2),
        ],
        out_specs=pl.BlockSpec((nb, tr, d), lambda b, i: (b, i, 0)),
        out_shape=jax.ShapeDtypeStruct((bsz, t, d), f32),
        compiler_params=_params(("arbitrary", "arbitrary")),
        name="nsa_out",
    )(o, z, x, mod, w_out, ln_g, ln_b)


def _head_slopes(nq):
    s = 2.0 ** (-8.0 * np.arange(1, N_HEADS + 1) / N_HEADS)
    s = np.repeat(s.reshape(N_KV, HPG), nq, axis=1)
    return jnp.asarray(s.reshape(N_KV, HPG * nq, 1), f32)


def kernel(x_prompt, x_sample, c_prompt, c_sample, state_h, state_conv, cache_cmp, cache_sel, state_win, page_table, w_ada, b_ada, ln_g, ln_b, w_in_a, conv_w_a, conv_b_a, w_r_a, b_r_a, w_i_a, b_i_a, lam_a, w_out_a, w_kv, phi_pe, w_phi1, b_phi1, w_phi2, b_phi2, w_in_b, b_gate_b, w_out_b):
    bp, t_len, d = x_prompt.shape
    bs, s_len, _ = x_sample.shape
    n_pages = page_table.shape[1]
    past_len = n_pages * PAGE_SIZE
    hd = N_HEADS * HEAD_DIM
    kvw = N_KV * GRP
    assert s_len <= CMP_BLOCK and t_len % 128 == 0 and w_ada.shape[0] == DEPTH == 2

    n_c = bp + bs
    pad = -n_c % 8
    c_all = jnp.concatenate([c_prompt, c_sample, jnp.zeros((pad, d), f32)], axis=0)
    mods = _ada_mod(c_all, w_ada, b_ada).reshape(DEPTH, n_c + pad, 3, d)
    mod_p = mods[:, :bp]
    mod_s = mods[:, bp:n_c]

    rg = d // N_RG_BLOCKS
    w_in0 = w_in_a[0].astype(bf16)
    w_gate = jnp.concatenate([w_r_a[0], w_i_a[0]], axis=-1).astype(bf16)
    w_out0 = w_out_a[0].astype(bf16)
    row = lambda v: v.reshape(1, -1)
    rg_args = (w_in0, conv_w_a[0], row(conv_b_a[0]), w_gate, row(b_r_a[0]), row(b_i_a[0]), row(lam_a[0]),
               w_out0, row(ln_g[0]), row(ln_b[0]))
    xp1, h_p, conv_p = _rglru_prompt(x_prompt, mod_p[0], jnp.zeros((bp, 1, d), f32),
                                     jnp.zeros((bp, CONV_W - 1, d), f32), *rg_args, tc=256)
    xs1_tm, h_s, conv_s_tm = _rglru_sample(
        x_sample.transpose(1, 0, 2), mod_s[0].transpose(1, 0, 2), state_h[0],
        state_conv[0].transpose(1, 0, 2), *rg_args)
    xs1 = xs1_tm.transpose(1, 0, 2)
    new_h_p = h_p.reshape(1, bp, d)
    new_conv_p = conv_p.reshape(1, bp, CONV_W - 1, d)
    new_h_s = h_s.reshape(1, bs, d)
    new_conv_s = conv_s_tm.transpose(1, 0, 2).reshape(1, bs, CONV_W - 1, d)

    w_b = w_in_b[0]
    w_q = w_b[:, :hd].astype(bf16)
    w_z = w_b[:, hd:2 * hd].astype(bf16)
    gpg = HPG * 3
    w_g = jnp.pad(w_b[:, 2 * hd:].reshape(d, N_KV, gpg), ((0, 0), (0, 0), (0, 128 - gpg)))
    w_g = w_g.reshape(d, N_KV * 128).astype(bf16)
    b_g = jnp.pad(b_gate_b[0].reshape(N_KV, gpg), ((0, 0), (0, 128 - gpg))).reshape(1, N_KV * 128)
    tmin = lambda v: v.transpose(0, 2, 3, 4, 1)
    tq = min(128, t_len)
    tk = min(512, t_len)
    consts = _prompt_consts(t_len, tq, tk, PAGE_SIZE)
    cmpt_p, selt_p, wint_p, cmp_pages, kaug_p, vaug_p, wint_tiles, q_p, z_p, gl_p = _proj_prompt(
        xp1, mod_p[1], w_kv.T.astype(bf16), w_q, w_z, w_g, b_g, consts["e"], consts["prow"], consts["ones"],
        tr=tk)
    cmp_s, sel_s, win_s, selb_s, winb_s, q_s, z_s, gl_s = _proj(
        xs1, mod_s[1], w_kv.astype(bf16), w_q, w_z, w_g, b_g, nb=bs, tr=s_len)
    per_p = lambda v: v.reshape(bp, t_len, v.shape[-1])
    per_s = lambda v: v.reshape(bs, s_len, v.shape[-1])

    bpp = PAGE_SIZE // CMP_BLOCK
    dphi = w_phi2.shape[1]
    diag2 = lambda w: jnp.concatenate([jnp.concatenate([w, jnp.zeros_like(w)], axis=-1),
                                       jnp.concatenate([jnp.zeros_like(w), w], axis=-1)], axis=-2)
    pet = jnp.tile(phi_pe.transpose(1, 2, 0), (1, 1, bpp))
    w1 = diag2(w_phi1.transpose(0, 2, 1, 3)).astype(bf16)
    w1 = w1.reshape(2, HEAD_DIM // 2, 2 * PAGE_SIZE, bpp * dphi)
    b1 = jnp.tile(b_phi1, (1, bpp)).reshape(2, 1, bpp * dphi)
    w2 = diag2(w_phi2).astype(bf16)
    b2 = jnp.tile(b_phi2, (1, bpp)).reshape(2, 1, bpp * HEAD_DIM)
    kc_p, vc_p = _compress(cmp_pages.reshape(-1, PAGE_SIZE), pet, w1, b1, w2, b2, npg=64, out_dtype=f32)
    kc_all, vc_all = _compress(tmin(cache_cmp).reshape(-1, PAGE_SIZE), pet, w1, b1, w2, b2,
                               npg=64, out_dtype=bf16)

    o_p = _nsa_prompt(per_p(q_p), per_p(gl_p), kc_p, vc_p, kaug_p, vaug_p, wint_tiles, consts, tq=tq)
    tpad = lambda v: jnp.pad(per_s(v), ((0, 0), (0, PAGE_SIZE - s_len), (0, 0))).transpose(0, 2, 1)
    o_s = _nsa_sample(page_table, per_s(q_s), per_s(gl_s), kc_all, vc_all, page_table.reshape(bs, n_pages, 1),
                      tpad(selb_s), tmin(state_win).reshape(bs, kvw, -1), tpad(winb_s), _head_slopes(s_len),
                      tmin(cache_sel).reshape(-1, kvw, PAGE_SIZE), pp=16)

    w_out1 = w_out_b[0].astype(bf16)
    y_p = _out_proj(o_p.reshape(bp * t_len, hd), z_p, xp1, mod_p[1], w_out1, row(ln_g[1]), row(ln_b[1]),
                    nb=1, tr=min(512, t_len))
    y_s = _out_proj(o_s.reshape(bs * s_len, hd), z_s, xs1, mod_s[1], w_out1, row(ln_g[1]), row(ln_b[1]),
                    nb=bs, tr=s_len)

    kv5 = lambda v: v.reshape(bs, s_len, N_KV, 2, HEAD_DIM)
    kv5t = lambda v: v.reshape(bp, N_KV, 2, HEAD_DIM, -1).transpose(0, 4, 1, 2, 3)
    wb = state_win.shape[1]
    new_win_p = kv5t(wint_p[:, :, -min(WINDOW, t_len):])
    new_win_s = jnp.concatenate([state_win, kv5(win_s)], axis=1)[:, -wb:]
    return (y_p, y_s, kv5t(cmpt_p), kv5t(selt_p), new_win_p, new_h_p, new_conv_p,
            kv5(cmp_s), kv5(sel_s), new_win_s, new_h_s, new_conv_s)
```

```python
import functools
import math

import numpy as np
import jax
import jax.numpy as jnp
from jax import lax
from jax.experimental import pallas as pl
from jax.experimental.pallas import tpu as pltpu

f32 = jnp.float32
bf16 = jnp.bfloat16
i32 = jnp.int32

DEPTH = 2
N_RG_BLOCKS = 8
CONV_W = 4
RG_C = 8.0
N_HEADS = 16
HEAD_DIM = 64
N_KV = 4
HPG = N_HEADS // N_KV
CMP_BLOCK = 64
N_SEL = 16
WINDOW = 512
PAGE_SIZE = 128
DN_ALPHA = (2.0 * DEPTH) ** 0.25
LN_EPS = 1e-5
NEG = -1e30
FORCED = 1e6
REMOVED = -3e38
GRP = 2 * HEAD_DIM
QGRP = HPG * HEAD_DIM
PAGE_PITCH = N_KV * GRP + 8
VMEM_LIMIT_BYTES = 56 * 1024 * 1024


def _params(sem):
    return pltpu.CompilerParams(dimension_semantics=sem, vmem_limit_bytes=VMEM_LIMIT_BYTES)


def _dot(a, b):
    return jnp.dot(a, b, preferred_element_type=f32)


def _dot_nt(a, b):
    return lax.dot_general(a, b, (((1,), (1,)), ((), ())), preferred_element_type=f32)


def _sigmoid(x):
    return 1.0 / (1.0 + jnp.exp(-x))


def _silu(x):
    return x * _sigmoid(x)


def _log1p(e):
    u = 1.0 + e
    dlt = u - 1.0
    return jnp.where(dlt == 0.0, e, jnp.log(u) * (e / jnp.where(dlt == 0.0, 1.0, dlt)))


def _layer_norm(x, g, b):
    mu = jnp.mean(x, axis=-1, keepdims=True)
    xc = x - mu
    var = jnp.mean(xc * xc, axis=-1, keepdims=True)
    return xc * lax.rsqrt(var + LN_EPS) * g + b


def _ada_kernel(c_ref, w_ref, b_ref, o_ref):
    a = _silu(c_ref[...])
    o_ref[0] = jnp.dot(a, w_ref[0], preferred_element_type=f32,
                       precision=lax.Precision.HIGHEST) + b_ref[0]


def _ada_mod(c_all, w_ada, b_ada):
    rows, d = c_all.shape
    depth = w_ada.shape[0]
    return pl.pallas_call(
        _ada_kernel,
        grid=(depth, 3),
        in_specs=[
            pl.BlockSpec((rows, d), lambda l, n: (0, 0)),
            pl.BlockSpec((1, d, d), lambda l, n: (l, 0, n)),
            pl.BlockSpec((1, 1, d), lambda l, n: (l, 0, n)),
        ],
        out_specs=pl.BlockSpec((1, rows, d), lambda l, n: (l, 0, n)),
        out_shape=jax.ShapeDtypeStruct((depth, rows, 3 * d), f32),
        compiler_params=_params(("arbitrary", "arbitrary")),
        name="ada_mod",
    )(c_all, w_ada, b_ada.reshape(depth, 1, 3 * d))


def _rglru_gates(xc, wg_ref, br, bi, lam):
    xcb = xc.astype(bf16)
    rg = xc.shape[1] // N_RG_BLOCKS
    rs, is_ = [], []
    for n in range(N_RG_BLOCKS):
        g = _dot(xcb[:, n * rg:(n + 1) * rg], wg_ref[n])
        rs.append(g[:, :rg])
        is_.append(g[:, rg:])
    r = _sigmoid(jnp.concatenate(rs, axis=1) + br)
    i = _sigmoid(jnp.concatenate(is_, axis=1) + bi)
    nl = -lam
    softplus = jnp.maximum(nl, 0.0) + _log1p(jnp.exp(-jnp.abs(nl)))
    log_a = (-RG_C * softplus) * r
    a = jnp.exp(log_a)
    gain = jnp.sqrt(jnp.maximum(-jnp.tanh(log_a) * (a * a + 1.0), 0.0))
    b = gain * i * xc
    return a, b


def _rglru_prompt_kernel(x_ref, mod_ref, h0_ref, c0_ref, win_ref, cw_ref, cb_ref, wg_ref, br_ref,
                         bi_ref, lam_ref, wout_ref, lng_ref, lnb_ref,
                         xo_ref, hl_ref, cl_ref, xbuf, a_s, b_s, hs_s, hc, *, tc, d):
    t = pl.program_id(1)

    @pl.when(t == 0)
    def _():
        xbuf[0:8, :] = jnp.zeros((8, d), f32)
        xbuf[8 - (CONV_W - 1):8, :] = c0_ref[0]
        hc[...] = jnp.broadcast_to(h0_ref[0], (8, d))

    x = x_ref[0]
    mod = mod_ref[0]
    shift, scale, gate = mod[0:1], mod[1:2], mod[2:3]
    m = x * (1.0 + scale) + shift
    u = _dot(m.astype(bf16), win_ref[...])
    xb = u[:, :d]
    zg = u[:, d:]
    xbuf[8:8 + tc, :] = xb
    base = 8 - (CONV_W - 1)
    xc = cb_ref[...] + xbuf[base:base + tc, :] * cw_ref[0:1, :]
    for k in range(1, CONV_W):
        xc = xc + xbuf[base + k:base + k + tc, :] * cw_ref[k:k + 1, :]
    tail = xbuf[8 + tc - (CONV_W - 1):8 + tc, :]
    xbuf[base:8, :] = tail
    cl_ref[0] = tail

    a, b = _rglru_gates(xc, wg_ref, br_ref[...], bi_ref[...], lam_ref[...])

    rowi = lax.broadcasted_iota(i32, (tc, 1), 0) % 8
    for s in (1, 2, 4):
        ok = rowi >= s
        a_sh = pltpu.roll(a, s, 0)
        b_sh = pltpu.roll(b, s, 0)
        b = jnp.where(ok, a * b_sh + b, b)
        a = jnp.where(ok, a * a_sh, a)
    a_s[...] = a
    b_s[...] = b

    def tile_step(j, hprev):
        r0 = pl.multiple_of(j * 8, 8)
        ht = a_s[pl.ds(r0, 8), :] * hprev + b_s[pl.ds(r0, 8), :]
        hs_s[pl.ds(r0, 8), :] = ht
        return jnp.broadcast_to(ht[7:8, :], (8, d))

    hlast = lax.fori_loop(0, tc // 8, tile_step, hc[...])
    hc[...] = hlast
    hl_ref[0] = hlast[0:1, :]

    y = _dot((hs_s[...] * _silu(zg)).astype(bf16), wout_ref[...])
    xo_ref[0] = _layer_norm(DN_ALPHA * x + (1.0 + gate) * y, lng_ref[...], lnb_ref[...])


def _rglru_prompt(x, mod, h0, c0, w_in, conv_w, conv_b, w_gate, b_r, b_i, lam, w_out, ln_g, ln_b, tc):
    bsz, t, d = x.shape
    tc = min(tc, t)
    const2 = lambda b, i: (0, 0)
    const3 = lambda b, i: (0, 0, 0)
    perb = lambda b, i: (b, 0, 0)
    kern = functools.partial(_rglru_prompt_kernel, tc=tc, d=d)
    return pl.pallas_call(
        kern,
        grid=(bsz, t // tc),
        in_specs=[
            pl.BlockSpec((1, tc, d), lambda b, i: (b, i, 0)),
            pl.BlockSpec((1, 3, d), perb),
            pl.BlockSpec((1, 1, d), perb),
            pl.BlockSpec((1, CONV_W - 1, d), perb),
            pl.BlockSpec(w_in.shape, const2),
            pl.BlockSpec(conv_w.shape, const2),
            pl.BlockSpec((1, d), const2),
            pl.BlockSpec(w_gate.shape, const3),
            pl.BlockSpec((1, d), const2),
            pl.BlockSpec((1, d), const2),
            pl.BlockSpec((1, d), const2),
            pl.BlockSpec(w_out.shape, const2),
            pl.BlockSpec((1, d), const2),
            pl.BlockSpec((1, d), const2),
        ],
        out_specs=[
            pl.BlockSpec((1, tc, d), lambda b, i: (b, i, 0)),
            pl.BlockSpec((1, 1, d), perb),
            pl.BlockSpec((1, CONV_W - 1, d), perb),
        ],
        out_shape=[
            jax.ShapeDtypeStruct((bsz, t, d), f32),
            jax.ShapeDtypeStruct((bsz, 1, d), f32),
            jax.ShapeDtypeStruct((bsz, CONV_W - 1, d), f32),
        ],
        scratch_shapes=[
            pltpu.VMEM((tc + 8, d), f32),
            pltpu.VMEM((tc, d), f32),
            pltpu.VMEM((tc, d), f32),
            pltpu.VMEM((tc, d), f32),
            pltpu.VMEM((8, d), f32),
        ],
        compiler_params=_params(("arbitrary", "arbitrary")),
        name="rglru_prompt",
    )(x, mod, h0, c0, w_in, conv_w, conv_b, w_gate, b_r, b_i, lam, w_out, ln_g, ln_b)


def _rglru_sample_kernel(x_ref, mod_ref, h0_ref, c0_ref, win_ref, cw_ref, cb_ref, wg_ref, br_ref,
                         bi_ref, lam_ref, wout_ref, lng_ref, lnb_ref,
                         xo_ref, hl_ref, cl_ref, *, s_len, bsz, d):
    x = x_ref[...]
    mod = mod_ref[...]
    shift, scale, gate = mod[0:1], mod[1:2], mod[2:3]
    m = x * (1.0 + scale) + shift
    u = _dot(m.reshape(s_len * bsz, d).astype(bf16), win_ref[...])
    xb = u[:, :d].reshape(s_len, bsz, d)
    zg = u[:, d:]
    xp = jnp.concatenate([c0_ref[...], xb], axis=0)
    cw = cw_ref[...]
    xc = cb_ref[...] + xp[0:s_len] * cw[0:1]
    for k in range(1, CONV_W):
        xc = xc + xp[k:k + s_len] * cw[k:k + 1]
    cl_ref[...] = xp[s_len:s_len + CONV_W - 1]
    a, b = _rglru_gates(xc.reshape(s_len * bsz, d), wg_ref, br_ref[...], bi_ref[...], lam_ref[...])
    h = h0_ref[...]
    hs = []
    for s in range(s_len):
        h = a[s * bsz:(s + 1) * bsz] * h + b[s * bsz:(s + 1) * bsz]
        hs.append(h)
    hl_ref[...] = h
    hs = jnp.concatenate(hs, axis=0)
    y = _dot((hs * _silu(zg)).astype(bf16), wout_ref[...])
    xo = _layer_norm(DN_ALPHA * x + (1.0 + gate) * y.reshape(s_len, bsz, d), lng_ref[...], lnb_ref[...])
    xo_ref[...] = xo


def _rglru_sample(x_tm, mod_tm, h0, c0_tm, w_in, conv_w, conv_b, w_gate, b_r, b_i, lam, w_out, ln_g, ln_b):
    s_len, bsz, d = x_tm.shape
    kern = functools.partial(_rglru_sample_kernel, s_len=s_len, bsz=bsz, d=d)
    return pl.pallas_call(
        kern,
        out_shape=[
            jax.ShapeDtypeStruct((s_len, bsz, d), f32),
            jax.ShapeDtypeStruct((bsz, d), f32),
            jax.ShapeDtypeStruct((CONV_W - 1, bsz, d), f32),
        ],
        compiler_params=pltpu.CompilerParams(vmem_limit_bytes=VMEM_LIMIT_BYTES),
        name="rglru_sample",
    )(x_tm, mod_tm, h0, c0_tm, w_in, conv_w, conv_b, w_gate, b_r, b_i, lam, w_out, ln_g, ln_b)


def _proj_kernel(x_ref, mod_ref, wkv_ref, wq_ref, wz_ref, wg_ref, bg_ref,
                 cmp_ref, sel_ref, win_ref, selb_ref, winb_ref, q_ref, z_ref, gl_ref, *, nb, tr, d):
    x = x_ref[...]
    mod = mod_ref[...]
    m = x * (1.0 + mod[:, 1:2, :]) + mod[:, 0:1, :]
    xb = x.reshape(nb * tr, d).astype(bf16)
    mb = m.reshape(nb * tr, d).astype(bf16)
    kv = _dot(xb, wkv_ref[...])
    w = kv.shape[1] // 3
    sel = kv[:, w:2 * w]
    win = kv[:, 2 * w:]
    cmp_ref[...] = kv[:, :w]
    sel_ref[...] = sel
    win_ref[...] = win
    selb_ref[...] = sel.astype(bf16)
    winb_ref[...] = win.astype(bf16)
    q_ref[...] = (_dot(mb, wq_ref[...]) * (HEAD_DIM ** -0.5)).astype(bf16)
    z_ref[...] = _dot(mb, wz_ref[...])
    gl_ref[...] = _dot(mb, wg_ref[...]) + bg_ref[...]


def _proj(x, mod, w_kv, w_q, w_z, w_g, b_g, nb, tr):
    bsz, t, d = x.shape
    kvw = w_kv.shape[1] // 3
    hd = w_q.shape[1]
    gw = w_g.shape[1]
    nt = t // tr
    const2 = lambda b, i: (0, 0)
    blk = lambda width: pl.BlockSpec((nb * tr, width), lambda b, i: (b * nt + i, 0))
    out = lambda width, dt: jax.ShapeDtypeStruct((bsz * t, width), dt)
    kern = functools.partial(_proj_kernel, nb=nb, tr=tr, d=d)
    return pl.pallas_call(
        kern,
        grid=(bsz // nb, nt),
        in_specs=[
            pl.BlockSpec((nb, tr, d), lambda b, i: (b, i, 0)),
            pl.BlockSpec((nb, 3, d), lambda b, i: (b, 0, 0)),
            pl.BlockSpec(w_kv.shape, const2),
            pl.BlockSpec(w_q.shape, const2),
            pl.BlockSpec(w_z.shape, const2),
            pl.BlockSpec(w_g.shape, const2),
            pl.BlockSpec((1, gw), const2),
        ],
        out_specs=[blk(kvw), blk(kvw), blk(kvw), blk(kvw), blk(kvw), blk(hd), blk(hd), blk(gw)],
        out_shape=[out(kvw, f32), out(kvw, f32), out(kvw, f32), out(kvw, bf16), out(kvw, bf16),
                   out(hd, bf16), out(hd, f32), out(gw, f32)],
        compiler_params=_params(("arbitrary", "arbitrary")),
        name="nsa_proj",
    )(x, mod, w_kv, w_q, w_z, w_g, b_g)


def _proj_prompt_kernel(x_ref, mod_ref, wkvt_ref, wq_ref, wz_ref, wg_ref, bg_ref, e_ref, prow_ref, ones_ref,
                        cmpt_ref, selt_ref, wint_ref, cmppg_ref, kaug_ref, vaug_ref, wintt_ref, q_ref, z_ref,
                        gl_ref, *, tr):
    x = x_ref[0]
    mod = mod_ref[0]
    m = x * (1.0 + mod[1:2]) + mod[0:1]
    mb = m.astype(bf16)
    kvt = _dot_nt(wkvt_ref[...], x.astype(bf16))
    kvw = kvt.shape[0] // 3
    cmpt = kvt[:kvw]
    selt = kvt[kvw:2 * kvw]
    wint = kvt[2 * kvw:]
    cmpt_ref[0] = cmpt
    selt_ref[0] = selt
    wint_ref[0] = wint
    selb = selt.astype(bf16)
    n_c = e_ref.shape[1]
    for g in range(N_KV):
        kaug_ref[0, g, 0, 0:n_c] = e_ref[0]
        kaug_ref[0, g, 0, n_c:n_c + HEAD_DIM] = selb[g * GRP:g * GRP + HEAD_DIM]
        kaug_ref[0, g, 0, n_c + HEAD_DIM:] = prow_ref[...]
        vaug_ref[0, g, 0, 0:HEAD_DIM] = selb[g * GRP + HEAD_DIM:(g + 1) * GRP]
        vaug_ref[0, g, 0, HEAD_DIM:] = ones_ref[...]
    winb = wint.astype(bf16)
    for k in range(tr // PAGE_SIZE):
        cmppg_ref[k] = cmpt[:, k * PAGE_SIZE:(k + 1) * PAGE_SIZE]
        wintt_ref[0, k] = winb[:, k * PAGE_SIZE:(k + 1) * PAGE_SIZE]
    q_ref[...] = (_dot(mb, wq_ref[...]) * (HEAD_DIM ** -0.5)).astype(bf16)
    z_ref[...] = _dot(mb, wz_ref[...])
    gl_ref[...] = _dot(mb, wg_ref[...]) + bg_ref[...]


def _proj_prompt(x, mod, w_kvt, w_q, w_z, w_g, b_g, e, prow, ones, tr):
    bsz, t, d = x.shape
    kvw = w_kvt.shape[0] // 3
    hd = w_q.shape[1]
    gw = w_g.shape[1]
    nt = t // tr
    ppt = tr // PAGE_SIZE
    ka_rows = e.shape[1] + HEAD_DIM + prow.shape[0]
    va_rows = HEAD_DIM + ones.shape[0]
    const2 = lambda b, i: (0, 0)
    rows = lambda width: pl.BlockSpec((tr, width), lambda b, i: (b * nt + i, 0))
    tmin = pl.BlockSpec((1, kvw, tr), lambda b, i: (b, 0, i))
    kern = functools.partial(_proj_prompt_kernel, tr=tr)
    return pl.pallas_call(
        kern,
        grid=(bsz, nt),
        in_specs=[
            pl.BlockSpec((1, tr, d), lambda b, i: (b, i, 0)),
            pl.BlockSpec((1, 3, d), lambda b, i: (b, 0, 0)),
            pl.BlockSpec(w_kvt.shape, const2),
            pl.BlockSpec(w_q.shape, const2),
            pl.BlockSpec(w_z.shape, const2),
            pl.BlockSpec(w_g.shape, const2),
            pl.BlockSpec((1, gw), const2),
            pl.BlockSpec((1,) + e.shape[1:], lambda b, i: (i, 0, 0)),
            pl.BlockSpec(prow.shape, const2),
            pl.BlockSpec(ones.shape, const2),
        ],
        out_specs=[
            tmin, tmin, tmin,
            pl.BlockSpec((ppt, kvw, PAGE_SIZE), lambda b, i: (b * nt + i, 0, 0)),
            pl.BlockSpec((1, N_KV, 1, ka_rows, tr), lambda b, i: (b, 0, i, 0, 0)),
            pl.BlockSpec((1, N_KV, 1, va_rows, tr), lambda b, i: (b, 0, i, 0, 0)),
            pl.BlockSpec((1, ppt, kvw, PAGE_SIZE), lambda b, i: (b, i, 0, 0)),
            rows(hd), rows(hd), rows(gw),
        ],
        out_shape=[
            jax.ShapeDtypeStruct((bsz, kvw, t), f32),
            jax.ShapeDtypeStruct((bsz, kvw, t), f32),
            jax.ShapeDtypeStruct((bsz, kvw, t), f32),
            jax.ShapeDtypeStruct((bsz * t // PAGE_SIZE, kvw, PAGE_SIZE), f32),
            jax.ShapeDtypeStruct((bsz, N_KV, nt, ka_rows, tr), bf16),
            jax.ShapeDtypeStruct((bsz, N_KV, nt, va_rows, tr), bf16),
            jax.ShapeDtypeStruct((bsz, t // PAGE_SIZE, kvw, PAGE_SIZE), bf16),
            jax.ShapeDtypeStruct((bsz * t, hd), bf16),
            jax.ShapeDtypeStruct((bsz * t, hd), f32),
            jax.ShapeDtypeStruct((bsz * t, gw), f32),
        ],
        compiler_params=_params(("arbitrary", "arbitrary")),
        name="nsa_proj_prompt",
    )(x, mod, w_kvt, w_q, w_z, w_g, b_g, e, prow, ones)


def _compress_kernel(x_ref, pet_ref, w1_ref, b1_ref, w2_ref, b2_ref, kc_ref, vc_ref, *, npg):
    x_rows = x_ref.reshape(npg * PAGE_PITCH, x_ref.shape[2])

    def dim_rows(c, dd):
        return jnp.concatenate(
            [(x_rows[pl.ds((g * 2 + c) * HEAD_DIM + dd, npg, stride=PAGE_PITCH), :]
              + pet_ref[c, dd:dd + 1, :]).astype(bf16) for g in range(N_KV)], axis=0)

    outs = []
    for c in range(2):
        acc = None
        for dp in range(HEAD_DIM // 2):
            lhs = jnp.concatenate([dim_rows(c, 2 * dp), dim_rows(c, 2 * dp + 1)], axis=1)
            part = _dot(lhs, w1_ref[c, dp])
            acc = part if acc is None else acc + part
        hid = _silu(acc + b1_ref[c])
        outs.append(_dot(hid.astype(bf16), w2_ref[c]) + b2_ref[c])
    for g in range(N_KV):
        kc_ref[g] = outs[0][g * npg:(g + 1) * npg, :].astype(kc_ref.dtype)
        vc_ref[g] = outs[1][g * npg:(g + 1) * npg, :].astype(vc_ref.dtype)


def _compress(x2d, pet, w1, b1, w2, b2, npg, out_dtype):
    rows, width = x2d.shape
    n_pages = rows // (N_KV * GRP)
    npg = min(npg, n_pages)
    while n_pages % npg:
        npg -= 8
    ow = (PAGE_SIZE // CMP_BLOCK) * HEAD_DIM
    const3 = lambda i: (0, 0, 0)
    kern = functools.partial(_compress_kernel, npg=npg)
    out = pl.BlockSpec((N_KV, npg, ow), lambda i: (0, i, 0))
    return pl.pallas_call(
        kern,
        grid=(n_pages // npg,),
        in_specs=[
            pl.BlockSpec((npg, PAGE_PITCH, width), lambda i: (i, 0, 0)),
            pl.BlockSpec(pet.shape, const3),
            pl.BlockSpec(w1.shape, lambda i: (0, 0, 0, 0), pipeline_mode=pl.Buffered(1)),
            pl.BlockSpec(b1.shape, const3),
            pl.BlockSpec(w2.shape, const3),
            pl.BlockSpec(b2.shape, const3),
        ],
        out_specs=[out, out],
        out_shape=[jax.ShapeDtypeStruct((N_KV, n_pages, ow), out_dtype)] * 2,
        compiler_params=_params(("arbitrary",)),
        name="nsa_compress",
    )(x2d.reshape(n_pages, N_KV * GRP, width), pet, w1, b1, w2, b2)


def _block_ids(n_pages, lanes, axis):
    shape = (1, lanes) if axis == 1 else (lanes, 1)
    r = lax.broadcasted_iota(i32, shape, axis)
    bpp = PAGE_SIZE // CMP_BLOCK
    perm = jnp.where(r < n_pages, bpp * r, bpp * (r - n_pages) + 1)
    return jnp.where(r < bpp * n_pages, perm, r)


def _unpage(x):
    return jnp.concatenate([x[:, :HEAD_DIM], x[:, HEAD_DIM:]], axis=0)


def _cmp_branch(q4, kc, vc, slope, t_col, nq, jrow):
    s = _dot_nt(q4, kc)
    c_end = (jrow + 1) * CMP_BLOCK - 1
    dist = t_col.astype(f32) - c_end.astype(f32)
    s = s - slope * dist
    mask = c_end <= t_col
    s = jnp.where(mask, s, NEG)
    e = jnp.exp(s - jnp.max(s, axis=-1, keepdims=True))
    p = e / jnp.sum(e, axis=-1, keepdims=True)
    p = jnp.where(mask, p, 0.0)
    o = _dot(p.astype(bf16), vc)
    imp = p[0:nq]
    for h in range(1, HPG):
        imp = imp + p[h * nq:(h + 1) * nq]
    return o, imp


def _select_blocks(imp, tq_col, n_sb, jr):
    nq, lanes = imp.shape
    cb = tq_col // CMP_BLOCK
    forced = (jr == 0) | (jr == cb) | (jr == cb - 1)
    causal = jr <= cb
    score = jnp.where(forced, FORCED, jnp.where(causal, imp, -1.0))
    score = jnp.where(jr < n_sb, score, REMOVED)
    jf = jr.astype(f32)
    sel = jnp.zeros((nq, lanes), f32)
    for _ in range(min(N_SEL, n_sb)):
        mx = jnp.max(score, axis=-1, keepdims=True)
        idx = jnp.min(jnp.where(score == mx, jf, 1e9), axis=-1, keepdims=True)
        hit = jf == idx
        sel = jnp.where(hit, 1.0, sel)
        score = jnp.where(hit, REMOVED, score)
    return sel


def _flash_update(carry, q4, kt, vt, slope, valid, pos_rel):
    m_i, l_i, acc = carry
    s = _dot(q4, kt) + slope * pos_rel
    s = jnp.where(valid, s, NEG)
    m_new = jnp.maximum(m_i, jnp.max(s, axis=-1, keepdims=True))
    alpha = jnp.exp(m_i - m_new)
    p = jnp.exp(s - m_new)
    l_new = alpha * l_i + jnp.sum(p, axis=-1, keepdims=True)
    acc = alpha * acc + _dot_nt(p.astype(bf16), vt)
    return m_new, l_new, acc


def _block_mask(selb, jcol, pos, reps):
    onehot = (jcol == pos // CMP_BLOCK).astype(bf16)
    msk = _dot(selb, onehot)
    return jnp.concatenate([msk] * reps, axis=0) > 0.5


def _gate_mix(gl, o_cmp, o_sel, o_win, nq):
    gs = _sigmoid(gl)
    outs = []
    for h in range(HPG):
        sl = slice(h * nq, (h + 1) * nq)
        outs.append(gs[:, 3 * h:3 * h + 1] * o_cmp[sl] + gs[:, 3 * h + 1:3 * h + 2] * o_sel[sl]
                    + gs[:, 3 * h + 2:3 * h + 3] * o_win[sl])
    return jnp.concatenate(outs, axis=1)


def _init_carry(rows):
    return (jnp.full((rows, 1), NEG, f32), jnp.zeros((rows, 1), f32), jnp.zeros((rows, HEAD_DIM), f32))


def _select_blocks_t(impt, tq_row, n_sb, jcol, fillers=()):
    n_c, nq = impt.shape
    cb = tq_row // CMP_BLOCK
    forced = (jcol == 0) | (jcol == cb) | (jcol == cb - 1)
    causal = jcol <= cb
    score = jnp.where(forced, REMOVED, jnp.where(causal, impt, -1.0))
    score = jnp.where(jcol < n_sb, score, REMOVED)
    jf = jcol.astype(f32)
    sel = jnp.where(forced & (jcol < n_sb), 1.0, 0.0) + jnp.zeros((n_c, nq), f32)
    rounds = max(min(N_SEL, n_sb) - 3, 0)
    fillers = list(fillers)
    every = max(rounds // max(len(fillers), 1), 1)
    for it in range(rounds):
        mx = jnp.max(score, axis=0, keepdims=True)
        idx = jnp.min(jnp.where(score == mx, jf, 1e9), axis=0, keepdims=True)
        hit = jf == idx
        sel = jnp.where(hit, 1.0, sel)
        score = jnp.where(hit, REMOVED, score)
        if fillers and it % every == 0:
            fillers.pop(0)()
    for f in fillers:
        f()
    return sel


def _nsa_prompt_kernel(q_ref, gl_ref, kc_ref, vc_ref, kaug_ref, vaug_ref, kwin_ref, scol_ref, srow_ref, coef_ref,
                       wbias_ref, cbias_ref, eye_ref, tilemap_ref, o_ref,
                       s0_s, s1_s, p0_s, p1_s, a0_s, a1_s, m_s, acc_s, tiles_s, *, t_len, tq, tk, tkw):
    qi = pl.program_id(2)
    t0 = qi * tq
    rows = HPG * tq
    q = q_ref[0]
    q4 = jnp.concatenate([q[:, h * HEAD_DIM:(h + 1) * HEAD_DIM] for h in range(HPG)], axis=0)
    slope = scol_ref[0]
    t_col = t0 + lax.broadcasted_iota(i32, (rows, 1), 0) % tq

    n_c = t_len // CMP_BLOCK
    n_pages = t_len // PAGE_SIZE
    jrow = _block_ids(n_pages, n_c, 1)
    jcol = _block_ids(n_pages, n_c, 0)
    kc = _unpage(kc_ref[0]).astype(bf16)
    vc = _unpage(vc_ref[0]).astype(bf16)
    o_cmp_box = []

    def cmp_output():
        o_cmp_box.append(_cmp_branch(q4, kc, vc, slope, t_col, tq, jrow)[0])

    def init_state():
        m_s[...] = jnp.full((rows, 1), NEG, f32)
        acc_s[...] = jnp.zeros(acc_s.shape, f32)
        p1_s[...] = jnp.zeros(p1_s.shape, bf16)
        a1_s[...] = jnp.ones((rows, 1), f32)

    nslots = (WINDOW + tq) // tkw
    base = (t0 - WINDOW) // tkw
    kts, vts, pens = [], [], []
    for m in range(nslots):
        kw = base + m
        kvw = kwin_ref[0, jnp.maximum(kw, 0)]
        kts.append(kvw[:HEAD_DIM])
        vts.append(kvw[HEAD_DIM:])
        pens.append(jnp.broadcast_to(jnp.where(kw >= 0, 0.0, NEG).astype(f32), (1, tkw)))
    sw = _dot(q4, jnp.concatenate(kts, axis=1))
    pen = jnp.concatenate(pens, axis=1)
    pw_h, lw_h = [], []

    def window_softmax(h):
        def run():
            s_h = sw[h * tq:(h + 1) * tq] + wbias_ref[0, h * tq:(h + 1) * tq] + pen
            p_h = jnp.exp(s_h - jnp.max(s_h, axis=-1, keepdims=True))
            pw_h.append(p_h.astype(bf16))
            lw_h.append(jnp.sum(p_h, axis=-1, keepdims=True))
        return run

    t_row = t0 + lax.broadcasted_iota(i32, (1, rows), 1) % tq
    c_end = (jcol + 1) * CMP_BLOCK - 1
    st = _dot_nt(kc, q4) - srow_ref[0] * (t_row.astype(f32) - c_end.astype(f32))
    maskt = c_end <= t_row
    st = jnp.where(maskt, st, NEG)
    et = jnp.exp(st - jnp.max(st, axis=0, keepdims=True))
    pt = et / jnp.sum(et, axis=0, keepdims=True)
    pt = jnp.where(maskt, pt, 0.0)
    impt = pt[:, 0:tq]
    for h in range(1, HPG):
        impt = impt + pt[:, h * tq:(h + 1) * tq]
    selt = _select_blocks_t(impt, t0 + lax.broadcasted_iota(i32, (1, tq), 1), n_c, jcol,
                            fillers=[window_softmax(h) for h in range(HPG)] + [cmp_output, init_state])
    o_cmp = o_cmp_box[0]
    sel = _dot_nt(eye_ref[...], selt.astype(bf16))
    selbias = ((sel - 1.0) * -NEG).astype(bf16)
    q_aug = jnp.concatenate([jnp.concatenate([selbias] * HPG, axis=0), q4, coef_ref[0]], axis=1)


    def scores_into(dst, kt):
        dst[...] = _dot(q_aug, kaug_ref[0, 0, kt])

    def softmax_into(s, kt, p_dst, a_dst):
        c = slope * (kt * tk - t0).astype(f32)
        m_i = m_s[...]
        m_new = jnp.maximum(m_i, jnp.max(s, axis=-1, keepdims=True) + c)
        p_dst[...] = jnp.exp(s - (m_new - c)).astype(bf16)
        a_dst[...] = jnp.exp(m_i - m_new)
        m_s[...] = m_new

    def values_from(p_src, a_src, kt):
        acc_s[...] = a_src[...] * acc_s[...] + _dot_nt(p_src[...], vaug_ref[0, 0, kt])

    kd = t0 // tk
    tile_hits = jnp.max(_dot(tilemap_ref[...], selt.astype(bf16)), axis=1, keepdims=True)
    n_act = jnp.int32(0)
    tiles_s[0] = 0
    for kt in range(t_len // tk):
        tiles_s[n_act] = kt
        hit = (jnp.max(tile_hits[kt:kt + 1, :]) > 0.5) & (kt < kd)
        n_act = n_act + hit.astype(i32)
    npairs = n_act // 2
    last_piped = jnp.maximum(2 * npairs - 1, 0)
    scores_into(s0_s, tiles_s[0])

    def pair(j, carry):
        ia = 2 * j
        ka = tiles_s[ia]
        kb = tiles_s[ia + 1]
        values_from(p1_s, a1_s, tiles_s[jnp.maximum(ia - 1, 0)])
        scores_into(s1_s, kb)
        softmax_into(s0_s[...], ka, p0_s, a0_s)
        values_from(p0_s, a0_s, ka)
        scores_into(s0_s, tiles_s[jnp.minimum(ia + 2, last_piped)])
        softmax_into(s1_s[...], kb, p1_s, a1_s)
        return carry

    lax.fori_loop(0, npairs, pair, 0)
    values_from(p1_s, a1_s, tiles_s[last_piped])

    def tile_plain(kt, bias):
        s = _dot(q_aug, kaug_ref[0, 0, kt])
        if bias is not None:
            s = s + bias
        softmax_into(s, kt, p0_s, a0_s)
        values_from(p0_s, a0_s, kt)

    @pl.when(n_act % 2 == 1)
    def _():
        tile_plain(tiles_s[jnp.maximum(n_act - 1, 0)], None)

    s_d = _dot(q_aug, kaug_ref[0, 0, kd]) + jnp.concatenate([cbias_ref[(t0 % tk) // tq]] * HPG, axis=0)
    o_win = (_dot_nt(jnp.concatenate(pw_h, axis=0), jnp.concatenate(vts, axis=1))
             / jnp.concatenate(lw_h, axis=0))
    softmax_into(s_d, kd, p0_s, a0_s)
    values_from(p0_s, a0_s, kd)
    acc = acc_s[...]
    o_sel = acc[:, :HEAD_DIM] / acc[:, HEAD_DIM:HEAD_DIM + 1]

    o_ref[0] = _gate_mix(gl_ref[0], o_cmp, o_sel, o_win, tq)


def _prompt_consts(t_len, tq, tk, tkw):
    n_c = t_len // CMP_BLOCK
    n_pages = t_len // PAGE_SIZE
    rows = HPG * tq
    r = np.arange(n_c)
    bpp = PAGE_SIZE // CMP_BLOCK
    ids = np.where(r < n_pages, bpp * r, bpp * (r - n_pages) + 1)
    onehot = ids[:, None] == (np.arange(t_len) // CMP_BLOCK)[None, :]
    e = onehot.reshape(n_c, t_len // tk, tk).transpose(1, 0, 2)
    lane = np.arange(tk)
    prow = np.zeros((HEAD_DIM, tk), np.float32)
    prow[0:3] = lane // 256
    prow[3:6] = lane % 256
    slopes = (2.0 ** (-8.0 * np.arange(1, N_HEADS + 1) / N_HEADS)).astype(np.float32)
    srow = np.repeat(slopes.reshape(N_KV, HPG), tq, axis=1)
    s1 = srow.astype(jnp.bfloat16).astype(np.float32)
    s2 = (srow - s1).astype(jnp.bfloat16).astype(np.float32)
    s3 = (srow - s1 - s2).astype(jnp.bfloat16).astype(np.float32)
    coef = np.zeros((N_KV, rows, HEAD_DIM), np.float32)
    for k, piece in enumerate((s1, s2, s3)):
        coef[:, :, k] = 256.0 * piece
        coef[:, :, 3 + k] = piece
    a = np.tile(np.arange(tq), HPG)[:, None]
    nslots = (WINDOW + tq) // tkw
    dpos = a + WINDOW - np.arange(nslots * tkw)[None, :]
    band = np.where((dpos >= 0) & (dpos <= WINDOW), 0.0, NEG)
    wbias = -srow[:, :, None] * dpos[None] + band[None]
    off = np.arange(tk // tq)[:, None, None] * tq
    cbias = np.where(lane[None, None, :] <= off + np.arange(tq)[None, :, None], 0.0, NEG)
    n_tiles = t_len // tk
    tilemap = np.arange(-(-n_tiles // 16) * 16)[:, None] == (ids // (tk // CMP_BLOCK))[None, :]
    return dict(
        tilemap=jnp.asarray(tilemap, bf16),
        scol=jnp.asarray(srow.reshape(N_KV, rows, 1), f32), srow=jnp.asarray(srow.reshape(N_KV, 1, rows), f32),
        coef=jnp.asarray(coef, bf16), e=jnp.asarray(e, bf16), prow=jnp.asarray(prow, bf16),
        ones=jnp.ones((16, tk), bf16), wbias=jnp.asarray(wbias, f32), cbias=jnp.asarray(cbias, f32),
        eye=jnp.eye(tq, dtype=bf16))


def _nsa_prompt(q, gl, kc, vc, kaug, vaug, kwin, c, tq):
    bsz, t_len, _ = q.shape
    tk = kaug.shape[-1]
    tkw = kwin.shape[-1]
    npb = t_len // PAGE_SIZE
    rows = HPG * tq
    kern = functools.partial(_nsa_prompt_kernel, t_len=t_len, tq=tq, tk=tk, tkw=tkw)
    augspec = lambda arr: pl.BlockSpec((1, 1) + arr.shape[2:], lambda b, g, i: (b, g, 0, 0, 0))
    cspec = pl.BlockSpec((1, npb, kc.shape[-1]), lambda b, g, i: (g, b, 0))
    perg = lambda arr: pl.BlockSpec((1,) + arr.shape[1:], lambda b, g, i: (g,) + (0,) * (arr.ndim - 1))
    whole = lambda arr: pl.BlockSpec(arr.shape, lambda b, g, i: (0,) * arr.ndim)
    return pl.pallas_call(
        kern,
        grid=(bsz, N_KV, t_len // tq),
        in_specs=[
            pl.BlockSpec((1, tq, QGRP), lambda b, g, i: (b, i, g)),
            pl.BlockSpec((1, tq, 128), lambda b, g, i: (b, i, g)),
            cspec, cspec,
            augspec(kaug), augspec(vaug),
            pl.BlockSpec((1, t_len // tkw, GRP, tkw), lambda b, g, i: (b, 0, g, 0)),
            perg(c["scol"]), perg(c["srow"]), perg(c["coef"]),
            perg(c["wbias"]), whole(c["cbias"]), whole(c["eye"]), whole(c["tilemap"]),
        ],
        out_specs=pl.BlockSpec((1, tq, QGRP), lambda b, g, i: (b, i, g)),
        out_shape=jax.ShapeDtypeStruct((bsz, t_len, N_KV * QGRP), f32),
        scratch_shapes=[
            pltpu.VMEM((rows, tk), f32), pltpu.VMEM((rows, tk), f32),
            pltpu.VMEM((rows, tk), bf16), pltpu.VMEM((rows, tk), bf16),
            pltpu.VMEM((rows, 1), f32), pltpu.VMEM((rows, 1), f32),
            pltpu.VMEM((rows, 1), f32), pltpu.VMEM((rows, vaug.shape[3]), f32),
            pltpu.SMEM((t_len // tk + 8,), i32),
        ],
        compiler_params=_params(("arbitrary", "arbitrary", "arbitrary")),
        name="nsa_prompt_attn",
    )(q, gl, kc, vc, kaug, vaug, kwin, c["scol"], c["srow"], c["coef"], c["wbias"], c["cbias"], c["eye"],
      c["tilemap"])


def _nsa_sample_kernel(pt_ref, q_ref, gl_ref, kcall_ref, vcall_ref, pid_ref, newsel_ref, swin_ref, nwin_ref,
                       slope_ref, *rest, s_len, past_len, pp, n_groups, lanes):
    page_refs = rest[:pp]
    o_ref = rest[pp]
    sel_s, ocmp_s, qbd_s, m_s, l_s, acc_s = rest[pp + 1:]
    j = pl.program_id(1)
    rows = HPG * s_len
    n_c = past_len // CMP_BLOCK
    npb = past_len // PAGE_SIZE
    jrow = _block_ids(npb, lanes, 1)
    jcol = _block_ids(npb, lanes, 0)
    q = q_ref[0].astype(f32)
    t_col = past_len + lax.broadcasted_iota(i32, (rows, 1), 0) % s_len
    tq_col = past_len + lax.broadcasted_iota(i32, (s_len, 1), 0)

    def q_of(g):
        return jnp.concatenate(
            [q[:, (g * HPG + h) * HEAD_DIM:(g * HPG + h + 1) * HEAD_DIM] for h in range(HPG)],
            axis=0).astype(bf16)

    groups = range(N_KV)
    all_rows = N_KV * rows
    slope_all = jnp.concatenate([slope_ref[g] for g in groups], axis=0)
    t_all = past_len + lax.broadcasted_iota(i32, (all_rows, 1), 0) % s_len

    @pl.when(j == 0)
    def _():
        n_all = kcall_ref.shape[1]
        gather = (lax.broadcasted_iota(i32, (npb, n_all), 1) == pid_ref[0]).astype(bf16)
        zeros = jnp.zeros((rows, HEAD_DIM), bf16)
        imps = []
        for g in groups:
            kc = _unpage(_dot(gather, kcall_ref[g])).astype(bf16)
            vc = _unpage(_dot(gather, vcall_ref[g])).astype(bf16)
            q4 = q_of(g)
            o_cmp, imp = _cmp_branch(q4, kc, vc, slope_ref[g], t_col, s_len, jrow[:, :n_c])
            imps.append(jnp.concatenate([imp, jnp.zeros((s_len, lanes - n_c), f32)], axis=1))
            ocmp_s[g] = o_cmp
            qbd_s[g * rows:(g + 1) * rows] = jnp.concatenate([q4 if gg == g else zeros for gg in groups], axis=1)
        sel_s[...] = _select_blocks(jnp.concatenate(imps, axis=0), jnp.concatenate([tq_col] * N_KV, axis=0),
                                    n_c + 1, jrow)
        m_s[...] = jnp.full((all_rows, 1), NEG, f32)
        l_s[...] = jnp.zeros((all_rows, 1), f32)
        acc_s[...] = jnp.zeros(acc_s.shape, f32)

    def stack(keys):
        kt = jnp.concatenate([keys[g * GRP:g * GRP + HEAD_DIM] for g in groups], axis=0).astype(bf16)
        vt = jnp.concatenate([keys[g * GRP + HEAD_DIM:(g + 1) * GRP] for g in groups], axis=0).astype(bf16)
        return kt, vt

    def attend(keys, pos):
        onehot = (jcol == pos // CMP_BLOCK).astype(bf16)
        msk = _dot(sel_s[...].astype(bf16), onehot)
        msk = jnp.concatenate([msk[g * s_len:(g + 1) * s_len] for g in groups for _ in range(HPG)], axis=0)
        valid = (msk > 0.5) & (pos <= t_all)
        kt, vt = stack(keys)
        m_new, l_new, acc = _flash_update((m_s[...], l_s[...], acc_s[...]), qbd_s[...], kt, vt, slope_all,
                                          valid, (pos - past_len).astype(f32))
        m_s[...] = m_new
        l_s[...] = l_new
        acc_s[...] = acc

    nk = pp * PAGE_SIZE
    keys = jnp.concatenate([r[0] for r in page_refs], axis=1)
    attend(keys, j * nk + lax.broadcasted_iota(i32, (1, nk), 1))

    @pl.when(j == n_groups - 1)
    def _():
        new_keys = newsel_ref[0]
        attend(new_keys, past_len + lax.broadcasted_iota(i32, (1, new_keys.shape[1]), 1))
        o_sel = acc_s[...] / l_s[...]
        carry = (jnp.full((all_rows, 1), NEG, f32), jnp.zeros((all_rows, 1), f32),
                 jnp.zeros((all_rows, N_KV * HEAD_DIM), f32))
        for wkeys, wbase in ((swin_ref[0], past_len - swin_ref.shape[2]), (nwin_ref[0], past_len)):
            wpos = wbase + lax.broadcasted_iota(i32, (1, wkeys.shape[1]), 1)
            dpos = t_all - wpos
            valid = (dpos >= 0) & (dpos <= WINDOW)
            kt, vt = stack(wkeys)
            carry = _flash_update(carry, qbd_s[...], kt, vt, slope_all, valid, (wpos - past_len).astype(f32))
        o_win = carry[2] / carry[1]
        own = lambda o, g: o[g * rows:(g + 1) * rows, g * HEAD_DIM:(g + 1) * HEAD_DIM]
        gl = gl_ref[0]
        o_ref[0] = jnp.concatenate(
            [_gate_mix(gl[:, g * 128:(g + 1) * 128], ocmp_s[g], own(o_sel, g), own(o_win, g), s_len)
             for g in groups], axis=1)


def _nsa_sample(page_table, q, gl, kc_all, vc_all, pids, new_sel, state_win, new_win, slopes, cache_sel, pp):
    bsz, s_len, hd = q.shape
    n_pages = page_table.shape[1]
    past_len = n_pages * PAGE_SIZE
    pp = min(pp, n_pages)
    n_groups = n_pages // pp
    n_c = past_len // CMP_BLOCK
    lanes = -(-(n_c + 2) // 128) * 128
    rows = HPG * s_len
    kvw = cache_sel.shape[1]
    perb = lambda b, j, pt: (b, 0, 0)

    def page_spec(i):
        return pl.BlockSpec((1, kvw, PAGE_SIZE), lambda b, j, pt: (pt[b, j * pp + i], 0, 0))

    kern = functools.partial(_nsa_sample_kernel, s_len=s_len, past_len=past_len, pp=pp,
                             n_groups=n_groups, lanes=lanes)
    grid_spec = pltpu.PrefetchScalarGridSpec(
        num_scalar_prefetch=1,
        grid=(bsz, n_groups),
        in_specs=[
            pl.BlockSpec((1, s_len, hd), perb),
            pl.BlockSpec((1, s_len, gl.shape[-1]), perb),
            pl.BlockSpec(kc_all.shape, lambda b, j, pt: (0, 0, 0)),
            pl.BlockSpec(vc_all.shape, lambda b, j, pt: (0, 0, 0)),
            pl.BlockSpec((1, n_pages, 1), perb),
            pl.BlockSpec((1,) + new_sel.shape[1:], perb),
            pl.BlockSpec((1,) + state_win.shape[1:], perb),
            pl.BlockSpec((1,) + new_win.shape[1:], perb),
            pl.BlockSpec(slopes.shape, lambda b, j, pt: (0, 0, 0)),
        ] + [page_spec(i) for i in range(pp)],
        out_specs=pl.BlockSpec((1, s_len, hd), perb),
        scratch_shapes=[
            pltpu.VMEM((N_KV * s_len, lanes), f32),
            pltpu.VMEM((N_KV, rows, HEAD_DIM), f32),
            pltpu.VMEM((N_KV * rows, N_KV * HEAD_DIM), bf16),
            pltpu.VMEM((N_KV * rows, 1), f32),
            pltpu.VMEM((N_KV * rows, 1), f32),
            pltpu.VMEM((N_KV * rows, N_KV * HEAD_DIM), f32),
        ],
    )
    return pl.pallas_call(
        kern,
        grid_spec=grid_spec,
        out_shape=jax.ShapeDtypeStruct((bsz, s_len, hd), f32),
        compiler_params=_params(("arbitrary", "arbitrary")),
        name="nsa_sample_attn",
    )(page_table, q, gl, kc_all, vc_all, pids, new_sel, state_win, new_win, slopes, *([cache_sel] * pp))


def _out_kernel(o_ref, z_ref, x_ref, mod_ref, w_ref, lng_ref, lnb_ref, xo_ref, *, nb, tr, d):
    y = _dot((o_ref[...] * _silu(z_ref[...])).astype(bf16), w_ref[...]).reshape(nb, tr, d)
    gate = mod_ref[...][:, 2:3, :]
    xo_ref[...] = _layer_norm(DN_ALPHA * x_ref[...] + (1.0 + gate) * y, lng_ref[...], lnb_ref[...])


def _out_proj(o, z, x, mod, w_out, ln_g, ln_b, nb, tr):
    bsz, t, d = x.shape
    hd = o.shape[-1]
    nt = t // tr
    const2 = lambda b, i: (0, 0)
    kern = functools.partial(_out_kernel, nb=nb, tr=tr, d=d)
    return pl.pallas_call(
        kern,
        grid=(bsz // nb, nt),
        in_specs=[
            pl.BlockSpec((nb * tr, hd), lambda b, i: (b * nt + i, 0)),
            pl.BlockSpec((nb * tr, hd), lambda b, i: (b * nt + i, 0)),
            pl.BlockSpec((nb, tr, d), lambda b, i: (b, i, 0)),
            pl.BlockSpec((nb, 3, d), lambda b, i: (b, 0, 0)),
            pl.BlockSpec(w_out.shape, const2),
            pl.BlockSpec((1, d), const2),
            pl.BlockSpec((1, d), const2),
        ],
        out_specs=pl.BlockSpec((nb, tr, d), lambda b, i: (b, i, 0)),
        out_shape=jax.ShapeDtypeStruct((bsz, t, d), f32),
        compiler_params=_params(("arbitrary", "arbitrary")),
        name="nsa_out",
    )(o, z, x, mod, w_out, ln_g, ln_b)


def _head_slopes(nq):
    s = 2.0 ** (-8.0 * np.arange(1, N_HEADS + 1) / N_HEADS)
    s = np.repeat(s.reshape(N_KV, HPG), nq, axis=1)
    return jnp.asarray(s.reshape(N_KV, HPG * nq, 1), f32)


def kernel(x_prompt, x_sample, c_prompt, c_sample, state_h, state_conv, cache_cmp, cache_sel, state_win, page_table, w_ada, b_ada, ln_g, ln_b, w_in_a, conv_w_a, conv_b_a, w_r_a, b_r_a, w_i_a, b_i_a, lam_a, w_out_a, w_kv, phi_pe, w_phi1, b_phi1, w_phi2, b_phi2, w_in_b, b_gate_b, w_out_b):
    bp, t_len, d = x_prompt.shape
    bs, s_len, _ = x_sample.shape
    n_pages = page_table.shape[1]
    past_len = n_pages * PAGE_SIZE
    hd = N_HEADS * HEAD_DIM
    kvw = N_KV * GRP
    assert s_len <= CMP_BLOCK and t_len % 128 == 0 and w_ada.shape[0] == DEPTH == 2

    n_c = bp + bs
    pad = -n_c % 8
    c_all = jnp.concatenate([c_prompt, c_sample, jnp.zeros((pad, d), f32)], axis=0)
    mods = _ada_mod(c_all, w_ada, b_ada).reshape(DEPTH, n_c + pad, 3, d)
    mod_p = mods[:, :bp]
    mod_s = mods[:, bp:n_c]

    rg = d // N_RG_BLOCKS
    w_in0 = w_in_a[0].astype(bf16)
    w_gate = jnp.concatenate([w_r_a[0], w_i_a[0]], axis=-1).astype(bf16)
    w_out0 = w_out_a[0].astype(bf16)
    row = lambda v: v.reshape(1, -1)
    rg_args = (w_in0, conv_w_a[0], row(conv_b_a[0]), w_gate, row(b_r_a[0]), row(b_i_a[0]), row(lam_a[0]),
               w_out0, row(ln_g[0]), row(ln_b[0]))
    xp1, h_p, conv_p = _rglru_prompt(x_prompt, mod_p[0], jnp.zeros((bp, 1, d), f32),
                                     jnp.zeros((bp, CONV_W - 1, d), f32), *rg_args, tc=256)
    xs1_tm, h_s, conv_s_tm = _rglru_sample(
        x_sample.transpose(1, 0, 2), mod_s[0].transpose(1, 0, 2), state_h[0],
        state_conv[0].transpose(1, 0, 2), *rg_args)
    xs1 = xs1_tm.transpose(1, 0, 2)
    new_h_p = h_p.reshape(1, bp, d)
    new_conv_p = conv_p.reshape(1, bp, CONV_W - 1, d)
    new_h_s = h_s.reshape(1, bs, d)
    new_conv_s = conv_s_tm.transpose(1, 0, 2).reshape(1, bs, CONV_W - 1, d)

    w_b = w_in_b[0]
    w_q = w_b[:, :hd].astype(bf16)
    w_z = w_b[:, hd:2 * hd].astype(bf16)
    gpg = HPG * 3
    w_g = jnp.pad(w_b[:, 2 * hd:].reshape(d, N_KV, gpg), ((0, 0), (0, 0), (0, 128 - gpg)))
    w_g = w_g.reshape(d, N_KV * 128).astype(bf16)
    b_g = jnp.pad(b_gate_b[0].reshape(N_KV, gpg), ((0, 0), (0, 128 - gpg))).reshape(1, N_KV * 128)
    tmin = lambda v: v.transpose(0, 2, 3, 4, 1)
    tq = min(128, t_len)
    tk = min(512, t_len)
    consts = _prompt_consts(t_len, tq, tk, PAGE_SIZE)
    cmpt_p, selt_p, wint_p, cmp_pages, kaug_p, vaug_p, wint_tiles, q_p, z_p, gl_p = _proj_prompt(
        xp1, mod_p[1], w_kv.T.astype(bf16), w_q, w_z, w_g, b_g, consts["e"], consts["prow"], consts["ones"],
        tr=tk)
    cmp_s, sel_s, win_s, selb_s, winb_s, q_s, z_s, gl_s = _proj(
        xs1, mod_s[1], w_kv.astype(bf16), w_q, w_z, w_g, b_g, nb=bs, tr=s_len)
    per_p = lambda v: v.reshape(bp, t_len, v.shape[-1])
    per_s = lambda v: v.reshape(bs, s_len, v.shape[-1])

    bpp = PAGE_SIZE // CMP_BLOCK
    dphi = w_phi2.shape[1]
    diag2 = lambda w: jnp.concatenate([jnp.concatenate([w, jnp.zeros_like(w)], axis=-1),
                                       jnp.concatenate([jnp.zeros_like(w), w], axis=-1)], axis=-2)
    pet = jnp.tile(phi_pe.transpose(1, 2, 0), (1, 1, bpp))
    w1 = diag2(w_phi1.transpose(0, 2, 1, 3)).astype(bf16)
    w1 = w1.reshape(2, HEAD_DIM // 2, 2 * PAGE_SIZE, bpp * dphi)
    b1 = jnp.tile(b_phi1, (1, bpp)).reshape(2, 1, bpp * dphi)
    w2 = diag2(w_phi2).astype(bf16)
    b2 = jnp.tile(b_phi2, (1, bpp)).reshape(2, 1, bpp * HEAD_DIM)
    kc_p, vc_p = _compress(cmp_pages.reshape(-1, PAGE_SIZE), pet, w1, b1, w2, b2, npg=64, out_dtype=f32)
    kc_all, vc_all = _compress(tmin(cache_cmp).reshape(-1, PAGE_SIZE), pet, w1, b1, w2, b2,
                               npg=64, out_dtype=bf16)

    o_p = _nsa_prompt(per_p(q_p), per_p(gl_p), kc_p, vc_p, kaug_p, vaug_p, wint_tiles, consts, tq=tq)
    tpad = lambda v: jnp.pad(per_s(v), ((0, 0), (0, PAGE_SIZE - s_len), (0, 0))).transpose(0, 2, 1)
    o_s = _nsa_sample(page_table, per_s(q_s), per_s(gl_s), kc_all, vc_all, page_table.reshape(bs, n_pages, 1),
                      tpad(selb_s), tmin(state_win).reshape(bs, kvw, -1), tpad(winb_s), _head_slopes(s_len),
                      tmin(cache_sel).reshape(-1, kvw, PAGE_SIZE), pp=16)

    w_out1 = w_out_b[0].astype(bf16)
    y_p = _out_proj(o_p.reshape(bp * t_len, hd), z_p, xp1, mod_p[1], w_out1, row(ln_g[1]), row(ln_b[1]),
                    nb=1, tr=min(512, t_len))
    y_s = _out_proj(o_s.reshape(bs * s_len, hd), z_s, xs1, mod_s[1], w_out1, row(ln_g[1]), row(ln_b[1]),
                    nb=bs, tr=s_len)

    kv5 = lambda v: v.reshape(bs, s_len, N_KV, 2, HEAD_DIM)
    kv5t = lambda v: v.reshape(bp, N_KV, 2, HEAD_DIM, -1).transpose(0, 4, 1, 2, 3)
    wb = state_win.shape[1]
    new_win_p = kv5t(wint_p[:, :, -min(WINDOW, t_len):])
    new_win_s = jnp.concatenate([state_win, kv5(win_s)], axis=1)[:, -wb:]
    return (y_p, y_s, kv5t(cmpt_p), kv5t(selt_p), new_win_p, new_h_p, new_conv_p,
            kv5(cmp_s), kv5(sel_s), new_win_s, new_h_s, new_conv_s)
```

```python
import functools
import math

import numpy as np
import jax
import jax.numpy as jnp
from jax import lax
from jax.experimental import pallas as pl
from jax.experimental.pallas import tpu as pltpu

f32 = jnp.float32
bf16 = jnp.bfloat16
i32 = jnp.int32

DEPTH = 2
N_RG_BLOCKS = 8
CONV_W = 4
RG_C = 8.0
N_HEADS = 16
HEAD_DIM = 64
N_KV = 4
HPG = N_HEADS // N_KV
CMP_BLOCK = 64
N_SEL = 16
WINDOW = 512
PAGE_SIZE = 128
DN_ALPHA = (2.0 * DEPTH) ** 0.25
LN_EPS = 1e-5
NEG = -1e30
FORCED = 1e6
REMOVED = -3e38
GRP = 2 * HEAD_DIM
QGRP = HPG * HEAD_DIM
PAGE_PITCH = N_KV * GRP + 8
VMEM_LIMIT_BYTES = 56 * 1024 * 1024


def _params(sem):
    return pltpu.CompilerParams(dimension_semantics=sem, vmem_limit_bytes=VMEM_LIMIT_BYTES)


def _dot(a, b):
    return jnp.dot(a, b, preferred_element_type=f32)


def _dot_nt(a, b):
    return lax.dot_general(a, b, (((1,), (1,)), ((), ())), preferred_element_type=f32)


def _sigmoid(x):
    return 1.0 / (1.0 + jnp.exp(-x))


def _silu(x):
    return x * _sigmoid(x)


def _log1p(e):
    u = 1.0 + e
    dlt = u - 1.0
    return jnp.where(dlt == 0.0, e, jnp.log(u) * (e / jnp.where(dlt == 0.0, 1.0, dlt)))


def _layer_norm(x, g, b):
    mu = jnp.mean(x, axis=-1, keepdims=True)
    xc = x - mu
    var = jnp.mean(xc * xc, axis=-1, keepdims=True)
    return xc * lax.rsqrt(var + LN_EPS) * g + b


def _ada_kernel(c_ref, w_ref, b_ref, o_ref):
    a = _silu(c_ref[...])
    o_ref[0] = jnp.dot(a, w_ref[0], preferred_element_type=f32,
                       precision=lax.Precision.HIGHEST) + b_ref[0]


def _ada_mod(c_all, w_ada, b_ada):
    rows, d = c_all.shape
    depth = w_ada.shape[0]
    return pl.pallas_call(
        _ada_kernel,
        grid=(depth, 3),
        in_specs=[
            pl.BlockSpec((rows, d), lambda l, n: (0, 0)),
            pl.BlockSpec((1, d, d), lambda l, n: (l, 0, n)),
            pl.BlockSpec((1, 1, d), lambda l, n: (l, 0, n)),
        ],
        out_specs=pl.BlockSpec((1, rows, d), lambda l, n: (l, 0, n)),
        out_shape=jax.ShapeDtypeStruct((depth, rows, 3 * d), f32),
        compiler_params=_params(("arbitrary", "arbitrary")),
        name="ada_mod",
    )(c_all, w_ada, b_ada.reshape(depth, 1, 3 * d))


def _rglru_gates(xc, wg_ref, br, bi, lam):
    xcb = xc.astype(bf16)
    rg = xc.shape[1] // N_RG_BLOCKS
    rs, is_ = [], []
    for n in range(N_RG_BLOCKS):
        g = _dot(xcb[:, n * rg:(n + 1) * rg], wg_ref[n])
        rs.append(g[:, :rg])
        is_.append(g[:, rg:])
    r = _sigmoid(jnp.concatenate(rs, axis=1) + br)
    i = _sigmoid(jnp.concatenate(is_, axis=1) + bi)
    nl = -lam
    softplus = jnp.maximum(nl, 0.0) + _log1p(jnp.exp(-jnp.abs(nl)))
    log_a = (-RG_C * softplus) * r
    a = jnp.exp(log_a)
    gain = jnp.sqrt(jnp.maximum(-jnp.tanh(log_a) * (a * a + 1.0), 0.0))
    b = gain * i * xc
    return a, b


def _rglru_prompt_kernel(x_ref, mod_ref, h0_ref, c0_ref, win_ref, cw_ref, cb_ref, wg_ref, br_ref,
                         bi_ref, lam_ref, wout_ref, lng_ref, lnb_ref,
                         xo_ref, hl_ref, cl_ref, xbuf, a_s, b_s, hs_s, hc, *, tc, d):
    t = pl.program_id(1)

    @pl.when(t == 0)
    def _():
        xbuf[0:8, :] = jnp.zeros((8, d), f32)
        xbuf[8 - (CONV_W - 1):8, :] = c0_ref[0]
        hc[...] = jnp.broadcast_to(h0_ref[0], (8, d))

    x = x_ref[0]
    mod = mod_ref[0]
    shift, scale, gate = mod[0:1], mod[1:2], mod[2:3]
    m = x * (1.0 + scale) + shift
    u = _dot(m.astype(bf16), win_ref[...])
    xb = u[:, :d]
    zg = u[:, d:]
    xbuf[8:8 + tc, :] = xb
    base = 8 - (CONV_W - 1)
    xc = cb_ref[...] + xbuf[base:base + tc, :] * cw_ref[0:1, :]
    for k in range(1, CONV_W):
        xc = xc + xbuf[base + k:base + k + tc, :] * cw_ref[k:k + 1, :]
    tail = xbuf[8 + tc - (CONV_W - 1):8 + tc, :]
    xbuf[base:8, :] = tail
    cl_ref[0] = tail

    a, b = _rglru_gates(xc, wg_ref, br_ref[...], bi_ref[...], lam_ref[...])

    a = a.reshape(tc // 8, 8, d)
    b = b.reshape(tc // 8, 8, d)
    rowi = lax.broadcasted_iota(i32, (1, 8, 1), 1)
    for s in (1, 2, 4):
        ok = rowi >= s
        a_sh = pltpu.roll(a, s, 1)
        b_sh = pltpu.roll(b, s, 1)
        b = jnp.where(ok, a * b_sh + b, b)
        a = jnp.where(ok, a * a_sh, a)
    a_s[...] = a.reshape(tc, d)
    b_s[...] = b.reshape(tc, d)

    def tile_step(j, hprev):
        r0 = pl.multiple_of(j * 8, 8)
        ht = a_s[pl.ds(r0, 8), :] * hprev + b_s[pl.ds(r0, 8), :]
        hs_s[pl.ds(r0, 8), :] = ht
        return jnp.broadcast_to(ht[7:8, :], (8, d))

    hlast = lax.fori_loop(0, tc // 8, tile_step, hc[...])
    hc[...] = hlast
    hl_ref[0] = hlast[0:1, :]

    y = _dot((hs_s[...] * _silu(zg)).astype(bf16), wout_ref[...])
    xo_ref[0] = _layer_norm(DN_ALPHA * x + (1.0 + gate) * y, lng_ref[...], lnb_ref[...])


def _rglru_prompt(x, mod, h0, c0, w_in, conv_w, conv_b, w_gate, b_r, b_i, lam, w_out, ln_g, ln_b, tc):
    bsz, t, d = x.shape
    tc = min(tc, t)
    const2 = lambda b, i: (0, 0)
    const3 = lambda b, i: (0, 0, 0)
    perb = lambda b, i: (b, 0, 0)
    kern = functools.partial(_rglru_prompt_kernel, tc=tc, d=d)
    return pl.pallas_call(
        kern,
        grid=(bsz, t // tc),
        in_specs=[
            pl.BlockSpec((1, tc, d), lambda b, i: (b, i, 0)),
            pl.BlockSpec((1, 3, d), perb),
            pl.BlockSpec((1, 1, d), perb),
            pl.BlockSpec((1, CONV_W - 1, d), perb),
            pl.BlockSpec(w_in.shape, const2),
            pl.BlockSpec(conv_w.shape, const2),
            pl.BlockSpec((1, d), const2),
            pl.BlockSpec(w_gate.shape, const3),
            pl.BlockSpec((1, d), const2),
            pl.BlockSpec((1, d), const2),
            pl.BlockSpec((1, d), const2),
            pl.BlockSpec(w_out.shape, const2),
            pl.BlockSpec((1, d), const2),
            pl.BlockSpec((1, d), const2),
        ],
        out_specs=[
            pl.BlockSpec((1, tc, d), lambda b, i: (b, i, 0)),
            pl.BlockSpec((1, 1, d), perb),
            pl.BlockSpec((1, CONV_W - 1, d), perb),
        ],
        out_shape=[
            jax.ShapeDtypeStruct((bsz, t, d), f32),
            jax.ShapeDtypeStruct((bsz, 1, d), f32),
            jax.ShapeDtypeStruct((bsz, CONV_W - 1, d), f32),
        ],
        scratch_shapes=[
            pltpu.VMEM((tc + 8, d), f32),
            pltpu.VMEM((tc, d), f32),
            pltpu.VMEM((tc, d), f32),
            pltpu.VMEM((tc, d), f32),
            pltpu.VMEM((8, d), f32),
        ],
        compiler_params=_params(("arbitrary", "arbitrary")),
        name="rglru_prompt",
    )(x, mod, h0, c0, w_in, conv_w, conv_b, w_gate, b_r, b_i, lam, w_out, ln_g, ln_b)


def _rglru_sample_kernel(x_ref, mod_ref, h0_ref, c0_ref, win_ref, cw_ref, cb_ref, wg_ref, br_ref,
                         bi_ref, lam_ref, wout_ref, lng_ref, lnb_ref,
                         xo_ref, hl_ref, cl_ref, *, s_len, bsz, d):
    x = x_ref[...]
    mod = mod_ref[...]
    shift, scale, gate = mod[0:1], mod[1:2], mod[2:3]
    m = x * (1.0 + scale) + shift
    u = _dot(m.reshape(s_len * bsz, d).astype(bf16), win_ref[...])
    xb = u[:, :d].reshape(s_len, bsz, d)
    zg = u[:, d:]
    xp = jnp.concatenate([c0_ref[...], xb], axis=0)
    cw = cw_ref[...]
    xc = cb_ref[...] + xp[0:s_len] * cw[0:1]
    for k in range(1, CONV_W):
        xc = xc + xp[k:k + s_len] * cw[k:k + 1]
    cl_ref[...] = xp[s_len:s_len + CONV_W - 1]
    a, b = _rglru_gates(xc.reshape(s_len * bsz, d), wg_ref, br_ref[...], bi_ref[...], lam_ref[...])
    h = h0_ref[...]
    hs = []
    for s in range(s_len):
        h = a[s * bsz:(s + 1) * bsz] * h + b[s * bsz:(s + 1) * bsz]
        hs.append(h)
    hl_ref[...] = h
    hs = jnp.concatenate(hs, axis=0)
    y = _dot((hs * _silu(zg)).astype(bf16), wout_ref[...])
    xo = _layer_norm(DN_ALPHA * x + (1.0 + gate) * y.reshape(s_len, bsz, d), lng_ref[...], lnb_ref[...])
    xo_ref[...] = xo


def _rglru_sample(x_tm, mod_tm, h0, c0_tm, w_in, conv_w, conv_b, w_gate, b_r, b_i, lam, w_out, ln_g, ln_b):
    s_len, bsz, d = x_tm.shape
    kern = functools.partial(_rglru_sample_kernel, s_len=s_len, bsz=bsz, d=d)
    return pl.pallas_call(
        kern,
        out_shape=[
            jax.ShapeDtypeStruct((s_len, bsz, d), f32),
            jax.ShapeDtypeStruct((bsz, d), f32),
            jax.ShapeDtypeStruct((CONV_W - 1, bsz, d), f32),
        ],
        compiler_params=pltpu.CompilerParams(vmem_limit_bytes=VMEM_LIMIT_BYTES),
        name="rglru_sample",
    )(x_tm, mod_tm, h0, c0_tm, w_in, conv_w, conv_b, w_gate, b_r, b_i, lam, w_out, ln_g, ln_b)


def _proj_kernel(x_ref, mod_ref, wkv_ref, wq_ref, wz_ref, wg_ref, bg_ref,
                 cmp_ref, sel_ref, win_ref, selb_ref, winb_ref, q_ref, z_ref, gl_ref, *, nb, tr, d):
    x = x_ref[...]
    mod = mod_ref[...]
    m = x * (1.0 + mod[:, 1:2, :]) + mod[:, 0:1, :]
    xb = x.reshape(nb * tr, d).astype(bf16)
    mb = m.reshape(nb * tr, d).astype(bf16)
    kv = _dot(xb, wkv_ref[...])
    w = kv.shape[1] // 3
    sel = kv[:, w:2 * w]
    win = kv[:, 2 * w:]
    cmp_ref[...] = kv[:, :w]
    sel_ref[...] = sel
    win_ref[...] = win
    selb_ref[...] = sel.astype(bf16)
    winb_ref[...] = win.astype(bf16)
    q_ref[...] = (_dot(mb, wq_ref[...]) * (HEAD_DIM ** -0.5)).astype(bf16)
    z_ref[...] = _dot(mb, wz_ref[...])
    gl_ref[...] = _dot(mb, wg_ref[...]) + bg_ref[...]


def _proj(x, mod, w_kv, w_q, w_z, w_g, b_g, nb, tr):
    bsz, t, d = x.shape
    kvw = w_kv.shape[1] // 3
    hd = w_q.shape[1]
    gw = w_g.shape[1]
    nt = t // tr
    const2 = lambda b, i: (0, 0)
    blk = lambda width: pl.BlockSpec((nb * tr, width), lambda b, i: (b * nt + i, 0))
    out = lambda width, dt: jax.ShapeDtypeStruct((bsz * t, width), dt)
    kern = functools.partial(_proj_kernel, nb=nb, tr=tr, d=d)
    return pl.pallas_call(
        kern,
        grid=(bsz // nb, nt),
        in_specs=[
            pl.BlockSpec((nb, tr, d), lambda b, i: (b, i, 0)),
            pl.BlockSpec((nb, 3, d), lambda b, i: (b, 0, 0)),
            pl.BlockSpec(w_kv.shape, const2),
            pl.BlockSpec(w_q.shape, const2),
            pl.BlockSpec(w_z.shape, const2),
            pl.BlockSpec(w_g.shape, const2),
            pl.BlockSpec((1, gw), const2),
        ],
        out_specs=[blk(kvw), blk(kvw), blk(kvw), blk(kvw), blk(kvw), blk(hd), blk(hd), blk(gw)],
        out_shape=[out(kvw, f32), out(kvw, f32), out(kvw, f32), out(kvw, bf16), out(kvw, bf16),
                   out(hd, bf16), out(hd, f32), out(gw, f32)],
        compiler_params=_params(("arbitrary", "arbitrary")),
        name="nsa_proj",
    )(x, mod, w_kv, w_q, w_z, w_g, b_g)


def _proj_prompt_kernel(x_ref, mod_ref, wkvt_ref, wq_ref, wz_ref, wg_ref, bg_ref, e_ref, prow_ref, ones_ref,
                        cmpt_ref, selt_ref, wint_ref, cmppg_ref, kaug_ref, vaug_ref, wintt_ref, q_ref, z_ref,
                        gl_ref, *, tr):
    x = x_ref[0]
    mod = mod_ref[0]
    m = x * (1.0 + mod[1:2]) + mod[0:1]
    mb = m.astype(bf16)
    kvt = _dot_nt(wkvt_ref[...], x.astype(bf16))
    kvw = kvt.shape[0] // 3
    cmpt = kvt[:kvw]
    selt = kvt[kvw:2 * kvw]
    wint = kvt[2 * kvw:]
    cmpt_ref[0] = cmpt
    selt_ref[0] = selt
    wint_ref[0] = wint
    selb = selt.astype(bf16)
    n_c = e_ref.shape[1]
    for g in range(N_KV):
        kaug_ref[0, g, 0, 0:n_c] = e_ref[0]
        kaug_ref[0, g, 0, n_c:n_c + HEAD_DIM] = selb[g * GRP:g * GRP + HEAD_DIM]
        kaug_ref[0, g, 0, n_c + HEAD_DIM:] = prow_ref[...]
        vaug_ref[0, g, 0, 0:HEAD_DIM] = selb[g * GRP + HEAD_DIM:(g + 1) * GRP]
        vaug_ref[0, g, 0, HEAD_DIM:] = ones_ref[...]
    winb = wint.astype(bf16)
    for k in range(tr // PAGE_SIZE):
        cmppg_ref[k] = cmpt[:, k * PAGE_SIZE:(k + 1) * PAGE_SIZE]
        wintt_ref[0, k] = winb[:, k * PAGE_SIZE:(k + 1) * PAGE_SIZE]
    q_ref[...] = (_dot(mb, wq_ref[...]) * (HEAD_DIM ** -0.5)).astype(bf16)
    z_ref[...] = _dot(mb, wz_ref[...])
    gl_ref[...] = _dot(mb, wg_ref[...]) + bg_ref[...]


def _proj_prompt(x, mod, w_kvt, w_q, w_z, w_g, b_g, e, prow, ones, tr):
    bsz, t, d = x.shape
    kvw = w_kvt.shape[0] // 3
    hd = w_q.shape[1]
    gw = w_g.shape[1]
    nt = t // tr
    ppt = tr // PAGE_SIZE
    ka_rows = e.shape[1] + HEAD_DIM + prow.shape[0]
    va_rows = HEAD_DIM + ones.shape[0]
    const2 = lambda b, i: (0, 0)
    rows = lambda width: pl.BlockSpec((tr, width), lambda b, i: (b * nt + i, 0))
    tmin = pl.BlockSpec((1, kvw, tr), lambda b, i: (b, 0, i))
    kern = functools.partial(_proj_prompt_kernel, tr=tr)
    return pl.pallas_call(
        kern,
        grid=(bsz, nt),
        in_specs=[
            pl.BlockSpec((1, tr, d), lambda b, i: (b, i, 0)),
            pl.BlockSpec((1, 3, d), lambda b, i: (b, 0, 0)),
            pl.BlockSpec(w_kvt.shape, const2),
            pl.BlockSpec(w_q.shape, const2),
            pl.BlockSpec(w_z.shape, const2),
            pl.BlockSpec(w_g.shape, const2),
            pl.BlockSpec((1, gw), const2),
            pl.BlockSpec((1,) + e.shape[1:], lambda b, i: (i, 0, 0)),
            pl.BlockSpec(prow.shape, const2),
            pl.BlockSpec(ones.shape, const2),
        ],
        out_specs=[
            tmin, tmin, tmin,
            pl.BlockSpec((ppt, kvw, PAGE_SIZE), lambda b, i: (b * nt + i, 0, 0)),
            pl.BlockSpec((1, N_KV, 1, ka_rows, tr), lambda b, i: (b, 0, i, 0, 0)),
            pl.BlockSpec((1, N_KV, 1, va_rows, tr), lambda b, i: (b, 0, i, 0, 0)),
            pl.BlockSpec((1, ppt, kvw, PAGE_SIZE), lambda b, i: (b, i, 0, 0)),
            rows(hd), rows(hd), rows(gw),
        ],
        out_shape=[
            jax.ShapeDtypeStruct((bsz, kvw, t), f32),
            jax.ShapeDtypeStruct((bsz, kvw, t), f32),
            jax.ShapeDtypeStruct((bsz, kvw, t), f32),
            jax.ShapeDtypeStruct((bsz * t // PAGE_SIZE, kvw, PAGE_SIZE), f32),
            jax.ShapeDtypeStruct((bsz, N_KV, nt, ka_rows, tr), bf16),
            jax.ShapeDtypeStruct((bsz, N_KV, nt, va_rows, tr), bf16),
            jax.ShapeDtypeStruct((bsz, t // PAGE_SIZE, kvw, PAGE_SIZE), bf16),
            jax.ShapeDtypeStruct((bsz * t, hd), bf16),
            jax.ShapeDtypeStruct((bsz * t, hd), f32),
            jax.ShapeDtypeStruct((bsz * t, gw), f32),
        ],
        compiler_params=_params(("arbitrary", "arbitrary")),
        name="nsa_proj_prompt",
    )(x, mod, w_kvt, w_q, w_z, w_g, b_g, e, prow, ones)


def _compress_kernel(x_ref, pet_ref, w1_ref, b1_ref, w2_ref, b2_ref, kc_ref, vc_ref, *, npg):
    x_rows = x_ref.reshape(npg * PAGE_PITCH, x_ref.shape[2])

    def dim_rows(c, dd):
        return jnp.concatenate(
            [(x_rows[pl.ds((g * 2 + c) * HEAD_DIM + dd, npg, stride=PAGE_PITCH), :]
              + pet_ref[c, dd:dd + 1, :]).astype(bf16) for g in range(N_KV)], axis=0)

    outs = []
    for c in range(2):
        acc = None
        for dp in range(HEAD_DIM // 2):
            lhs = jnp.concatenate([dim_rows(c, 2 * dp), dim_rows(c, 2 * dp + 1)], axis=1)
            part = _dot(lhs, w1_ref[c, dp])
            acc = part if acc is None else acc + part
        hid = _silu(acc + b1_ref[c])
        outs.append(_dot(hid.astype(bf16), w2_ref[c]) + b2_ref[c])
    for g in range(N_KV):
        kc_ref[g] = outs[0][g * npg:(g + 1) * npg, :].astype(kc_ref.dtype)
        vc_ref[g] = outs[1][g * npg:(g + 1) * npg, :].astype(vc_ref.dtype)


def _compress(x2d, pet, w1, b1, w2, b2, npg, out_dtype):
    rows, width = x2d.shape
    n_pages = rows // (N_KV * GRP)
    npg = min(npg, n_pages)
    while n_pages % npg:
        npg -= 8
    ow = (PAGE_SIZE // CMP_BLOCK) * HEAD_DIM
    const3 = lambda i: (0, 0, 0)
    kern = functools.partial(_compress_kernel, npg=npg)
    out = pl.BlockSpec((N_KV, npg, ow), lambda i: (0, i, 0))
    return pl.pallas_call(
        kern,
        grid=(n_pages // npg,),
        in_specs=[
            pl.BlockSpec((npg, PAGE_PITCH, width), lambda i: (i, 0, 0)),
            pl.BlockSpec(pet.shape, const3),
            pl.BlockSpec(w1.shape, lambda i: (0, 0, 0, 0), pipeline_mode=pl.Buffered(1)),
            pl.BlockSpec(b1.shape, const3),
            pl.BlockSpec(w2.shape, const3),
            pl.BlockSpec(b2.shape, const3),
        ],
        out_specs=[out, out],
        out_shape=[jax.ShapeDtypeStruct((N_KV, n_pages, ow), out_dtype)] * 2,
        compiler_params=_params(("arbitrary",)),
        name="nsa_compress",
    )(x2d.reshape(n_pages, N_KV * GRP, width), pet, w1, b1, w2, b2)


def _block_ids(n_pages, lanes, axis):
    shape = (1, lanes) if axis == 1 else (lanes, 1)
    r = lax.broadcasted_iota(i32, shape, axis)
    bpp = PAGE_SIZE // CMP_BLOCK
    perm = jnp.where(r < n_pages, bpp * r, bpp * (r - n_pages) + 1)
    return jnp.where(r < bpp * n_pages, perm, r)


def _unpage(x):
    return jnp.concatenate([x[:, :HEAD_DIM], x[:, HEAD_DIM:]], axis=0)


def _cmp_branch(q4, kc, vc, slope, t_col, nq, jrow):
    s = _dot_nt(q4, kc)
    c_end = (jrow + 1) * CMP_BLOCK - 1
    dist = t_col.astype(f32) - c_end.astype(f32)
    s = s - slope * dist
    mask = c_end <= t_col
    s = jnp.where(mask, s, NEG)
    e = jnp.exp(s - jnp.max(s, axis=-1, keepdims=True))
    p = e / jnp.sum(e, axis=-1, keepdims=True)
    p = jnp.where(mask, p, 0.0)
    o = _dot(p.astype(bf16), vc)
    imp = p[0:nq]
    for h in range(1, HPG):
        imp = imp + p[h * nq:(h + 1) * nq]
    return o, imp


def _select_blocks(imp, tq_col, n_sb, jr):
    nq, lanes = imp.shape
    cb = tq_col // CMP_BLOCK
    forced = (jr == 0) | (jr == cb) | (jr == cb - 1)
    causal = jr <= cb
    score = jnp.where(forced, FORCED, jnp.where(causal, imp, -1.0))
    score = jnp.where(jr < n_sb, score, REMOVED)
    jf = jr.astype(f32)
    sel = jnp.zeros((nq, lanes), f32)
    for _ in range(min(N_SEL, n_sb)):
        mx = jnp.max(score, axis=-1, keepdims=True)
        idx = jnp.min(jnp.where(score == mx, jf, 1e9), axis=-1, keepdims=True)
        hit = jf == idx
        sel = jnp.where(hit, 1.0, sel)
        score = jnp.where(hit, REMOVED, score)
    return sel


def _flash_update(carry, q4, kt, vt, slope, valid, pos_rel):
    m_i, l_i, acc = carry
    s = _dot(q4, kt) + slope * pos_rel
    s = jnp.where(valid, s, NEG)
    m_new = jnp.maximum(m_i, jnp.max(s, axis=-1, keepdims=True))
    alpha = jnp.exp(m_i - m_new)
    p = jnp.exp(s - m_new)
    l_new = alpha * l_i + jnp.sum(p, axis=-1, keepdims=True)
    acc = alpha * acc + _dot_nt(p.astype(bf16), vt)
    return m_new, l_new, acc


def _block_mask(selb, jcol, pos, reps):
    onehot = (jcol == pos // CMP_BLOCK).astype(bf16)
    msk = _dot(selb, onehot)
    return jnp.concatenate([msk] * reps, axis=0) > 0.5


def _gate_mix(gl, o_cmp, o_sel, o_win, nq):
    gs = _sigmoid(gl)
    outs = []
    for h in range(HPG):
        sl = slice(h * nq, (h + 1) * nq)
        outs.append(gs[:, 3 * h:3 * h + 1] * o_cmp[sl] + gs[:, 3 * h + 1:3 * h + 2] * o_sel[sl]
                    + gs[:, 3 * h + 2:3 * h + 3] * o_win[sl])
    return jnp.concatenate(outs, axis=1)


def _init_carry(rows):
    return (jnp.full((rows, 1), NEG, f32), jnp.zeros((rows, 1), f32), jnp.zeros((rows, HEAD_DIM), f32))


def _select_blocks_t(impt, tq_row, n_sb, jcol, fillers=()):
    n_c, nq = impt.shape
    cb = tq_row // CMP_BLOCK
    forced = (jcol == 0) | (jcol == cb) | (jcol == cb - 1)
    causal = jcol <= cb
    score = jnp.where(forced, REMOVED, jnp.where(causal, impt, -1.0))
    score = jnp.where(jcol < n_sb, score, REMOVED)
    jf = jcol.astype(f32)
    sel = jnp.where(forced & (jcol < n_sb), 1.0, 0.0) + jnp.zeros((n_c, nq), f32)
    rounds = max(min(N_SEL, n_sb) - 3, 0)
    fillers = list(fillers)
    every = max(rounds // max(len(fillers), 1), 1)
    for it in range(rounds):
        mx = jnp.max(score, axis=0, keepdims=True)
        idx = jnp.min(jnp.where(score == mx, jf, 1e9), axis=0, keepdims=True)
        hit = jf == idx
        sel = jnp.where(hit, 1.0, sel)
        score = jnp.where(hit, REMOVED, score)
        if fillers and it % every == 0:
            fillers.pop(0)()
    for f in fillers:
        f()
    return sel


def _nsa_prompt_kernel(q_ref, gl_ref, kc_ref, vc_ref, kaug_ref, vaug_ref, kwin_ref, scol_ref, srow_ref, coef_ref,
                       wbias_ref, cbias_ref, eye_ref, tilemap_ref, knull_ref, o_ref,
                       s0_s, s1_s, p0_s, p1_s, a0_s, a1_s, m_s, acc_s, tiles_s, *, t_len, tq, tk, tkw):
    qi = pl.program_id(2)
    t0 = qi * tq
    rows = HPG * tq
    q = q_ref[0]
    q4 = jnp.concatenate([q[:, h * HEAD_DIM:(h + 1) * HEAD_DIM] for h in range(HPG)], axis=0)
    slope = scol_ref[0]
    t_col = t0 + lax.broadcasted_iota(i32, (rows, 1), 0) % tq

    n_c = t_len // CMP_BLOCK
    n_pages = t_len // PAGE_SIZE
    jrow = _block_ids(n_pages, n_c, 1)
    jcol = _block_ids(n_pages, n_c, 0)
    kc = _unpage(kc_ref[0]).astype(bf16)
    vc = _unpage(vc_ref[0]).astype(bf16)
    o_cmp_box = []

    def cmp_output():
        o_cmp_box.append(_cmp_branch(q4, kc, vc, slope, t_col, tq, jrow)[0])

    def init_state():
        m_s[...] = jnp.full((rows, 1), NEG, f32)
        acc_s[...] = jnp.zeros(acc_s.shape, f32)
        p1_s[...] = jnp.zeros(p1_s.shape, bf16)
        a1_s[...] = jnp.ones((rows, 1), f32)

    nslots = (WINDOW + tq) // tkw
    base = (t0 - WINDOW) // tkw
    kts, vts, pens = [], [], []
    for m in range(nslots):
        kw = base + m
        kvw = kwin_ref[0, jnp.maximum(kw, 0)]
        kts.append(kvw[:HEAD_DIM])
        vts.append(kvw[HEAD_DIM:])
        pens.append(jnp.broadcast_to(jnp.where(kw >= 0, 0.0, NEG).astype(f32), (1, tkw)))
    sw = _dot(q4, jnp.concatenate(kts, axis=1))
    pen = jnp.concatenate(pens, axis=1)
    pw_h, lw_h = [], []

    def window_softmax(h):
        def run():
            s_h = sw[h * tq:(h + 1) * tq] + wbias_ref[0, h * tq:(h + 1) * tq] + pen
            p_h = jnp.exp(s_h - jnp.max(s_h, axis=-1, keepdims=True))
            pw_h.append(p_h.astype(bf16))
            lw_h.append(jnp.sum(p_h, axis=-1, keepdims=True))
        return run

    t_row = t0 + lax.broadcasted_iota(i32, (1, rows), 1) % tq
    c_end = (jcol + 1) * CMP_BLOCK - 1
    st = _dot_nt(kc, q4) - srow_ref[0] * (t_row.astype(f32) - c_end.astype(f32))
    maskt = c_end <= t_row
    st = jnp.where(maskt, st, NEG)
    et = jnp.exp(st - jnp.max(st, axis=0, keepdims=True))
    pt = et / jnp.sum(et, axis=0, keepdims=True)
    pt = jnp.where(maskt, pt, 0.0)
    impt = pt[:, 0:tq]
    for h in range(1, HPG):
        impt = impt + pt[:, h * tq:(h + 1) * tq]
    selt = _select_blocks_t(impt, t0 + lax.broadcasted_iota(i32, (1, tq), 1), n_c, jcol,
                            fillers=[window_softmax(h) for h in range(HPG)] + [cmp_output, init_state])
    o_cmp = o_cmp_box[0]
    sel = _dot_nt(eye_ref[...], selt.astype(bf16))
    selbias = ((sel - 1.0) * -NEG).astype(bf16)
    q_aug = jnp.concatenate([jnp.concatenate([selbias] * HPG, axis=0), q4, coef_ref[0]], axis=1)


    def softmax_into(s, kt, p_dst, a_dst):
        c = slope * (kt * tk - t0).astype(f32)
        m_i = m_s[...]
        m_new = jnp.maximum(m_i, jnp.max(s, axis=-1, keepdims=True) + c)
        p_dst[...] = jnp.exp(s - (m_new - c)).astype(bf16)
        a_dst[...] = jnp.exp(m_i - m_new)
        m_s[...] = m_new

    def values_from(p_src, a_src, kt):
        acc_s[...] = a_src[...] * acc_s[...] + _dot_nt(p_src[...], vaug_ref[0, 0, kt])

    kd = t0 // tk
    tile_hits = jnp.max(_dot(tilemap_ref[...], selt.astype(bf16)), axis=1, keepdims=True)
    n_act = jnp.int32(0)
    tiles_s[0] = 0
    for kt in range(t_len // tk):
        tiles_s[n_act] = kt
        hit = (jnp.max(tile_hits[kt:kt + 1, :]) > 0.5) & (kt < kd)
        n_act = n_act + hit.astype(i32)
    tiles_s[n_act] = 0
    npairs = (n_act + 1) // 2
    last_piped = jnp.maximum(2 * npairs - 1, 0)

    def scores_at(dst, i):
        keys = jnp.where(i >= n_act, knull_ref[...], kaug_ref[0, 0, tiles_s[i]])
        dst[...] = _dot(q_aug, keys)

    scores_at(s0_s, 0)

    def pair(j, carry):
        ia = 2 * j
        values_from(p1_s, a1_s, tiles_s[jnp.maximum(ia - 1, 0)])
        scores_at(s1_s, ia + 1)
        softmax_into(s0_s[...], tiles_s[ia], p0_s, a0_s)
        values_from(p0_s, a0_s, tiles_s[ia])
        scores_at(s0_s, jnp.minimum(ia + 2, last_piped))
        softmax_into(s1_s[...], tiles_s[ia + 1], p1_s, a1_s)
        return carry

    lax.fori_loop(0, npairs, pair, 0)
    values_from(p1_s, a1_s, tiles_s[last_piped])

    s_d = _dot(q_aug, kaug_ref[0, 0, kd]) + jnp.concatenate([cbias_ref[(t0 % tk) // tq]] * HPG, axis=0)
    o_win = (_dot_nt(jnp.concatenate(pw_h, axis=0), jnp.concatenate(vts, axis=1))
             / jnp.concatenate(lw_h, axis=0))
    softmax_into(s_d, kd, p0_s, a0_s)
    values_from(p0_s, a0_s, kd)
    acc = acc_s[...]
    o_sel = acc[:, :HEAD_DIM] / acc[:, HEAD_DIM:HEAD_DIM + 1]

    o_ref[0] = _gate_mix(gl_ref[0], o_cmp, o_sel, o_win, tq)


def _prompt_consts(t_len, tq, tk, tkw):
    n_c = t_len // CMP_BLOCK
    n_pages = t_len // PAGE_SIZE
    rows = HPG * tq
    r = np.arange(n_c)
    bpp = PAGE_SIZE // CMP_BLOCK
    ids = np.where(r < n_pages, bpp * r, bpp * (r - n_pages) + 1)
    onehot = ids[:, None] == (np.arange(t_len) // CMP_BLOCK)[None, :]
    e = onehot.reshape(n_c, t_len // tk, tk).transpose(1, 0, 2)
    lane = np.arange(tk)
    prow = np.zeros((HEAD_DIM, tk), np.float32)
    prow[0:3] = lane // 256
    prow[3:6] = lane % 256
    slopes = (2.0 ** (-8.0 * np.arange(1, N_HEADS + 1) / N_HEADS)).astype(np.float32)
    srow = np.repeat(slopes.reshape(N_KV, HPG), tq, axis=1)
    s1 = srow.astype(jnp.bfloat16).astype(np.float32)
    s2 = (srow - s1).astype(jnp.bfloat16).astype(np.float32)
    s3 = (srow - s1 - s2).astype(jnp.bfloat16).astype(np.float32)
    coef = np.zeros((N_KV, rows, HEAD_DIM), np.float32)
    for k, piece in enumerate((s1, s2, s3)):
        coef[:, :, k] = 256.0 * piece
        coef[:, :, 3 + k] = piece
    coef[:, :, 6] = 1.0
    knull = np.zeros((n_c + 2 * HEAD_DIM, tk), np.float32)
    knull[n_c + HEAD_DIM + 6] = NEG
    a = np.tile(np.arange(tq), HPG)[:, None]
    nslots = (WINDOW + tq) // tkw
    dpos = a + WINDOW - np.arange(nslots * tkw)[None, :]
    band = np.where((dpos >= 0) & (dpos <= WINDOW), 0.0, NEG)
    wbias = -srow[:, :, None] * dpos[None] + band[None]
    off = np.arange(tk // tq)[:, None, None] * tq
    cbias = np.where(lane[None, None, :] <= off + np.arange(tq)[None, :, None], 0.0, NEG)
    n_tiles = t_len // tk
    tilemap = np.arange(-(-n_tiles // 16) * 16)[:, None] == (ids // (tk // CMP_BLOCK))[None, :]
    return dict(
        tilemap=jnp.asarray(tilemap, bf16), knull=jnp.asarray(knull, bf16),
        scol=jnp.asarray(srow.reshape(N_KV, rows, 1), f32), srow=jnp.asarray(srow.reshape(N_KV, 1, rows), f32),
        coef=jnp.asarray(coef, bf16), e=jnp.asarray(e, bf16), prow=jnp.asarray(prow, bf16),
        ones=jnp.ones((16, tk), bf16), wbias=jnp.asarray(wbias, f32), cbias=jnp.asarray(cbias, f32),
        eye=jnp.eye(tq, dtype=bf16))


def _nsa_prompt(q, gl, kc, vc, kaug, vaug, kwin, c, tq):
    bsz, t_len, _ = q.shape
    tk = kaug.shape[-1]
    tkw = kwin.shape[-1]
    npb = t_len // PAGE_SIZE
    rows = HPG * tq
    kern = functools.partial(_nsa_prompt_kernel, t_len=t_len, tq=tq, tk=tk, tkw=tkw)
    augspec = lambda arr: pl.BlockSpec((1, 1) + arr.shape[2:], lambda b, g, i: (b, g, 0, 0, 0))
    cspec = pl.BlockSpec((1, npb, kc.shape[-1]), lambda b, g, i: (g, b, 0))
    perg = lambda arr: pl.BlockSpec((1,) + arr.shape[1:], lambda b, g, i: (g,) + (0,) * (arr.ndim - 1))
    whole = lambda arr: pl.BlockSpec(arr.shape, lambda b, g, i: (0,) * arr.ndim)
    return pl.pallas_call(
        kern,
        grid=(bsz, N_KV, t_len // tq),
        in_specs=[
            pl.BlockSpec((1, tq, QGRP), lambda b, g, i: (b, i, g)),
            pl.BlockSpec((1, tq, 128), lambda b, g, i: (b, i, g)),
            cspec, cspec,
            augspec(kaug), augspec(vaug),
            pl.BlockSpec((1, t_len // tkw, GRP, tkw), lambda b, g, i: (b, 0, g, 0)),
            perg(c["scol"]), perg(c["srow"]), perg(c["coef"]),
            perg(c["wbias"]), whole(c["cbias"]), whole(c["eye"]), whole(c["tilemap"]), whole(c["knull"]),
        ],
        out_specs=pl.BlockSpec((1, tq, QGRP), lambda b, g, i: (b, i, g)),
        out_shape=jax.ShapeDtypeStruct((bsz, t_len, N_KV * QGRP), f32),
        scratch_shapes=[
            pltpu.VMEM((rows, tk), f32), pltpu.VMEM((rows, tk), f32),
            pltpu.VMEM((rows, tk), bf16), pltpu.VMEM((rows, tk), bf16),
            pltpu.VMEM((rows, 1), f32), pltpu.VMEM((rows, 1), f32),
            pltpu.VMEM((rows, 1), f32), pltpu.VMEM((rows, vaug.shape[3]), f32),
            pltpu.SMEM((t_len // tk + 8,), i32),
        ],
        compiler_params=_params(("arbitrary", "arbitrary", "arbitrary")),
        name="nsa_prompt_attn",
    )(q, gl, kc, vc, kaug, vaug, kwin, c["scol"], c["srow"], c["coef"], c["wbias"], c["cbias"], c["eye"],
      c["tilemap"], c["knull"])


def _nsa_sample_kernel(pt_ref, q_ref, gl_ref, kcall_ref, vcall_ref, pid_ref, newsel_ref, swin_ref, nwin_ref,
                       slope_ref, *rest, s_len, past_len, pp, n_groups, lanes):
    page_refs = rest[:pp]
    o_ref = rest[pp]
    sel_s, ocmp_s, qbd_s, m_s, l_s, acc_s = rest[pp + 1:]
    j = pl.program_id(1)
    rows = HPG * s_len
    n_c = past_len // CMP_BLOCK
    npb = past_len // PAGE_SIZE
    jrow = _block_ids(npb, lanes, 1)
    jcol = _block_ids(npb, lanes, 0)
    q = q_ref[0].astype(f32)
    t_col = past_len + lax.broadcasted_iota(i32, (rows, 1), 0) % s_len
    tq_col = past_len + lax.broadcasted_iota(i32, (s_len, 1), 0)

    def q_of(g):
        return jnp.concatenate(
            [q[:, (g * HPG + h) * HEAD_DIM:(g * HPG + h + 1) * HEAD_DIM] for h in range(HPG)],
            axis=0).astype(bf16)

    groups = range(N_KV)
    all_rows = N_KV * rows
    slope_all = jnp.concatenate([slope_ref[g] for g in groups], axis=0)
    t_all = past_len + lax.broadcasted_iota(i32, (all_rows, 1), 0) % s_len

    @pl.when(j == 0)
    def _():
        n_all = kcall_ref.shape[1]
        gather = (lax.broadcasted_iota(i32, (npb, n_all), 1) == pid_ref[0]).astype(bf16)
        zeros = jnp.zeros((rows, HEAD_DIM), bf16)
        imps = []
        for g in groups:
            kc = _unpage(_dot(gather, kcall_ref[g])).astype(bf16)
            vc = _unpage(_dot(gather, vcall_ref[g])).astype(bf16)
            q4 = q_of(g)
            o_cmp, imp = _cmp_branch(q4, kc, vc, slope_ref[g], t_col, s_len, jrow[:, :n_c])
            imps.append(jnp.concatenate([imp, jnp.zeros((s_len, lanes - n_c), f32)], axis=1))
            ocmp_s[g] = o_cmp
            qbd_s[g * rows:(g + 1) * rows] = jnp.concatenate([q4 if gg == g else zeros for gg in groups], axis=1)
        sel_s[...] = _select_blocks(jnp.concatenate(imps, axis=0), jnp.concatenate([tq_col] * N_KV, axis=0),
                                    n_c + 1, jrow)
        m_s[...] = jnp.full((all_rows, 1), NEG, f32)
        l_s[...] = jnp.zeros((all_rows, 1), f32)
        acc_s[...] = jnp.zeros(acc_s.shape, f32)

    def stack(keys):
        kt = jnp.concatenate([keys[g * GRP:g * GRP + HEAD_DIM] for g in groups], axis=0).astype(bf16)
        vt = jnp.concatenate([keys[g * GRP + HEAD_DIM:(g + 1) * GRP] for g in groups], axis=0).astype(bf16)
        return kt, vt

    def attend(keys, pos):
        onehot = (jcol == pos // CMP_BLOCK).astype(bf16)
        msk = _dot(sel_s[...].astype(bf16), onehot)
        msk = jnp.concatenate([msk[g * s_len:(g + 1) * s_len] for g in groups for _ in range(HPG)], axis=0)
        valid = (msk > 0.5) & (pos <= t_all)
        kt, vt = stack(keys)
        m_new, l_new, acc = _flash_update((m_s[...], l_s[...], acc_s[...]), qbd_s[...], kt, vt, slope_all,
                                          valid, (pos - past_len).astype(f32))
        m_s[...] = m_new
        l_s[...] = l_new
        acc_s[...] = acc

    nk = pp * PAGE_SIZE
    keys = jnp.concatenate([r[0] for r in page_refs], axis=1)
    attend(keys, j * nk + lax.broadcasted_iota(i32, (1, nk), 1))

    @pl.when(j == n_groups - 1)
    def _():
        new_keys = newsel_ref[0]
        attend(new_keys, past_len + lax.broadcasted_iota(i32, (1, new_keys.shape[1]), 1))
        o_sel = acc_s[...] / l_s[...]
        carry = (jnp.full((all_rows, 1), NEG, f32), jnp.zeros((all_rows, 1), f32),
                 jnp.zeros((all_rows, N_KV * HEAD_DIM), f32))
        for wkeys, wbase in ((swin_ref[0], past_len - swin_ref.shape[2]), (nwin_ref[0], past_len)):
            wpos = wbase + lax.broadcasted_iota(i32, (1, wkeys.shape[1]), 1)
            dpos = t_all - wpos
            valid = (dpos >= 0) & (dpos <= WINDOW)
            kt, vt = stack(wkeys)
            carry = _flash_update(carry, qbd_s[...], kt, vt, slope_all, valid, (wpos - past_len).astype(f32))
        o_win = carry[2] / carry[1]
        own = lambda o, g: o[g * rows:(g + 1) * rows, g * HEAD_DIM:(g + 1) * HEAD_DIM]
        gl = gl_ref[0]
        o_ref[0] = jnp.concatenate(
            [_gate_mix(gl[:, g * 128:(g + 1) * 128], ocmp_s[g], own(o_sel, g), own(o_win, g), s_len)
             for g in groups], axis=1)


def _nsa_sample(page_table, q, gl, kc_all, vc_all, pids, new_sel, state_win, new_win, slopes, cache_sel, pp):
    bsz, s_len, hd = q.shape
    n_pages = page_table.shape[1]
    past_len = n_pages * PAGE_SIZE
    pp = min(pp, n_pages)
    n_groups = n_pages // pp
    n_c = past_len // CMP_BLOCK
    lanes = -(-(n_c + 2) // 128) * 128
    rows = HPG * s_len
    kvw = cache_sel.shape[1]
    perb = lambda b, j, pt: (b, 0, 0)

    def page_spec(i):
        return pl.BlockSpec((1, kvw, PAGE_SIZE), lambda b, j, pt: (pt[b, j * pp + i], 0, 0))

    kern = functools.partial(_nsa_sample_kernel, s_len=s_len, past_len=past_len, pp=pp,
                             n_groups=n_groups, lanes=lanes)
    grid_spec = pltpu.PrefetchScalarGridSpec(
        num_scalar_prefetch=1,
        grid=(bsz, n_groups),
        in_specs=[
            pl.BlockSpec((1, s_len, hd), perb),
            pl.BlockSpec((1, s_len, gl.shape[-1]), perb),
            pl.BlockSpec(kc_all.shape, lambda b, j, pt: (0, 0, 0)),
            pl.BlockSpec(vc_all.shape, lambda b, j, pt: (0, 0, 0)),
            pl.BlockSpec((1, n_pages, 1), perb),
            pl.BlockSpec((1,) + new_sel.shape[1:], perb),
            pl.BlockSpec((1,) + state_win.shape[1:], perb),
            pl.BlockSpec((1,) + new_win.shape[1:], perb),
            pl.BlockSpec(slopes.shape, lambda b, j, pt: (0, 0, 0)),
        ] + [page_spec(i) for i in range(pp)],
        out_specs=pl.BlockSpec((1, s_len, hd), perb),
        scratch_shapes=[
            pltpu.VMEM((N_KV * s_len, lanes), f32),
            pltpu.VMEM((N_KV, rows, HEAD_DIM), f32),
            pltpu.VMEM((N_KV * rows, N_KV * HEAD_DIM), bf16),
            pltpu.VMEM((N_KV * rows, 1), f32),
            pltpu.VMEM((N_KV * rows, 1), f32),
            pltpu.VMEM((N_KV * rows, N_KV * HEAD_DIM), f32),
        ],
    )
    return pl.pallas_call(
        kern,
        grid_spec=grid_spec,
        out_shape=jax.ShapeDtypeStruct((bsz, s_len, hd), f32),
        compiler_params=_params(("arbitrary", "arbitrary")),
        name="nsa_sample_attn",
    )(page_table, q, gl, kc_all, vc_all, pids, new_sel, state_win, new_win, slopes, *([cache_sel] * pp))


def _out_kernel(o_ref, z_ref, x_ref, mod_ref, w_ref, lng_ref, lnb_ref, xo_ref, *, nb, tr, d):
    y = _dot((o_ref[...] * _silu(z_ref[...])).astype(bf16), w_ref[...]).reshape(nb, tr, d)
    gate = mod_ref[...][:, 2:3, :]
    xo_ref[...] = _layer_norm(DN_ALPHA * x_ref[...] + (1.0 + gate) * y, lng_ref[...], lnb_ref[...])


def _out_proj(o, z, x, mod, w_out, ln_g, ln_b, nb, tr):
    bsz, t, d = x.shape
    hd = o.shape[-1]
    nt = t // tr
    const2 = lambda b, i: (0, 0)
    kern = functools.partial(_out_kernel, nb=nb, tr=tr, d=d)
    return pl.pallas_call(
        kern,
        grid=(bsz // nb, nt),
        in_specs=[
            pl.BlockSpec((nb * tr, hd), lambda b, i: (b * nt + i, 0)),
            pl.BlockSpec((nb * tr, hd), lambda b, i: (b * nt + i, 0)),
            pl.BlockSpec((nb, tr, d), lambda b, i: (b, i, 0)),
            pl.BlockSpec((nb, 3, d), lambda b, i: (b, 0, 0)),
            pl.BlockSpec(w_out.shape, const2),
            pl.BlockSpec((1, d), const2),
            pl.BlockSpec((1, d), const2),
        ],
        out_specs=pl.BlockSpec((nb, tr, d), lambda b, i: (b, i, 0)),
        out_shape=jax.ShapeDtypeStruct((bsz, t, d), f32),
        compiler_params=_params(("arbitrary", "arbitrary")),
        name="nsa_out",
    )(o, z, x, mod, w_out, ln_g, ln_b)


def _head_slopes(nq):
    s = 2.0 ** (-8.0 * np.arange(1, N_HEADS + 1) / N_HEADS)
    s = np.repeat(s.reshape(N_KV, HPG), nq, axis=1)
    return jnp.asarray(s.reshape(N_KV, HPG * nq, 1), f32)


def kernel(x_prompt, x_sample, c_prompt, c_sample, state_h, state_conv, cache_cmp, cache_sel, state_win, page_table, w_ada, b_ada, ln_g, ln_b, w_in_a, conv_w_a, conv_b_a, w_r_a, b_r_a, w_i_a, b_i_a, lam_a, w_out_a, w_kv, phi_pe, w_phi1, b_phi1, w_phi2, b_phi2, w_in_b, b_gate_b, w_out_b):
    bp, t_len, d = x_prompt.shape
    bs, s_len, _ = x_sample.shape
    n_pages = page_table.shape[1]
    past_len = n_pages * PAGE_SIZE
    hd = N_HEADS * HEAD_DIM
    kvw = N_KV * GRP
    assert s_len <= CMP_BLOCK and t_len % 128 == 0 and w_ada.shape[0] == DEPTH == 2

    n_c = bp + bs
    pad = -n_c % 8
    c_all = jnp.concatenate([c_prompt, c_sample, jnp.zeros((pad, d), f32)], axis=0)
    mods = _ada_mod(c_all, w_ada, b_ada).reshape(DEPTH, n_c + pad, 3, d)
    mod_p = mods[:, :bp]
    mod_s = mods[:, bp:n_c]

    rg = d // N_RG_BLOCKS
    w_in0 = w_in_a[0].astype(bf16)
    w_gate = jnp.concatenate([w_r_a[0], w_i_a[0]], axis=-1).astype(bf16)
    w_out0 = w_out_a[0].astype(bf16)
    row = lambda v: v.reshape(1, -1)
    rg_args = (w_in0, conv_w_a[0], row(conv_b_a[0]), w_gate, row(b_r_a[0]), row(b_i_a[0]), row(lam_a[0]),
               w_out0, row(ln_g[0]), row(ln_b[0]))
    xp1, h_p, conv_p = _rglru_prompt(x_prompt, mod_p[0], jnp.zeros((bp, 1, d), f32),
                                     jnp.zeros((bp, CONV_W - 1, d), f32), *rg_args, tc=256)
    xs1_tm, h_s, conv_s_tm = _rglru_sample(
        x_sample.transpose(1, 0, 2), mod_s[0].transpose(1, 0, 2), state_h[0],
        state_conv[0].transpose(1, 0, 2), *rg_args)
    xs1 = xs1_tm.transpose(1, 0, 2)
    new_h_p = h_p.reshape(1, bp, d)
    new_conv_p = conv_p.reshape(1, bp, CONV_W - 1, d)
    new_h_s = h_s.reshape(1, bs, d)
    new_conv_s = conv_s_tm.transpose(1, 0, 2).reshape(1, bs, CONV_W - 1, d)

    w_b = w_in_b[0]
    w_q = w_b[:, :hd].astype(bf16)
    w_z = w_b[:, hd:2 * hd].astype(bf16)
    gpg = HPG * 3
    w_g = jnp.pad(w_b[:, 2 * hd:].reshape(d, N_KV, gpg), ((0, 0), (0, 0), (0, 128 - gpg)))
    w_g = w_g.reshape(d, N_KV * 128).astype(bf16)
    b_g = jnp.pad(b_gate_b[0].reshape(N_KV, gpg), ((0, 0), (0, 128 - gpg))).reshape(1, N_KV * 128)
    tmin = lambda v: v.transpose(0, 2, 3, 4, 1)
    tq = min(128, t_len)
    tk = min(512, t_len)
    consts = _prompt_consts(t_len, tq, tk, PAGE_SIZE)
    cmpt_p, selt_p, wint_p, cmp_pages, kaug_p, vaug_p, wint_tiles, q_p, z_p, gl_p = _proj_prompt(
        xp1, mod_p[1], w_kv.T.astype(bf16), w_q, w_z, w_g, b_g, consts["e"], consts["prow"], consts["ones"],
        tr=tk)
    cmp_s, sel_s, win_s, selb_s, winb_s, q_s, z_s, gl_s = _proj(
        xs1, mod_s[1], w_kv.astype(bf16), w_q, w_z, w_g, b_g, nb=bs, tr=s_len)
    per_p = lambda v: v.reshape(bp, t_len, v.shape[-1])
    per_s = lambda v: v.reshape(bs, s_len, v.shape[-1])

    bpp = PAGE_SIZE // CMP_BLOCK
    dphi = w_phi2.shape[1]
    diag2 = lambda w: jnp.concatenate([jnp.concatenate([w, jnp.zeros_like(w)], axis=-1),
                                       jnp.concatenate([jnp.zeros_like(w), w], axis=-1)], axis=-2)
    pet = jnp.tile(phi_pe.transpose(1, 2, 0), (1, 1, bpp))
    w1 = diag2(w_phi1.transpose(0, 2, 1, 3)).astype(bf16)
    w1 = w1.reshape(2, HEAD_DIM // 2, 2 * PAGE_SIZE, bpp * dphi)
    b1 = jnp.tile(b_phi1, (1, bpp)).reshape(2, 1, bpp * dphi)
    w2 = diag2(w_phi2).astype(bf16)
    b2 = jnp.tile(b_phi2, (1, bpp)).reshape(2, 1, bpp * HEAD_DIM)
    kc_p, vc_p = _compress(cmp_pages.reshape(-1, PAGE_SIZE), pet, w1, b1, w2, b2, npg=64, out_dtype=f32)
    kc_all, vc_all = _compress(tmin(cache_cmp).reshape(-1, PAGE_SIZE), pet, w1, b1, w2, b2,
                               npg=64, out_dtype=bf16)

    o_p = _nsa_prompt(per_p(q_p), per_p(gl_p), kc_p, vc_p, kaug_p, vaug_p, wint_tiles, consts, tq=tq)
    tpad = lambda v: jnp.pad(per_s(v), ((0, 0), (0, PAGE_SIZE - s_len), (0, 0))).transpose(0, 2, 1)
    o_s = _nsa_sample(page_table, per_s(q_s), per_s(gl_s), kc_all, vc_all, page_table.reshape(bs, n_pages, 1),
                      tpad(selb_s), tmin(state_win).reshape(bs, kvw, -1), tpad(winb_s), _head_slopes(s_len),
                      tmin(cache_sel).reshape(-1, kvw, PAGE_SIZE), pp=16)

    w_out1 = w_out_b[0].astype(bf16)
    y_p = _out_proj(o_p.reshape(bp * t_len, hd), z_p, xp1, mod_p[1], w_out1, row(ln_g[1]), row(ln_b[1]),
                    nb=1, tr=min(512, t_len))
    y_s = _out_proj(o_s.reshape(bs * s_len, hd), z_s, xs1, mod_s[1], w_out1, row(ln_g[1]), row(ln_b[1]),
                    nb=bs, tr=s_len)

    kv5 = lambda v: v.reshape(bs, s_len, N_KV, 2, HEAD_DIM)
    kv5t = lambda v: v.reshape(bp, N_KV, 2, HEAD_DIM, -1).transpose(0, 4, 1, 2, 3)
    wb = state_win.shape[1]
    new_win_p = kv5t(wint_p[:, :, -min(WINDOW, t_len):])
    new_win_s = jnp.concatenate([state_win, kv5(win_s)], axis=1)[:, -wb:]
    return (y_p, y_s, kv5t(cmpt_p), kv5t(selt_p), new_win_p, new_h_p, new_conv_p,
            kv5(cmp_s), kv5(sel_s), new_win_s, new_h_s, new_conv_s)
```

```python
import functools
import math

import numpy as np
import jax
import jax.numpy as jnp
from jax import lax
from jax.experimental import pallas as pl
from jax.experimental.pallas import tpu as pltpu

f32 = jnp.float32
bf16 = jnp.bfloat16
i32 = jnp.int32

DEPTH = 2
N_RG_BLOCKS = 8
CONV_W = 4
RG_C = 8.0
N_HEADS = 16
HEAD_DIM = 64
N_KV = 4
HPG = N_HEADS // N_KV
CMP_BLOCK = 64
N_SEL = 16
WINDOW = 512
PAGE_SIZE = 128
DN_ALPHA = (2.0 * DEPTH) ** 0.25
LN_EPS = 1e-5
NEG = -1e30
FORCED = 1e6
REMOVED = -3e38
GRP = 2 * HEAD_DIM
QGRP = HPG * HEAD_DIM
PAGE_PITCH = N_KV * GRP + 8
VMEM_LIMIT_BYTES = 56 * 1024 * 1024


def _params(sem):
    return pltpu.CompilerParams(dimension_semantics=sem, vmem_limit_bytes=VMEM_LIMIT_BYTES)


def _dot(a, b):
    return jnp.dot(a, b, preferred_element_type=f32)


def _dot_nt(a, b):
    return lax.dot_general(a, b, (((1,), (1,)), ((), ())), preferred_element_type=f32)


def _sigmoid(x):
    return 1.0 / (1.0 + jnp.exp(-x))


def _silu(x):
    return x * _sigmoid(x)


def _log1p(e):
    u = 1.0 + e
    dlt = u - 1.0
    return jnp.where(dlt == 0.0, e, jnp.log(u) * (e / jnp.where(dlt == 0.0, 1.0, dlt)))


def _layer_norm(x, g, b):
    mu = jnp.mean(x, axis=-1, keepdims=True)
    xc = x - mu
    var = jnp.mean(xc * xc, axis=-1, keepdims=True)
    return xc * lax.rsqrt(var + LN_EPS) * g + b


def _ada_kernel(c_ref, w_ref, b_ref, o_ref):
    a = _silu(c_ref[...])
    o_ref[0] = jnp.dot(a, w_ref[0], preferred_element_type=f32,
                       precision=lax.Precision.HIGHEST) + b_ref[0]


def _ada_mod(c_all, w_ada, b_ada):
    rows, d = c_all.shape
    depth = w_ada.shape[0]
    return pl.pallas_call(
        _ada_kernel,
        grid=(depth, 3),
        in_specs=[
            pl.BlockSpec((rows, d), lambda l, n: (0, 0)),
            pl.BlockSpec((1, d, d), lambda l, n: (l, 0, n)),
            pl.BlockSpec((1, 1, d), lambda l, n: (l, 0, n)),
        ],
        out_specs=pl.BlockSpec((1, rows, d), lambda l, n: (l, 0, n)),
        out_shape=jax.ShapeDtypeStruct((depth, rows, 3 * d), f32),
        compiler_params=_params(("arbitrary", "arbitrary")),
        name="ada_mod",
    )(c_all, w_ada, b_ada.reshape(depth, 1, 3 * d))


def _rglru_gates(xc, wg_ref, br, bi, lam):
    xcb = xc.astype(bf16)
    rg = xc.shape[1] // N_RG_BLOCKS
    rs, is_ = [], []
    for n in range(N_RG_BLOCKS):
        g = _dot(xcb[:, n * rg:(n + 1) * rg], wg_ref[n])
        rs.append(g[:, :rg])
        is_.append(g[:, rg:])
    r = _sigmoid(jnp.concatenate(rs, axis=1) + br)
    i = _sigmoid(jnp.concatenate(is_, axis=1) + bi)
    nl = -lam
    softplus = jnp.maximum(nl, 0.0) + _log1p(jnp.exp(-jnp.abs(nl)))
    log_a = (-RG_C * softplus) * r
    a = jnp.exp(log_a)
    gain = jnp.sqrt(jnp.maximum(-jnp.tanh(log_a) * (a * a + 1.0), 0.0))
    b = gain * i * xc
    return a, b


def _rglru_prompt_kernel(x_ref, mod_ref, h0_ref, c0_ref, win_ref, cw_ref, cb_ref, wg_ref, br_ref,
                         bi_ref, lam_ref, wout_ref, lng_ref, lnb_ref,
                         xo_ref, hl_ref, cl_ref, xbuf, a_s, b_s, hs_s, hc, *, tc, d):
    t = pl.program_id(1)

    @pl.when(t == 0)
    def _():
        xbuf[0:8, :] = jnp.zeros((8, d), f32)
        xbuf[8 - (CONV_W - 1):8, :] = c0_ref[0]
        hc[...] = jnp.broadcast_to(h0_ref[0], (8, d))

    x = x_ref[0]
    mod = mod_ref[0]
    shift, scale, gate = mod[0:1], mod[1:2], mod[2:3]
    m = x * (1.0 + scale) + shift
    u = _dot(m.astype(bf16), win_ref[...])
    xb = u[:, :d]
    zg = u[:, d:]
    xbuf[8:8 + tc, :] = xb
    base = 8 - (CONV_W - 1)
    xc = cb_ref[...] + xbuf[base:base + tc, :] * cw_ref[0:1, :]
    for k in range(1, CONV_W):
        xc = xc + xbuf[base + k:base + k + tc, :] * cw_ref[k:k + 1, :]
    tail = xbuf[8 + tc - (CONV_W - 1):8 + tc, :]
    xbuf[base:8, :] = tail
    cl_ref[0] = tail

    a, b = _rglru_gates(xc, wg_ref, br_ref[...], bi_ref[...], lam_ref[...])

    a = a.reshape(tc // 8, 8, d)
    b = b.reshape(tc // 8, 8, d)
    rowi = lax.broadcasted_iota(i32, (1, 8, 1), 1)
    for s in (1, 2, 4):
        ok = rowi >= s
        a_sh = pltpu.roll(a, s, 1)
        b_sh = pltpu.roll(b, s, 1)
        b = jnp.where(ok, a * b_sh + b, b)
        a = jnp.where(ok, a * a_sh, a)
    a_s[...] = a.reshape(tc, d)
    b_s[...] = b.reshape(tc, d)

    def tile_step(j, hprev):
        r0 = pl.multiple_of(j * 8, 8)
        ht = a_s[pl.ds(r0, 8), :] * hprev + b_s[pl.ds(r0, 8), :]
        hs_s[pl.ds(r0, 8), :] = ht
        return jnp.broadcast_to(ht[7:8, :], (8, d))

    hlast = lax.fori_loop(0, tc // 8, tile_step, hc[...])
    hc[...] = hlast
    hl_ref[0] = hlast[0:1, :]

    y = _dot((hs_s[...] * _silu(zg)).astype(bf16), wout_ref[...])
    xo_ref[0] = _layer_norm(DN_ALPHA * x + (1.0 + gate) * y, lng_ref[...], lnb_ref[...])


def _rglru_prompt(x, mod, h0, c0, w_in, conv_w, conv_b, w_gate, b_r, b_i, lam, w_out, ln_g, ln_b, tc):
    bsz, t, d = x.shape
    tc = min(tc, t)
    const2 = lambda b, i: (0, 0)
    const3 = lambda b, i: (0, 0, 0)
    perb = lambda b, i: (b, 0, 0)
    kern = functools.partial(_rglru_prompt_kernel, tc=tc, d=d)
    return pl.pallas_call(
        kern,
        grid=(bsz, t // tc),
        in_specs=[
            pl.BlockSpec((1, tc, d), lambda b, i: (b, i, 0)),
            pl.BlockSpec((1, 3, d), perb),
            pl.BlockSpec((1, 1, d), perb),
            pl.BlockSpec((1, CONV_W - 1, d), perb),
            pl.BlockSpec(w_in.shape, const2),
            pl.BlockSpec(conv_w.shape, const2),
            pl.BlockSpec((1, d), const2),
            pl.BlockSpec(w_gate.shape, const3),
            pl.BlockSpec((1, d), const2),
            pl.BlockSpec((1, d), const2),
            pl.BlockSpec((1, d), const2),
            pl.BlockSpec(w_out.shape, const2),
            pl.BlockSpec((1, d), const2),
            pl.BlockSpec((1, d), const2),
        ],
        out_specs=[
            pl.BlockSpec((1, tc, d), lambda b, i: (b, i, 0)),
            pl.BlockSpec((1, 1, d), perb),
            pl.BlockSpec((1, CONV_W - 1, d), perb),
        ],
        out_shape=[
            jax.ShapeDtypeStruct((bsz, t, d), f32),
            jax.ShapeDtypeStruct((bsz, 1, d), f32),
            jax.ShapeDtypeStruct((bsz, CONV_W - 1, d), f32),
        ],
        scratch_shapes=[
            pltpu.VMEM((tc + 8, d), f32),
            pltpu.VMEM((tc, d), f32),
            pltpu.VMEM((tc, d), f32),
            pltpu.VMEM((tc, d), f32),
            pltpu.VMEM((8, d), f32),
        ],
        compiler_params=_params(("arbitrary", "arbitrary")),
        name="rglru_prompt",
    )(x, mod, h0, c0, w_in, conv_w, conv_b, w_gate, b_r, b_i, lam, w_out, ln_g, ln_b)


def _rglru_sample_kernel(x_ref, mod_ref, h0_ref, c0_ref, win_ref, cw_ref, cb_ref, wg_ref, br_ref,
                         bi_ref, lam_ref, wout_ref, lng_ref, lnb_ref,
                         xo_ref, hl_ref, cl_ref, *, s_len, bsz, d):
    x = x_ref[...]
    mod = mod_ref[...]
    shift, scale, gate = mod[0:1], mod[1:2], mod[2:3]
    m = x * (1.0 + scale) + shift
    u = _dot(m.reshape(s_len * bsz, d).astype(bf16), win_ref[...])
    xb = u[:, :d].reshape(s_len, bsz, d)
    zg = u[:, d:]
    xp = jnp.concatenate([c0_ref[...], xb], axis=0)
    cw = cw_ref[...]
    xc = cb_ref[...] + xp[0:s_len] * cw[0:1]
    for k in range(1, CONV_W):
        xc = xc + xp[k:k + s_len] * cw[k:k + 1]
    cl_ref[...] = xp[s_len:s_len + CONV_W - 1]
    a, b = _rglru_gates(xc.reshape(s_len * bsz, d), wg_ref, br_ref[...], bi_ref[...], lam_ref[...])
    h = h0_ref[...]
    hs = []
    for s in range(s_len):
        h = a[s * bsz:(s + 1) * bsz] * h + b[s * bsz:(s + 1) * bsz]
        hs.append(h)
    hl_ref[...] = h
    hs = jnp.concatenate(hs, axis=0)
    y = _dot((hs * _silu(zg)).astype(bf16), wout_ref[...])
    xo = _layer_norm(DN_ALPHA * x + (1.0 + gate) * y.reshape(s_len, bsz, d), lng_ref[...], lnb_ref[...])
    xo_ref[...] = xo


def _rglru_sample(x_tm, mod_tm, h0, c0_tm, w_in, conv_w, conv_b, w_gate, b_r, b_i, lam, w_out, ln_g, ln_b):
    s_len, bsz, d = x_tm.shape
    kern = functools.partial(_rglru_sample_kernel, s_len=s_len, bsz=bsz, d=d)
    return pl.pallas_call(
        kern,
        out_shape=[
            jax.ShapeDtypeStruct((s_len, bsz, d), f32),
            jax.ShapeDtypeStruct((bsz, d), f32),
            jax.ShapeDtypeStruct((CONV_W - 1, bsz, d), f32),
        ],
        compiler_params=pltpu.CompilerParams(vmem_limit_bytes=VMEM_LIMIT_BYTES),
        name="rglru_sample",
    )(x_tm, mod_tm, h0, c0_tm, w_in, conv_w, conv_b, w_gate, b_r, b_i, lam, w_out, ln_g, ln_b)


def _proj_kernel(x_ref, mod_ref, wkv_ref, wq_ref, wz_ref, wg_ref, bg_ref,
                 cmp_ref, sel_ref, win_ref, selb_ref, winb_ref, q_ref, z_ref, gl_ref, *, nb, tr, d):
    x = x_ref[...]
    mod = mod_ref[...]
    m = x * (1.0 + mod[:, 1:2, :]) + mod[:, 0:1, :]
    xb = x.reshape(nb * tr, d).astype(bf16)
    mb = m.reshape(nb * tr, d).astype(bf16)
    kv = _dot(xb, wkv_ref[...])
    w = kv.shape[1] // 3
    sel = kv[:, w:2 * w]
    win = kv[:, 2 * w:]
    cmp_ref[...] = kv[:, :w]
    sel_ref[...] = sel
    win_ref[...] = win
    selb_ref[...] = sel.astype(bf16)
    winb_ref[...] = win.astype(bf16)
    q_ref[...] = (_dot(mb, wq_ref[...]) * (HEAD_DIM ** -0.5)).astype(bf16)
    z_ref[...] = _dot(mb, wz_ref[...])
    gl_ref[...] = _dot(mb, wg_ref[...]) + bg_ref[...]


def _proj(x, mod, w_kv, w_q, w_z, w_g, b_g, nb, tr):
    bsz, t, d = x.shape
    kvw = w_kv.shape[1] // 3
    hd = w_q.shape[1]
    gw = w_g.shape[1]
    nt = t // tr
    const2 = lambda b, i: (0, 0)
    blk = lambda width: pl.BlockSpec((nb * tr, width), lambda b, i: (b * nt + i, 0))
    out = lambda width, dt: jax.ShapeDtypeStruct((bsz * t, width), dt)
    kern = functools.partial(_proj_kernel, nb=nb, tr=tr, d=d)
    return pl.pallas_call(
        kern,
        grid=(bsz // nb, nt),
        in_specs=[
            pl.BlockSpec((nb, tr, d), lambda b, i: (b, i, 0)),
            pl.BlockSpec((nb, 3, d), lambda b, i: (b, 0, 0)),
            pl.BlockSpec(w_kv.shape, const2),
            pl.BlockSpec(w_q.shape, const2),
            pl.BlockSpec(w_z.shape, const2),
            pl.BlockSpec(w_g.shape, const2),
            pl.BlockSpec((1, gw), const2),
        ],
        out_specs=[blk(kvw), blk(kvw), blk(kvw), blk(kvw), blk(kvw), blk(hd), blk(hd), blk(gw)],
        out_shape=[out(kvw, f32), out(kvw, f32), out(kvw, f32), out(kvw, bf16), out(kvw, bf16),
                   out(hd, bf16), out(hd, f32), out(gw, f32)],
        compiler_params=_params(("arbitrary", "arbitrary")),
        name="nsa_proj",
    )(x, mod, w_kv, w_q, w_z, w_g, b_g)


def _proj_prompt_kernel(x_ref, mod_ref, wkvt_ref, wq_ref, wz_ref, wg_ref, bg_ref, e_ref, prow_ref, ones_ref,
                        cmpt_ref, selt_ref, wint_ref, cmppg_ref, kaug_ref, vaug_ref, wintt_ref, q_ref, z_ref,
                        gl_ref, *, tr):
    x = x_ref[0]
    mod = mod_ref[0]
    m = x * (1.0 + mod[1:2]) + mod[0:1]
    mb = m.astype(bf16)
    kvt = _dot_nt(wkvt_ref[...], x.astype(bf16))
    kvw = kvt.shape[0] // 3
    cmpt = kvt[:kvw]
    selt = kvt[kvw:2 * kvw]
    wint = kvt[2 * kvw:]
    cmpt_ref[0] = cmpt
    selt_ref[0] = selt
    wint_ref[0] = wint
    selb = selt.astype(bf16)
    n_c = e_ref.shape[1]
    for g in range(N_KV):
        kaug_ref[0, g, 0, 0:n_c] = e_ref[0]
        kaug_ref[0, g, 0, n_c:n_c + HEAD_DIM] = selb[g * GRP:g * GRP + HEAD_DIM]
        kaug_ref[0, g, 0, n_c + HEAD_DIM:] = prow_ref[...]
        vaug_ref[0, g, 0, 0:HEAD_DIM] = selb[g * GRP + HEAD_DIM:(g + 1) * GRP]
        vaug_ref[0, g, 0, HEAD_DIM:] = ones_ref[...]
    winb = wint.astype(bf16)
    for k in range(tr // PAGE_SIZE):
        cmppg_ref[k] = cmpt[:, k * PAGE_SIZE:(k + 1) * PAGE_SIZE]
        wintt_ref[0, k] = winb[:, k * PAGE_SIZE:(k + 1) * PAGE_SIZE]
    q_ref[...] = (_dot(mb, wq_ref[...]) * (HEAD_DIM ** -0.5)).astype(bf16)
    z_ref[...] = _dot(mb, wz_ref[...])
    gl_ref[...] = _dot(mb, wg_ref[...]) + bg_ref[...]


def _proj_prompt(x, mod, w_kvt, w_q, w_z, w_g, b_g, e, prow, ones, tr):
    bsz, t, d = x.shape
    kvw = w_kvt.shape[0] // 3
    hd = w_q.shape[1]
    gw = w_g.shape[1]
    nt = t // tr
    ppt = tr // PAGE_SIZE
    ka_rows = e.shape[1] + HEAD_DIM + prow.shape[0]
    va_rows = HEAD_DIM + ones.shape[0]
    const2 = lambda b, i: (0, 0)
    rows = lambda width: pl.BlockSpec((tr, width), lambda b, i: (b * nt + i, 0))
    tmin = pl.BlockSpec((1, kvw, tr), lambda b, i: (b, 0, i))
    kern = functools.partial(_proj_prompt_kernel, tr=tr)
    return pl.pallas_call(
        kern,
        grid=(bsz, nt),
        in_specs=[
            pl.BlockSpec((1, tr, d), lambda b, i: (b, i, 0)),
            pl.BlockSpec((1, 3, d), lambda b, i: (b, 0, 0)),
            pl.BlockSpec(w_kvt.shape, const2),
            pl.BlockSpec(w_q.shape, const2),
            pl.BlockSpec(w_z.shape, const2),
            pl.BlockSpec(w_g.shape, const2),
            pl.BlockSpec((1, gw), const2),
            pl.BlockSpec((1,) + e.shape[1:], lambda b, i: (i, 0, 0)),
            pl.BlockSpec(prow.shape, const2),
            pl.BlockSpec(ones.shape, const2),
        ],
        out_specs=[
            tmin, tmin, tmin,
            pl.BlockSpec((ppt, kvw, PAGE_SIZE), lambda b, i: (b * nt + i, 0, 0)),
            pl.BlockSpec((1, N_KV, 1, ka_rows, tr), lambda b, i: (b, 0, i, 0, 0)),
            pl.BlockSpec((1, N_KV, 1, va_rows, tr), lambda b, i: (b, 0, i, 0, 0)),
            pl.BlockSpec((1, ppt, kvw, PAGE_SIZE), lambda b, i: (b, i, 0, 0)),
            rows(hd), rows(hd), rows(gw),
        ],
        out_shape=[
            jax.ShapeDtypeStruct((bsz, kvw, t), f32),
            jax.ShapeDtypeStruct((bsz, kvw, t), f32),
            jax.ShapeDtypeStruct((bsz, kvw, t), f32),
            jax.ShapeDtypeStruct((bsz * t // PAGE_SIZE, kvw, PAGE_SIZE), f32),
            jax.ShapeDtypeStruct((bsz, N_KV, nt, ka_rows, tr), bf16),
            jax.ShapeDtypeStruct((bsz, N_KV, nt, va_rows, tr), bf16),
            jax.ShapeDtypeStruct((bsz, t // PAGE_SIZE, kvw, PAGE_SIZE), bf16),
            jax.ShapeDtypeStruct((bsz * t, hd), bf16),
            jax.ShapeDtypeStruct((bsz * t, hd), f32),
            jax.ShapeDtypeStruct((bsz * t, gw), f32),
        ],
        compiler_params=_params(("arbitrary", "arbitrary")),
        name="nsa_proj_prompt",
    )(x, mod, w_kvt, w_q, w_z, w_g, b_g, e, prow, ones)


def _compress_kernel(x_ref, pet_ref, w1_ref, b1_ref, w2_ref, b2_ref, kc_ref, vc_ref, *, npg):
    x_rows = x_ref.reshape(npg * PAGE_PITCH, x_ref.shape[2])

    def dim_rows(c, dd):
        return jnp.concatenate(
            [(x_rows[pl.ds((g * 2 + c) * HEAD_DIM + dd, npg, stride=PAGE_PITCH), :]
              + pet_ref[c, dd:dd + 1, :]).astype(bf16) for g in range(N_KV)], axis=0)

    outs = []
    for c in range(2):
        acc = None
        for dp in range(HEAD_DIM // 2):
            lhs = jnp.concatenate([dim_rows(c, 2 * dp), dim_rows(c, 2 * dp + 1)], axis=1)
            part = _dot(lhs, w1_ref[c, dp])
            acc = part if acc is None else acc + part
        hid = _silu(acc + b1_ref[c])
        outs.append(_dot(hid.astype(bf16), w2_ref[c]) + b2_ref[c])
    for g in range(N_KV):
        kc_ref[g] = outs[0][g * npg:(g + 1) * npg, :].astype(kc_ref.dtype)
        vc_ref[g] = outs[1][g * npg:(g + 1) * npg, :].astype(vc_ref.dtype)


def _compress(x2d, pet, w1, b1, w2, b2, npg, out_dtype):
    rows, width = x2d.shape
    n_pages = rows // (N_KV * GRP)
    npg = min(npg, n_pages)
    while n_pages % npg:
        npg -= 8
    ow = (PAGE_SIZE // CMP_BLOCK) * HEAD_DIM
    const3 = lambda i: (0, 0, 0)
    kern = functools.partial(_compress_kernel, npg=npg)
    out = pl.BlockSpec((N_KV, npg, ow), lambda i: (0, i, 0))
    return pl.pallas_call(
        kern,
        grid=(n_pages // npg,),
        in_specs=[
            pl.BlockSpec((npg, PAGE_PITCH, width), lambda i: (i, 0, 0)),
            pl.BlockSpec(pet.shape, const3),
            pl.BlockSpec(w1.shape, lambda i: (0, 0, 0, 0), pipeline_mode=pl.Buffered(1)),
            pl.BlockSpec(b1.shape, const3),
            pl.BlockSpec(w2.shape, const3),
            pl.BlockSpec(b2.shape, const3),
        ],
        out_specs=[out, out],
        out_shape=[jax.ShapeDtypeStruct((N_KV, n_pages, ow), out_dtype)] * 2,
        compiler_params=_params(("arbitrary",)),
        name="nsa_compress",
    )(x2d.reshape(n_pages, N_KV * GRP, width), pet, w1, b1, w2, b2)


def _block_ids(n_pages, lanes, axis):
    shape = (1, lanes) if axis == 1 else (lanes, 1)
    r = lax.broadcasted_iota(i32, shape, axis)
    bpp = PAGE_SIZE // CMP_BLOCK
    perm = jnp.where(r < n_pages, bpp * r, bpp * (r - n_pages) + 1)
    return jnp.where(r < bpp * n_pages, perm, r)


def _unpage(x):
    return jnp.concatenate([x[:, :HEAD_DIM], x[:, HEAD_DIM:]], axis=0)


def _cmp_branch(q4, kc, vc, slope, t_col, nq, jrow):
    s = _dot_nt(q4, kc)
    c_end = (jrow + 1) * CMP_BLOCK - 1
    dist = t_col.astype(f32) - c_end.astype(f32)
    s = s - slope * dist
    mask = c_end <= t_col
    s = jnp.where(mask, s, NEG)
    e = jnp.exp(s - jnp.max(s, axis=-1, keepdims=True))
    p = e / jnp.sum(e, axis=-1, keepdims=True)
    p = jnp.where(mask, p, 0.0)
    o = _dot(p.astype(bf16), vc)
    imp = p[0:nq]
    for h in range(1, HPG):
        imp = imp + p[h * nq:(h + 1) * nq]
    return o, imp


def _select_blocks(imp, tq_col, n_sb, jr):
    nq, lanes = imp.shape
    cb = tq_col // CMP_BLOCK
    forced = (jr == 0) | (jr == cb) | (jr == cb - 1)
    causal = jr <= cb
    score = jnp.where(forced, FORCED, jnp.where(causal, imp, -1.0))
    score = jnp.where(jr < n_sb, score, REMOVED)
    jf = jr.astype(f32)
    sel = jnp.zeros((nq, lanes), f32)
    for _ in range(min(N_SEL, n_sb)):
        mx = jnp.max(score, axis=-1, keepdims=True)
        idx = jnp.min(jnp.where(score == mx, jf, 1e9), axis=-1, keepdims=True)
        hit = jf == idx
        sel = jnp.where(hit, 1.0, sel)
        score = jnp.where(hit, REMOVED, score)
    return sel


def _flash_update(carry, q4, kt, vt, slope, valid, pos_rel):
    m_i, l_i, acc = carry
    s = _dot(q4, kt) + slope * pos_rel
    s = jnp.where(valid, s, NEG)
    m_new = jnp.maximum(m_i, jnp.max(s, axis=-1, keepdims=True))
    alpha = jnp.exp(m_i - m_new)
    p = jnp.exp(s - m_new)
    l_new = alpha * l_i + jnp.sum(p, axis=-1, keepdims=True)
    acc = alpha * acc + _dot_nt(p.astype(bf16), vt)
    return m_new, l_new, acc


def _block_mask(selb, jcol, pos, reps):
    onehot = (jcol == pos // CMP_BLOCK).astype(bf16)
    msk = _dot(selb, onehot)
    return jnp.concatenate([msk] * reps, axis=0) > 0.5


def _gate_mix(gl, o_cmp, o_sel, o_win, nq):
    gs = _sigmoid(gl)
    outs = []
    for h in range(HPG):
        sl = slice(h * nq, (h + 1) * nq)
        outs.append(gs[:, 3 * h:3 * h + 1] * o_cmp[sl] + gs[:, 3 * h + 1:3 * h + 2] * o_sel[sl]
                    + gs[:, 3 * h + 2:3 * h + 3] * o_win[sl])
    return jnp.concatenate(outs, axis=1)


def _init_carry(rows):
    return (jnp.full((rows, 1), NEG, f32), jnp.zeros((rows, 1), f32), jnp.zeros((rows, HEAD_DIM), f32))


def _select_blocks_t(impt, tq_row, n_sb, jcol, fillers=()):
    n_c, nq = impt.shape
    cb = tq_row // CMP_BLOCK
    forced = (jcol == 0) | (jcol == cb) | (jcol == cb - 1)
    causal = jcol <= cb
    score = jnp.where(forced, REMOVED, jnp.where(causal, impt, -1.0))
    score = jnp.where(jcol < n_sb, score, REMOVED)
    jf = jcol.astype(f32)
    sel = jnp.where(forced & (jcol < n_sb), 1.0, 0.0) + jnp.zeros((n_c, nq), f32)
    rounds = max(min(N_SEL, n_sb) - 3, 0)
    fillers = list(fillers)
    every = max(rounds // max(len(fillers), 1), 1)
    for it in range(rounds):
        mx = jnp.max(score, axis=0, keepdims=True)
        idx = jnp.min(jnp.where(score == mx, jf, 1e9), axis=0, keepdims=True)
        hit = jf == idx
        sel = jnp.where(hit, 1.0, sel)
        score = jnp.where(hit, REMOVED, score)
        if fillers and it % every == 0:
            fillers.pop(0)()
    for f in fillers:
        f()
    return sel


def _nsa_prompt_kernel(q_ref, gl_ref, kc_ref, vc_ref, kaug_ref, vaug_ref, kwin_ref, scol_ref, srow_ref, coef_ref,
                       wbias_ref, cbias_ref, eye_ref, tilemap_ref, o_ref,
                       s0_s, s1_s, p0_s, p1_s, a0_s, a1_s, m_s, acc_s, tiles_s, *, t_len, tq, tk, tkw):
    qi = pl.program_id(2)
    t0 = qi * tq
    rows = HPG * tq
    q = q_ref[0]
    q4 = jnp.concatenate([q[:, h * HEAD_DIM:(h + 1) * HEAD_DIM] for h in range(HPG)], axis=0)
    slope = scol_ref[0]
    t_col = t0 + lax.broadcasted_iota(i32, (rows, 1), 0) % tq

    n_c = t_len // CMP_BLOCK
    n_pages = t_len // PAGE_SIZE
    jrow = _block_ids(n_pages, n_c, 1)
    jcol = _block_ids(n_pages, n_c, 0)
    kc = _unpage(kc_ref[0]).astype(bf16)
    vc = _unpage(vc_ref[0]).astype(bf16)
    o_cmp_box = []

    def cmp_output():
        o_cmp_box.append(_cmp_branch(q4, kc, vc, slope, t_col, tq, jrow)[0])

    def init_state():
        m_s[...] = jnp.full((rows, 1), NEG, f32)
        acc_s[...] = jnp.zeros(acc_s.shape, f32)
        p1_s[...] = jnp.zeros(p1_s.shape, bf16)
        a1_s[...] = jnp.ones((rows, 1), f32)

    nslots = (WINDOW + tq) // tkw
    base = (t0 - WINDOW) // tkw
    kts, vts, pens = [], [], []
    for m in range(nslots):
        kw = base + m
        kvw = kwin_ref[0, jnp.maximum(kw, 0)]
        kts.append(kvw[:HEAD_DIM])
        vts.append(kvw[HEAD_DIM:])
        pens.append(jnp.broadcast_to(jnp.where(kw >= 0, 0.0, NEG).astype(f32), (1, tkw)))
    sw = _dot(q4, jnp.concatenate(kts, axis=1))
    pen = jnp.concatenate(pens, axis=1)
    pw_h, lw_h = [], []

    def window_softmax(h):
        def run():
            s_h = sw[h * tq:(h + 1) * tq] + wbias_ref[0, h * tq:(h + 1) * tq] + pen
            p_h = jnp.exp(s_h - jnp.max(s_h, axis=-1, keepdims=True))
            pw_h.append(p_h.astype(bf16))
            lw_h.append(jnp.sum(p_h, axis=-1, keepdims=True))
        return run

    t_row = t0 + lax.broadcasted_iota(i32, (1, rows), 1) % tq
    c_end = (jcol + 1) * CMP_BLOCK - 1
    st = _dot_nt(kc, q4) - srow_ref[0] * (t_row.astype(f32) - c_end.astype(f32))
    maskt = c_end <= t_row
    st = jnp.where(maskt, st, NEG)
    et = jnp.exp(st - jnp.max(st, axis=0, keepdims=True))
    pt = et / jnp.sum(et, axis=0, keepdims=True)
    pt = jnp.where(maskt, pt, 0.0)
    impt = pt[:, 0:tq]
    for h in range(1, HPG):
        impt = impt + pt[:, h * tq:(h + 1) * tq]
    selt = _select_blocks_t(impt, t0 + lax.broadcasted_iota(i32, (1, tq), 1), n_c, jcol,
                            fillers=[window_softmax(h) for h in range(HPG)] + [cmp_output, init_state])
    o_cmp = o_cmp_box[0]
    sel = _dot_nt(eye_ref[...], selt.astype(bf16))
    selbias = ((sel - 1.0) * -NEG).astype(bf16)
    q_aug = jnp.concatenate([jnp.concatenate([selbias] * HPG, axis=0), q4, coef_ref[0]], axis=1)


    def scores_into(dst, kt):
        dst[...] = _dot(q_aug, kaug_ref[0, 0, kt])

    def softmax_into(s, kt, p_dst, a_dst):
        c = slope * (kt * tk - t0).astype(f32)
        m_i = m_s[...]
        m_new = jnp.maximum(m_i, jnp.max(s, axis=-1, keepdims=True) + c)
        p_dst[...] = jnp.exp(s - (m_new - c)).astype(bf16)
        a_dst[...] = jnp.exp(m_i - m_new)
        m_s[...] = m_new

    def values_from(p_src, a_src, kt):
        acc_s[...] = a_src[...] * acc_s[...] + _dot_nt(p_src[...], vaug_ref[0, 0, kt])

    kd = t0 // tk
    tile_hits = jnp.max(_dot(tilemap_ref[...], selt.astype(bf16)), axis=1, keepdims=True)
    n_act = jnp.int32(0)
    tiles_s[0] = 0
    for kt in range(t_len // tk):
        tiles_s[n_act] = kt
        hit = (jnp.max(tile_hits[kt:kt + 1, :]) > 0.5) & (kt < kd)
        n_act = n_act + hit.astype(i32)
    npairs = n_act // 2
    last_piped = jnp.maximum(2 * npairs - 1, 0)
    scores_into(s0_s, tiles_s[0])

    def pair(j, carry):
        ia = 2 * j
        ka = tiles_s[ia]
        kb = tiles_s[ia + 1]
        values_from(p1_s, a1_s, tiles_s[jnp.maximum(ia - 1, 0)])
        scores_into(s1_s, kb)
        softmax_into(s0_s[...], ka, p0_s, a0_s)
        values_from(p0_s, a0_s, ka)
        scores_into(s0_s, tiles_s[jnp.minimum(ia + 2, last_piped)])
        softmax_into(s1_s[...], kb, p1_s, a1_s)
        return carry

    lax.fori_loop(0, npairs, pair, 0)
    values_from(p1_s, a1_s, tiles_s[last_piped])

    def tile_plain(kt, bias):
        s = _dot(q_aug, kaug_ref[0, 0, kt])
        if bias is not None:
            s = s + bias
        softmax_into(s, kt, p0_s, a0_s)
        values_from(p0_s, a0_s, kt)

    @pl.when(n_act % 2 == 1)
    def _():
        tile_plain(tiles_s[jnp.maximum(n_act - 1, 0)], None)

    s_d = _dot(q_aug, kaug_ref[0, 0, kd]) + jnp.concatenate([cbias_ref[(t0 % tk) // tq]] * HPG, axis=0)
    o_win = (_dot_nt(jnp.concatenate(pw_h, axis=0), jnp.concatenate(vts, axis=1))
             / jnp.concatenate(lw_h, axis=0))
    softmax_into(s_d, kd, p0_s, a0_s)
    values_from(p0_s, a0_s, kd)
    acc = acc_s[...]
    o_sel = acc[:, :HEAD_DIM] / acc[:, HEAD_DIM:HEAD_DIM + 1]

    o_ref[0] = _gate_mix(gl_ref[0], o_cmp, o_sel, o_win, tq)


def _prompt_consts(t_len, tq, tk, tkw):
    n_c = t_len // CMP_BLOCK
    n_pages = t_len // PAGE_SIZE
    rows = HPG * tq
    r = np.arange(n_c)
    bpp = PAGE_SIZE // CMP_BLOCK
    ids = np.where(r < n_pages, bpp * r, bpp * (r - n_pages) + 1)
    onehot = ids[:, None] == (np.arange(t_len) // CMP_BLOCK)[None, :]
    e = onehot.reshape(n_c, t_len // tk, tk).transpose(1, 0, 2)
    lane = np.arange(tk)
    prow = np.zeros((HEAD_DIM, tk), np.float32)
    prow[0:3] = lane // 256
    prow[3:6] = lane % 256
    slopes = (2.0 ** (-8.0 * np.arange(1, N_HEADS + 1) / N_HEADS)).astype(np.float32)
    srow = np.repeat(slopes.reshape(N_KV, HPG), tq, axis=1)
    s1 = srow.astype(jnp.bfloat16).astype(np.float32)
    s2 = (srow - s1).astype(jnp.bfloat16).astype(np.float32)
    s3 = (srow - s1 - s2).astype(jnp.bfloat16).astype(np.float32)
    coef = np.zeros((N_KV, rows, HEAD_DIM), np.float32)
    for k, piece in enumerate((s1, s2, s3)):
        coef[:, :, k] = 256.0 * piece
        coef[:, :, 3 + k] = piece
    a = np.tile(np.arange(tq), HPG)[:, None]
    nslots = (WINDOW + tq) // tkw
    dpos = a + WINDOW - np.arange(nslots * tkw)[None, :]
    band = np.where((dpos >= 0) & (dpos <= WINDOW), 0.0, NEG)
    wbias = -srow[:, :, None] * dpos[None] + band[None]
    off = np.arange(tk // tq)[:, None, None] * tq
    cbias = np.where(lane[None, None, :] <= off + np.arange(tq)[None, :, None], 0.0, NEG)
    n_tiles = t_len // tk
    tilemap = np.arange(-(-n_tiles // 16) * 16)[:, None] == (ids // (tk // CMP_BLOCK))[None, :]
    return dict(
        tilemap=jnp.asarray(tilemap, bf16),
        scol=jnp.asarray(srow.reshape(N_KV, rows, 1), f32), srow=jnp.asarray(srow.reshape(N_KV, 1, rows), f32),
        coef=jnp.asarray(coef, bf16), e=jnp.asarray(e, bf16), prow=jnp.asarray(prow, bf16),
        ones=jnp.ones((16, tk), bf16), wbias=jnp.asarray(wbias, f32), cbias=jnp.asarray(cbias, f32),
        eye=jnp.eye(tq, dtype=bf16))


def _nsa_prompt(q, gl, kc, vc, kaug, vaug, kwin, c, tq):
    bsz, t_len, _ = q.shape
    tk = kaug.shape[-1]
    tkw = kwin.shape[-1]
    npb = t_len // PAGE_SIZE
    rows = HPG * tq
    kern = functools.partial(_nsa_prompt_kernel, t_len=t_len, tq=tq, tk=tk, tkw=tkw)
    augspec = lambda arr: pl.BlockSpec((1, 1) + arr.shape[2:], lambda b, g, i: (b, g, 0, 0, 0))
    cspec = pl.BlockSpec((1, npb, kc.shape[-1]), lambda b, g, i: (g, b, 0))
    perg = lambda arr: pl.BlockSpec((1,) + arr.shape[1:], lambda b, g, i: (g,) + (0,) * (arr.ndim - 1))
    whole = lambda arr: pl.BlockSpec(arr.shape, lambda b, g, i: (0,) * arr.ndim)
    return pl.pallas_call(
        kern,
        grid=(bsz, N_KV, t_len // tq),
        in_specs=[
            pl.BlockSpec((1, tq, QGRP), lambda b, g, i: (b, i, g)),
            pl.BlockSpec((1, tq, 128), lambda b, g, i: (b, i, g)),
            cspec, cspec,
            augspec(kaug), augspec(vaug),
            pl.BlockSpec((1, t_len // tkw, GRP, tkw), lambda b, g, i: (b, 0, g, 0)),
            perg(c["scol"]), perg(c["srow"]), perg(c["coef"]),
            perg(c["wbias"]), whole(c["cbias"]), whole(c["eye"]), whole(c["tilemap"]),
        ],
        out_specs=pl.BlockSpec((1, tq, QGRP), lambda b, g, i: (b, i, g)),
        out_shape=jax.ShapeDtypeStruct((bsz, t_len, N_KV * QGRP), f32),
        scratch_shapes=[
            pltpu.VMEM((rows, tk), f32), pltpu.VMEM((rows, tk), f32),
            pltpu.VMEM((rows, tk), bf16), pltpu.VMEM((rows, tk), bf16),
            pltpu.VMEM((rows, 1), f32), pltpu.VMEM((rows, 1), f32),
            pltpu.VMEM((rows, 1), f32), pltpu.VMEM((rows, vaug.shape[3]), f32),
            pltpu.SMEM((t_len // tk + 8,), i32),
        ],
        compiler_params=_params(("arbitrary", "arbitrary", "arbitrary")),
        name="nsa_prompt_attn",
    )(q, gl, kc, vc, kaug, vaug, kwin, c["scol"], c["srow"], c["coef"], c["wbias"], c["cbias"], c["eye"],
      c["tilemap"])


def _nsa_sample_kernel(pt_ref, q_ref, gl_ref, kcall_ref, vcall_ref, pid_ref, newsel_ref, swin_ref, nwin_ref,
                       slope_ref, *rest, s_len, past_len, pp, n_groups, lanes):
    page_refs = rest[:pp]
    o_ref = rest[pp]
    sel_s, ocmp_s, qbd_s, m_s, l_s, acc_s = rest[pp + 1:]
    j = pl.program_id(1)
    rows = HPG * s_len
    n_c = past_len // CMP_BLOCK
    npb = past_len // PAGE_SIZE
    jrow = _block_ids(npb, lanes, 1)
    jcol = _block_ids(npb, lanes, 0)
    q = q_ref[0].astype(f32)
    t_col = past_len + lax.broadcasted_iota(i32, (rows, 1), 0) % s_len
    tq_col = past_len + lax.broadcasted_iota(i32, (s_len, 1), 0)

    def q_of(g):
        return jnp.concatenate(
            [q[:, (g * HPG + h) * HEAD_DIM:(g * HPG + h + 1) * HEAD_DIM] for h in range(HPG)],
            axis=0).astype(bf16)

    groups = range(N_KV)
    all_rows = N_KV * rows
    slope_all = jnp.concatenate([slope_ref[g] for g in groups], axis=0)
    t_all = past_len + lax.broadcasted_iota(i32, (all_rows, 1), 0) % s_len

    @pl.when(j == 0)
    def _():
        n_all = kcall_ref.shape[1]
        gather = (lax.broadcasted_iota(i32, (npb, n_all), 1) == pid_ref[0]).astype(bf16)
        zeros = jnp.zeros((rows, HEAD_DIM), bf16)
        imps = []
        for g in groups:
            kc = _unpage(_dot(gather, kcall_ref[g])).astype(bf16)
            vc = _unpage(_dot(gather, vcall_ref[g])).astype(bf16)
            q4 = q_of(g)
            o_cmp, imp = _cmp_branch(q4, kc, vc, slope_ref[g], t_col, s_len, jrow[:, :n_c])
            imps.append(jnp.concatenate([imp, jnp.zeros((s_len, lanes - n_c), f32)], axis=1))
            ocmp_s[g] = o_cmp
            qbd_s[g * rows:(g + 1) * rows] = jnp.concatenate([q4 if gg == g else zeros for gg in groups], axis=1)
        sel_s[...] = _select_blocks(jnp.concatenate(imps, axis=0), jnp.concatenate([tq_col] * N_KV, axis=0),
                                    n_c + 1, jrow)
        m_s[...] = jnp.full((all_rows, 1), NEG, f32)
        l_s[...] = jnp.zeros((all_rows, 1), f32)
        acc_s[...] = jnp.zeros(acc_s.shape, f32)

    def stack(keys):
        kt = jnp.concatenate([keys[g * GRP:g * GRP + HEAD_DIM] for g in groups], axis=0).astype(bf16)
        vt = jnp.concatenate([keys[g * GRP + HEAD_DIM:(g + 1) * GRP] for g in groups], axis=0).astype(bf16)
        return kt, vt

    def attend(keys, pos):
        onehot = (jcol == pos // CMP_BLOCK).astype(bf16)
        msk = _dot(sel_s[...].astype(bf16), onehot)
        msk = jnp.concatenate([msk[g * s_len:(g + 1) * s_len] for g in groups for _ in range(HPG)], axis=0)
        valid = (msk > 0.5) & (pos <= t_all)
        kt, vt = stack(keys)
        m_new, l_new, acc = _flash_update((m_s[...], l_s[...], acc_s[...]), qbd_s[...], kt, vt, slope_all,
                                          valid, (pos - past_len).astype(f32))
        m_s[...] = m_new
        l_s[...] = l_new
        acc_s[...] = acc

    nk = pp * PAGE_SIZE
    keys = jnp.concatenate([r[0] for r in page_refs], axis=1)
    attend(keys, j * nk + lax.broadcasted_iota(i32, (1, nk), 1))

    @pl.when(j == n_groups - 1)
    def _():
        new_keys = newsel_ref[0]
        attend(new_keys, past_len + lax.broadcasted_iota(i32, (1, new_keys.shape[1]), 1))
        o_sel = acc_s[...] / l_s[...]
        carry = (jnp.full((all_rows, 1), NEG, f32), jnp.zeros((all_rows, 1), f32),
                 jnp.zeros((all_rows, N_KV * HEAD_DIM), f32))
        for wkeys, wbase in ((swin_ref[0], past_len - swin_ref.shape[2]), (nwin_ref[0], past_len)):
            wpos = wbase + lax.broadcasted_iota(i32, (1, wkeys.shape[1]), 1)
            dpos = t_all - wpos
            valid = (dpos >= 0) & (dpos <= WINDOW)
            kt, vt = stack(wkeys)
            carry = _flash_update(carry, qbd_s[...], kt, vt, slope_all, valid, (wpos - past_len).astype(f32))
        o_win = carry[2] / carry[1]
        own = lambda o, g: o[g * rows:(g + 1) * rows, g * HEAD_DIM:(g + 1) * HEAD_DIM]
        gl = gl_ref[0]
        o_ref[0] = jnp.concatenate(
            [_gate_mix(gl[:, g * 128:(g + 1) * 128], ocmp_s[g], own(o_sel, g), own(o_win, g), s_len)
             for g in groups], axis=1)


def _nsa_sample(page_table, q, gl, kc_all, vc_all, pids, new_sel, state_win, new_win, slopes, cache_sel, pp):
    bsz, s_len, hd = q.shape
    n_pages = page_table.shape[1]
    past_len = n_pages * PAGE_SIZE
    pp = min(pp, n_pages)
    n_groups = n_pages // pp
    n_c = past_len // CMP_BLOCK
    lanes = -(-(n_c + 2) // 128) * 128
    rows = HPG * s_len
    kvw = cache_sel.shape[1]
    perb = lambda b, j, pt: (b, 0, 0)

    def page_spec(i):
        return pl.BlockSpec((1, kvw, PAGE_SIZE), lambda b, j, pt: (pt[b, j * pp + i], 0, 0))

    kern = functools.partial(_nsa_sample_kernel, s_len=s_len, past_len=past_len, pp=pp,
                             n_groups=n_groups, lanes=lanes)
    grid_spec = pltpu.PrefetchScalarGridSpec(
        num_scalar_prefetch=1,
        grid=(bsz, n_groups),
        in_specs=[
            pl.BlockSpec((1, s_len, hd), perb),
            pl.BlockSpec((1, s_len, gl.shape[-1]), perb),
            pl.BlockSpec(kc_all.shape, lambda b, j, pt: (0, 0, 0)),
            pl.BlockSpec(vc_all.shape, lambda b, j, pt: (0, 0, 0)),
            pl.BlockSpec((1, n_pages, 1), perb),
            pl.BlockSpec((1,) + new_sel.shape[1:], perb),
            pl.BlockSpec((1,) + state_win.shape[1:], perb),
            pl.BlockSpec((1,) + new_win.shape[1:], perb),
            pl.BlockSpec(slopes.shape, lambda b, j, pt: (0, 0, 0)),
        ] + [page_spec(i) for i in range(pp)],
        out_specs=pl.BlockSpec((1, s_len, hd), perb),
        scratch_shapes=[
            pltpu.VMEM((N_KV * s_len, lanes), f32),
            pltpu.VMEM((N_KV, rows, HEAD_DIM), f32),
            pltpu.VMEM((N_KV * rows, N_KV * HEAD_DIM), bf16),
            pltpu.VMEM((N_KV * rows, 1), f32),
            pltpu.VMEM((N_KV * rows, 1), f32),
            pltpu.VMEM((N_KV * rows, N_KV * HEAD_DIM), f32),
        ],
    )
    return pl.pallas_call(
        kern,
        grid_spec=grid_spec,
        out_shape=jax.ShapeDtypeStruct((bsz, s_len, hd), f32),
        compiler_params=_params(("arbitrary", "arbitrary")),
        name="nsa_sample_attn",
    )(page_table, q, gl, kc_all, vc_all, pids, new_sel, state_win, new_win, slopes, *([cache_sel] * pp))


def _out_kernel(o_ref, z_ref, x_ref, mod_ref, w_ref, lng_ref, lnb_ref, xo_ref, *, nb, tr, d):
    y = _dot((o_ref[...] * _silu(z_ref[...])).astype(bf16), w_ref[...]).reshape(nb, tr, d)
    gate = mod_ref[...][:, 2:3, :]
    xo_ref[...] = _layer_norm(DN_ALPHA * x_ref[...] + (1.0 + gate) * y, lng_ref[...], lnb_ref[...])


def _out_proj(o, z, x, mod, w_out, ln_g, ln_b, nb, tr):
    bsz, t, d = x.shape
    hd = o.shape[-1]
    nt = t // tr
    const2 = lambda b, i: (0, 0)
    kern = functools.partial(_out_kernel, nb=nb, tr=tr, d=d)
    return pl.pallas_call(
        kern,
        grid=(bsz // nb, nt),
        in_specs=[
            pl.BlockSpec((nb * tr, hd), lambda b, i: (b * nt + i, 0)),
            pl.BlockSpec((nb * tr, hd), lambda b, i: (b * nt + i, 0)),
            pl.BlockSpec((nb, tr, d), lambda b, i: (b, i, 0)),
            pl.BlockSpec((nb, 3, d), lambda b, i: (b, 0, 0)),
            pl.BlockSpec(w_out.shape, const2),
            pl.BlockSpec((1, d), const2),
            pl.BlockSpec((1, d), const2),
        ],
        out_specs=pl.BlockSpec((nb, tr, d), lambda b, i: (b, i, 0)),
        out_shape=jax.ShapeDtypeStruct((bsz, t, d), f32),
        compiler_params=_params(("arbitrary", "arbitrary")),
        name="nsa_out",
    )(o, z, x, mod, w_out, ln_g, ln_b)


def _head_slopes(nq):
    s = 2.0 ** (-8.0 * np.arange(1, N_HEADS + 1) / N_HEADS)
    s = np.repeat(s.reshape(N_KV, HPG), nq, axis=1)
    return jnp.asarray(s.reshape(N_KV, HPG * nq, 1), f32)


def kernel(x_prompt, x_sample, c_prompt, c_sample, state_h, state_conv, cache_cmp, cache_sel, state_win, page_table, w_ada, b_ada, ln_g, ln_b, w_in_a, conv_w_a, conv_b_a, w_r_a, b_r_a, w_i_a, b_i_a, lam_a, w_out_a, w_kv, phi_pe, w_phi1, b_phi1, w_phi2, b_phi2, w_in_b, b_gate_b, w_out_b):
    bp, t_len, d = x_prompt.shape
    bs, s_len, _ = x_sample.shape
    n_pages = page_table.shape[1]
    past_len = n_pages * PAGE_SIZE
    hd = N_HEADS * HEAD_DIM
    kvw = N_KV * GRP
    assert s_len <= CMP_BLOCK and t_len % 128 == 0 and w_ada.shape[0] == DEPTH == 2

    n_c = bp + bs
    pad = -n_c % 8
    c_all = jnp.concatenate([c_prompt, c_sample, jnp.zeros((pad, d), f32)], axis=0)
    mods = _ada_mod(c_all, w_ada, b_ada).reshape(DEPTH, n_c + pad, 3, d)
    mod_p = mods[:, :bp]
    mod_s = mods[:, bp:n_c]

    rg = d // N_RG_BLOCKS
    w_in0 = w_in_a[0].astype(bf16)
    w_gate = jnp.concatenate([w_r_a[0], w_i_a[0]], axis=-1).astype(bf16)
    w_out0 = w_out_a[0].astype(bf16)
    row = lambda v: v.reshape(1, -1)
    rg_args = (w_in0, conv_w_a[0], row(conv_b_a[0]), w_gate, row(b_r_a[0]), row(b_i_a[0]), row(lam_a[0]),
               w_out0, row(ln_g[0]), row(ln_b[0]))
    xp1, h_p, conv_p = _rglru_prompt(x_prompt, mod_p[0], jnp.zeros((bp, 1, d), f32),
                                     jnp.zeros((bp, CONV_W - 1, d), f32), *rg_args, tc=256)
    xs1_tm, h_s, conv_s_tm = _rglru_sample(
        x_sample.transpose(1, 0, 2), mod_s[0].transpose(1, 0, 2), state_h[0],
        state_conv[0].transpose(1, 0, 2), *rg_args)
    xs1 = xs1_tm.transpose(1, 0, 2)
    new_h_p = h_p.reshape(1, bp, d)
    new_conv_p = conv_p.reshape(1, bp, CONV_W - 1, d)
    new_h_s = h_s.reshape(1, bs, d)
    new_conv_s = conv_s_tm.transpose(1, 0, 2).reshape(1, bs, CONV_W - 1, d)

    w_b = w_in_b[0]
    w_q = w_b[:, :hd].astype(bf16)
    w_z = w_b[:, hd:2 * hd].astype(bf16)
    gpg = HPG * 3
    w_g = jnp.pad(w_b[:, 2 * hd:].reshape(d, N_KV, gpg), ((0, 0), (0, 0), (0, 128 - gpg)))
    w_g = w_g.reshape(d, N_KV * 128).astype(bf16)
    b_g = jnp.pad(b_gate_b[0].reshape(N_KV, gpg), ((0, 0), (0, 128 - gpg))).reshape(1, N_KV * 128)
    tmin = lambda v: v.transpose(0, 2, 3, 4, 1)
    tq = min(128, t_len)
    tk = min(512, t_len)
    consts = _prompt_consts(t_len, tq, tk, PAGE_SIZE)
    cmpt_p, selt_p, wint_p, cmp_pages, kaug_p, vaug_p, wint_tiles, q_p, z_p, gl_p = _proj_prompt(
        xp1, mod_p[1], w_kv.T.astype(bf16), w_q, w_z, w_g, b_g, consts["e"], consts["prow"], consts["ones"],
        tr=tk)
    cmp_s, sel_s, win_s, selb_s, winb_s, q_s, z_s, gl_s = _proj(
        xs1, mod_s[1], w_kv.astype(bf16), w_q, w_z, w_g, b_g, nb=bs, tr=s_len)
    per_p = lambda v: v.reshape(bp, t_len, v.shape[-1])
    per_s = lambda v: v.reshape(bs, s_len, v.shape[-1])

    bpp = PAGE_SIZE // CMP_BLOCK
    dphi = w_phi2.shape[1]
    diag2 = lambda w: jnp.concatenate([jnp.concatenate([w, jnp.zeros_like(w)], axis=-1),
                                       jnp.concatenate([jnp.zeros_like(w), w], axis=-1)], axis=-2)
    pet = jnp.tile(phi_pe.transpose(1, 2, 0), (1, 1, bpp))
    w1 = diag2(w_phi1.transpose(0, 2, 1, 3)).astype(bf16)
    w1 = w1.reshape(2, HEAD_DIM // 2, 2 * PAGE_SIZE, bpp * dphi)
    b1 = jnp.tile(b_phi1, (1, bpp)).reshape(2, 1, bpp * dphi)
    w2 = diag2(w_phi2).astype(bf16)
    b2 = jnp.tile(b_phi2, (1, bpp)).reshape(2, 1, bpp * HEAD_DIM)
    kc_p, vc_p = _compress(cmp_pages.reshape(-1, PAGE_SIZE), pet, w1, b1, w2, b2, npg=64, out_dtype=f32)
    kc_all, vc_all = _compress(tmin(cache_cmp).reshape(-1, PAGE_SIZE), pet, w1, b1, w2, b2,
                               npg=64, out_dtype=bf16)

    o_p = _nsa_prompt(per_p(q_p), per_p(gl_p), kc_p, vc_p, kaug_p, vaug_p, wint_tiles, consts, tq=tq)
    tpad = lambda v: jnp.pad(per_s(v), ((0, 0), (0, PAGE_SIZE - s_len), (0, 0))).transpose(0, 2, 1)
    o_s = _nsa_sample(page_table, per_s(q_s), per_s(gl_s), kc_all, vc_all, page_table.reshape(bs, n_pages, 1),
                      tpad(selb_s), tmin(state_win).reshape(bs, kvw, -1), tpad(winb_s), _head_slopes(s_len),
                      tmin(cache_sel).reshape(-1, kvw, PAGE_SIZE), pp=16)

    w_out1 = w_out_b[0].astype(bf16)
    y_p = _out_proj(o_p.reshape(bp * t_len, hd), z_p, xp1, mod_p[1], w_out1, row(ln_g[1]), row(ln_b[1]),
                    nb=1, tr=min(512, t_len))
    y_s = _out_proj(o_s.reshape(bs * s_len, hd), z_s, xs1, mod_s[1], w_out1, row(ln_g[1]), row(ln_b[1]),
                    nb=bs, tr=s_len)

    kv5 = lambda v: v.reshape(bs, s_len, N_KV, 2, HEAD_DIM)
    kv5t = lambda v: v.reshape(bp, N_KV, 2, HEAD_DIM, -1).transpose(0, 4, 1, 2, 3)
    wb = state_win.shape[1]
    new_win_p = kv5t(wint_p[:, :, -min(WINDOW, t_len):])
    new_win_s = jnp.concatenate([state_win, kv5(win_s)], axis=1)[:, -wb:]
    return (y_p, y_s, kv5t(cmpt_p), kv5t(selt_p), new_win_p, new_h_p, new_conv_p,
            kv5(cmp_s), kv5(sel_s), new_win_s, new_h_s, new_conv_s)
```

```python
import functools
import math

import numpy as np
import jax
import jax.numpy as jnp
from jax import lax
from jax.experimental import pallas as pl
from jax.experimental.pallas import tpu as pltpu

f32 = jnp.float32
bf16 = jnp.bfloat16
i32 = jnp.int32

DEPTH = 2
N_RG_BLOCKS = 8
CONV_W = 4
RG_C = 8.0
N_HEADS = 16
HEAD_DIM = 64
N_KV = 4
HPG = N_HEADS // N_KV
CMP_BLOCK = 64
N_SEL = 16
WINDOW = 512
PAGE_SIZE = 128
DN_ALPHA = (2.0 * DEPTH) ** 0.25
LN_EPS = 1e-5
NEG = -1e30
FORCED = 1e6
REMOVED = -3e38
GRP = 2 * HEAD_DIM
QGRP = HPG * HEAD_DIM
PAGE_PITCH = N_KV * GRP + 8
VMEM_LIMIT_BYTES = 56 * 1024 * 1024


def _params(sem):
    return pltpu.CompilerParams(dimension_semantics=sem, vmem_limit_bytes=VMEM_LIMIT_BYTES)


def _dot(a, b):
    return jnp.dot(a, b, preferred_element_type=f32)


def _dot_nt(a, b):
    return lax.dot_general(a, b, (((1,), (1,)), ((), ())), preferred_element_type=f32)


def _sigmoid(x):
    return 1.0 / (1.0 + jnp.exp(-x))


def _silu(x):
    return x * _sigmoid(x)


def _log1p(e):
    u = 1.0 + e
    dlt = u - 1.0
    return jnp.where(dlt == 0.0, e, jnp.log(u) * (e / jnp.where(dlt == 0.0, 1.0, dlt)))


def _layer_norm(x, g, b):
    mu = jnp.mean(x, axis=-1, keepdims=True)
    xc = x - mu
    var = jnp.mean(xc * xc, axis=-1, keepdims=True)
    return xc * lax.rsqrt(var + LN_EPS) * g + b


def _ada_kernel(c_ref, w_ref, b_ref, o_ref):
    a = _silu(c_ref[...])
    o_ref[0] = jnp.dot(a, w_ref[0], preferred_element_type=f32,
                       precision=lax.Precision.HIGHEST) + b_ref[0]


def _ada_mod(c_all, w_ada, b_ada):
    rows, d = c_all.shape
    depth = w_ada.shape[0]
    return pl.pallas_call(
        _ada_kernel,
        grid=(depth, 3),
        in_specs=[
            pl.BlockSpec((rows, d), lambda l, n: (0, 0)),
            pl.BlockSpec((1, d, d), lambda l, n: (l, 0, n)),
            pl.BlockSpec((1, 1, d), lambda l, n: (l, 0, n)),
        ],
        out_specs=pl.BlockSpec((1, rows, d), lambda l, n: (l, 0, n)),
        out_shape=jax.ShapeDtypeStruct((depth, rows, 3 * d), f32),
        compiler_params=_params(("arbitrary", "arbitrary")),
        name="ada_mod",
    )(c_all, w_ada, b_ada.reshape(depth, 1, 3 * d))


def _rglru_gates(xc, wg_ref, br, bi, lam):
    xcb = xc.astype(bf16)
    rg = xc.shape[1] // N_RG_BLOCKS
    rs, is_ = [], []
    for n in range(N_RG_BLOCKS):
        g = _dot(xcb[:, n * rg:(n + 1) * rg], wg_ref[n])
        rs.append(g[:, :rg])
        is_.append(g[:, rg:])
    r = _sigmoid(jnp.concatenate(rs, axis=1) + br)
    i = _sigmoid(jnp.concatenate(is_, axis=1) + bi)
    nl = -lam
    softplus = jnp.maximum(nl, 0.0) + _log1p(jnp.exp(-jnp.abs(nl)))
    log_a = (-RG_C * softplus) * r
    a = jnp.exp(log_a)
    gain = jnp.sqrt(jnp.maximum(-jnp.tanh(log_a) * (a * a + 1.0), 0.0))
    b = gain * i * xc
    return a, b


def _rglru_prompt_kernel(x_ref, mod_ref, h0_ref, c0_ref, win_ref, cw_ref, cb_ref, wg_ref, br_ref,
                         bi_ref, lam_ref, wout_ref, lng_ref, lnb_ref,
                         xo_ref, hl_ref, cl_ref, xbuf, a_s, b_s, hs_s, hc, *, tc, d):
    t = pl.program_id(1)

    @pl.when(t == 0)
    def _():
        xbuf[0:8, :] = jnp.zeros((8, d), f32)
        xbuf[8 - (CONV_W - 1):8, :] = c0_ref[0]
        hc[...] = jnp.broadcast_to(h0_ref[0], (8, d))

    x = x_ref[0]
    mod = mod_ref[0]
    shift, scale, gate = mod[0:1], mod[1:2], mod[2:3]
    m = x * (1.0 + scale) + shift
    u = _dot(m.astype(bf16), win_ref[...])
    xb = u[:, :d]
    zg = u[:, d:]
    xbuf[8:8 + tc, :] = xb
    base = 8 - (CONV_W - 1)
    xc = cb_ref[...] + xbuf[base:base + tc, :] * cw_ref[0:1, :]
    for k in range(1, CONV_W):
        xc = xc + xbuf[base + k:base + k + tc, :] * cw_ref[k:k + 1, :]
    tail = xbuf[8 + tc - (CONV_W - 1):8 + tc, :]
    xbuf[base:8, :] = tail
    cl_ref[0] = tail

    a, b = _rglru_gates(xc, wg_ref, br_ref[...], bi_ref[...], lam_ref[...])

    a = a.reshape(tc // 8, 8, d)
    b = b.reshape(tc // 8, 8, d)
    rowi = lax.broadcasted_iota(i32, (1, 8, 1), 1)
    for s in (1, 2, 4):
        ok = rowi >= s
        a_sh = pltpu.roll(a, s, 1)
        b_sh = pltpu.roll(b, s, 1)
        b = jnp.where(ok, a * b_sh + b, b)
        a = jnp.where(ok, a * a_sh, a)
    a_s[...] = a.reshape(tc, d)
    b_s[...] = b.reshape(tc, d)

    def tile_step(j, hprev):
        r0 = pl.multiple_of(j * 8, 8)
        ht = a_s[pl.ds(r0, 8), :] * hprev + b_s[pl.ds(r0, 8), :]
        hs_s[pl.ds(r0, 8), :] = ht
        return jnp.broadcast_to(ht[7:8, :], (8, d))

    hlast = lax.fori_loop(0, tc // 8, tile_step, hc[...])
    hc[...] = hlast
    hl_ref[0] = hlast[0:1, :]

    y = _dot((hs_s[...] * _silu(zg)).astype(bf16), wout_ref[...])
    xo_ref[0] = _layer_norm(DN_ALPHA * x + (1.0 + gate) * y, lng_ref[...], lnb_ref[...])


def _rglru_prompt(x, mod, h0, c0, w_in, conv_w, conv_b, w_gate, b_r, b_i, lam, w_out, ln_g, ln_b, tc):
    bsz, t, d = x.shape
    tc = min(tc, t)
    const2 = lambda b, i: (0, 0)
    const3 = lambda b, i: (0, 0, 0)
    perb = lambda b, i: (b, 0, 0)
    kern = functools.partial(_rglru_prompt_kernel, tc=tc, d=d)
    return pl.pallas_call(
        kern,
        grid=(bsz, t // tc),
        in_specs=[
            pl.BlockSpec((1, tc, d), lambda b, i: (b, i, 0)),
            pl.BlockSpec((1, 3, d), perb),
            pl.BlockSpec((1, 1, d), perb),
            pl.BlockSpec((1, CONV_W - 1, d), perb),
            pl.BlockSpec(w_in.shape, const2),
            pl.BlockSpec(conv_w.shape, const2),
            pl.BlockSpec((1, d), const2),
            pl.BlockSpec(w_gate.shape, const3),
            pl.BlockSpec((1, d), const2),
            pl.BlockSpec((1, d), const2),
            pl.BlockSpec((1, d), const2),
            pl.BlockSpec(w_out.shape, const2),
            pl.BlockSpec((1, d), const2),
            pl.BlockSpec((1, d), const2),
        ],
        out_specs=[
            pl.BlockSpec((1, tc, d), lambda b, i: (b, i, 0)),
            pl.BlockSpec((1, 1, d), perb),
            pl.BlockSpec((1, CONV_W - 1, d), perb),
        ],
        out_shape=[
            jax.ShapeDtypeStruct((bsz, t, d), f32),
            jax.ShapeDtypeStruct((bsz, 1, d), f32),
            jax.ShapeDtypeStruct((bsz, CONV_W - 1, d), f32),
        ],
        scratch_shapes=[
            pltpu.VMEM((tc + 8, d), f32),
            pltpu.VMEM((tc, d), f32),
            pltpu.VMEM((tc, d), f32),
            pltpu.VMEM((tc, d), f32),
            pltpu.VMEM((8, d), f32),
        ],
        compiler_params=_params(("arbitrary", "arbitrary")),
        name="rglru_prompt",
    )(x, mod, h0, c0, w_in, conv_w, conv_b, w_gate, b_r, b_i, lam, w_out, ln_g, ln_b)


def _rglru_sample_kernel(x_ref, mod_ref, h0_ref, c0_ref, win_ref, cw_ref, cb_ref, wg_ref, br_ref,
                         bi_ref, lam_ref, wout_ref, lng_ref, lnb_ref,
                         xo_ref, hl_ref, cl_ref, *, s_len, bsz, d):
    x = x_ref[...]
    mod = mod_ref[...]
    shift, scale, gate = mod[0:1], mod[1:2], mod[2:3]
    m = x * (1.0 + scale) + shift
    u = _dot(m.reshape(s_len * bsz, d).astype(bf16), win_ref[...])
    xb = u[:, :d].reshape(s_len, bsz, d)
    zg = u[:, d:]
    xp = jnp.concatenate([c0_ref[...], xb], axis=0)
    cw = cw_ref[...]
    xc = cb_ref[...] + xp[0:s_len] * cw[0:1]
    for k in range(1, CONV_W):
        xc = xc + xp[k:k + s_len] * cw[k:k + 1]
    cl_ref[...] = xp[s_len:s_len + CONV_W - 1]
    a, b = _rglru_gates(xc.reshape(s_len * bsz, d), wg_ref, br_ref[...], bi_ref[...], lam_ref[...])
    h = h0_ref[...]
    hs = []
    for s in range(s_len):
        h = a[s * bsz:(s + 1) * bsz] * h + b[s * bsz:(s + 1) * bsz]
        hs.append(h)
    hl_ref[...] = h
    hs = jnp.concatenate(hs, axis=0)
    y = _dot((hs * _silu(zg)).astype(bf16), wout_ref[...])
    xo = _layer_norm(DN_ALPHA * x + (1.0 + gate) * y.reshape(s_len, bsz, d), lng_ref[...], lnb_ref[...])
    xo_ref[...] = xo


def _rglru_sample(x_tm, mod_tm, h0, c0_tm, w_in, conv_w, conv_b, w_gate, b_r, b_i, lam, w_out, ln_g, ln_b):
    s_len, bsz, d = x_tm.shape
    kern = functools.partial(_rglru_sample_kernel, s_len=s_len, bsz=bsz, d=d)
    return pl.pallas_call(
        kern,
        out_shape=[
            jax.ShapeDtypeStruct((s_len, bsz, d), f32),
            jax.ShapeDtypeStruct((bsz, d), f32),
            jax.ShapeDtypeStruct((CONV_W - 1, bsz, d), f32),
        ],
        compiler_params=pltpu.CompilerParams(vmem_limit_bytes=VMEM_LIMIT_BYTES),
        name="rglru_sample",
    )(x_tm, mod_tm, h0, c0_tm, w_in, conv_w, conv_b, w_gate, b_r, b_i, lam, w_out, ln_g, ln_b)


def _proj_kernel(x_ref, mod_ref, wkv_ref, wq_ref, wz_ref, wg_ref, bg_ref,
                 cmp_ref, sel_ref, win_ref, selb_ref, winb_ref, q_ref, z_ref, gl_ref, *, nb, tr, d):
    x = x_ref[...]
    mod = mod_ref[...]
    m = x * (1.0 + mod[:, 1:2, :]) + mod[:, 0:1, :]
    xb = x.reshape(nb * tr, d).astype(bf16)
    mb = m.reshape(nb * tr, d).astype(bf16)
    kv = _dot(xb, wkv_ref[...])
    w = kv.shape[1] // 3
    sel = kv[:, w:2 * w]
    win = kv[:, 2 * w:]
    cmp_ref[...] = kv[:, :w]
    sel_ref[...] = sel
    win_ref[...] = win
    selb_ref[...] = sel.astype(bf16)
    winb_ref[...] = win.astype(bf16)
    q_ref[...] = (_dot(mb, wq_ref[...]) * (HEAD_DIM ** -0.5)).astype(bf16)
    z_ref[...] = _dot(mb, wz_ref[...])
    gl_ref[...] = _dot(mb, wg_ref[...]) + bg_ref[...]


def _proj(x, mod, w_kv, w_q, w_z, w_g, b_g, nb, tr):
    bsz, t, d = x.shape
    kvw = w_kv.shape[1] // 3
    hd = w_q.shape[1]
    gw = w_g.shape[1]
    nt = t // tr
    const2 = lambda b, i: (0, 0)
    blk = lambda width: pl.BlockSpec((nb * tr, width), lambda b, i: (b * nt + i, 0))
    out = lambda width, dt: jax.ShapeDtypeStruct((bsz * t, width), dt)
    kern = functools.partial(_proj_kernel, nb=nb, tr=tr, d=d)
    return pl.pallas_call(
        kern,
        grid=(bsz // nb, nt),
        in_specs=[
            pl.BlockSpec((nb, tr, d), lambda b, i: (b, i, 0)),
            pl.BlockSpec((nb, 3, d), lambda b, i: (b, 0, 0)),
            pl.BlockSpec(w_kv.shape, const2),
            pl.BlockSpec(w_q.shape, const2),
            pl.BlockSpec(w_z.shape, const2),
            pl.BlockSpec(w_g.shape, const2),
            pl.BlockSpec((1, gw), const2),
        ],
        out_specs=[blk(kvw), blk(kvw), blk(kvw), blk(kvw), blk(kvw), blk(hd), blk(hd), blk(gw)],
        out_shape=[out(kvw, f32), out(kvw, f32), out(kvw, f32), out(kvw, bf16), out(kvw, bf16),
                   out(hd, bf16), out(hd, f32), out(gw, f32)],
        compiler_params=_params(("arbitrary", "arbitrary")),
        name="nsa_proj",
    )(x, mod, w_kv, w_q, w_z, w_g, b_g)


def _proj_prompt_kernel(x_ref, mod_ref, wkvt_ref, wq_ref, wz_ref, wg_ref, bg_ref, e_ref, prow_ref, ones_ref,
                        cmpt_ref, selt_ref, wint_ref, cmppg_ref, kaug_ref, vaug_ref, wintt_ref, q_ref, z_ref,
                        gl_ref, *, tr):
    x = x_ref[0]
    mod = mod_ref[0]
    m = x * (1.0 + mod[1:2]) + mod[0:1]
    mb = m.astype(bf16)
    kvt = _dot_nt(wkvt_ref[...], x.astype(bf16))
    kvw = kvt.shape[0] // 3
    cmpt = kvt[:kvw]
    selt = kvt[kvw:2 * kvw]
    wint = kvt[2 * kvw:]
    cmpt_ref[0] = cmpt
    selt_ref[0] = selt
    wint_ref[0] = wint
    selb = selt.astype(bf16)
    n_c = e_ref.shape[1]
    for g in range(N_KV):
        kaug_ref[0, g, 0, 0:n_c] = e_ref[0]
        kaug_ref[0, g, 0, n_c:n_c + HEAD_DIM] = selb[g * GRP:g * GRP + HEAD_DIM]
        kaug_ref[0, g, 0, n_c + HEAD_DIM:] = prow_ref[...]
        vaug_ref[0, g, 0, 0:HEAD_DIM] = selb[g * GRP + HEAD_DIM:(g + 1) * GRP]
        vaug_ref[0, g, 0, HEAD_DIM:] = ones_ref[...]
    winb = wint.astype(bf16)
    for k in range(tr // PAGE_SIZE):
        cmppg_ref[k] = cmpt[:, k * PAGE_SIZE:(k + 1) * PAGE_SIZE]
        wintt_ref[0, k] = winb[:, k * PAGE_SIZE:(k + 1) * PAGE_SIZE]
    q_ref[...] = (_dot(mb, wq_ref[...]) * (HEAD_DIM ** -0.5)).astype(bf16)
    z_ref[...] = _dot(mb, wz_ref[...])
    gl_ref[...] = _dot(mb, wg_ref[...]) + bg_ref[...]


def _proj_prompt(x, mod, w_kvt, w_q, w_z, w_g, b_g, e, prow, ones, tr):
    bsz, t, d = x.shape
    kvw = w_kvt.shape[0] // 3
    hd = w_q.shape[1]
    gw = w_g.shape[1]
    nt = t // tr
    ppt = tr // PAGE_SIZE
    ka_rows = e.shape[1] + HEAD_DIM + prow.shape[0]
    va_rows = HEAD_DIM + ones.shape[0]
    const2 = lambda b, i: (0, 0)
    rows = lambda width: pl.BlockSpec((tr, width), lambda b, i: (b * nt + i, 0))
    tmin = pl.BlockSpec((1, kvw, tr), lambda b, i: (b, 0, i))
    kern = functools.partial(_proj_prompt_kernel, tr=tr)
    return pl.pallas_call(
        kern,
        grid=(bsz, nt),
        in_specs=[
            pl.BlockSpec((1, tr, d), lambda b, i: (b, i, 0)),
            pl.BlockSpec((1, 3, d), lambda b, i: (b, 0, 0)),
            pl.BlockSpec(w_kvt.shape, const2),
            pl.BlockSpec(w_q.shape, const2),
            pl.BlockSpec(w_z.shape, const2),
            pl.BlockSpec(w_g.shape, const2),
            pl.BlockSpec((1, gw), const2),
            pl.BlockSpec((1,) + e.shape[1:], lambda b, i: (i, 0, 0)),
            pl.BlockSpec(prow.shape, const2),
            pl.BlockSpec(ones.shape, const2),
        ],
        out_specs=[
            tmin, tmin, tmin,
            pl.BlockSpec((ppt, kvw, PAGE_SIZE), lambda b, i: (b * nt + i, 0, 0)),
            pl.BlockSpec((1, N_KV, 1, ka_rows, tr), lambda b, i: (b, 0, i, 0, 0)),
            pl.BlockSpec((1, N_KV, 1, va_rows, tr), lambda b, i: (b, 0, i, 0, 0)),
            pl.BlockSpec((1, ppt, kvw, PAGE_SIZE), lambda b, i: (b, i, 0, 0)),
            rows(hd), rows(hd), rows(gw),
        ],
        out_shape=[
            jax.ShapeDtypeStruct((bsz, kvw, t), f32),
            jax.ShapeDtypeStruct((bsz, kvw, t), f32),
            jax.ShapeDtypeStruct((bsz, kvw, t), f32),
            jax.ShapeDtypeStruct((bsz * t // PAGE_SIZE, kvw, PAGE_SIZE), f32),
            jax.ShapeDtypeStruct((bsz, N_KV, nt, ka_rows, tr), bf16),
            jax.ShapeDtypeStruct((bsz, N_KV, nt, va_rows, tr), bf16),
            jax.ShapeDtypeStruct((bsz, t // PAGE_SIZE, kvw, PAGE_SIZE), bf16),
            jax.ShapeDtypeStruct((bsz * t, hd), bf16),
            jax.ShapeDtypeStruct((bsz * t, hd), f32),
            jax.ShapeDtypeStruct((bsz * t, gw), f32),
        ],
        compiler_params=_params(("arbitrary", "arbitrary")),
        name="nsa_proj_prompt",
    )(x, mod, w_kvt, w_q, w_z, w_g, b_g, e, prow, ones)


def _compress_kernel(x_ref, pet_ref, w1_ref, b1_ref, w2_ref, b2_ref, kc_ref, vc_ref, *, npg):
    x_rows = x_ref.reshape(npg * PAGE_PITCH, x_ref.shape[2])

    def dim_rows(c, dd):
        return jnp.concatenate(
            [(x_rows[pl.ds((g * 2 + c) * HEAD_DIM + dd, npg, stride=PAGE_PITCH), :]
              + pet_ref[c, dd:dd + 1, :]).astype(bf16) for g in range(N_KV)], axis=0)

    outs = []
    for c in range(2):
        acc = None
        for dp in range(HEAD_DIM // 2):
            lhs = jnp.concatenate([dim_rows(c, 2 * dp), dim_rows(c, 2 * dp + 1)], axis=1)
            part = _dot(lhs, w1_ref[c, dp])
            acc = part if acc is None else acc + part
        hid = _silu(acc + b1_ref[c])
        outs.append(_dot(hid.astype(bf16), w2_ref[c]) + b2_ref[c])
    for g in range(N_KV):
        kc_ref[g] = outs[0][g * npg:(g + 1) * npg, :].astype(kc_ref.dtype)
        vc_ref[g] = outs[1][g * npg:(g + 1) * npg, :].astype(vc_ref.dtype)


def _compress(x2d, pet, w1, b1, w2, b2, npg, out_dtype):
    rows, width = x2d.shape
    n_pages = rows // (N_KV * GRP)
    npg = min(npg, n_pages)
    while n_pages % npg:
        npg -= 8
    ow = (PAGE_SIZE // CMP_BLOCK) * HEAD_DIM
    const3 = lambda i: (0, 0, 0)
    kern = functools.partial(_compress_kernel, npg=npg)
    out = pl.BlockSpec((N_KV, npg, ow), lambda i: (0, i, 0))
    return pl.pallas_call(
        kern,
        grid=(n_pages // npg,),
        in_specs=[
            pl.BlockSpec((npg, PAGE_PITCH, width), lambda i: (i, 0, 0)),
            pl.BlockSpec(pet.shape, const3),
            pl.BlockSpec(w1.shape, lambda i: (0, 0, 0, 0), pipeline_mode=pl.Buffered(1)),
            pl.BlockSpec(b1.shape, const3),
            pl.BlockSpec(w2.shape, const3),
            pl.BlockSpec(b2.shape, const3),
        ],
        out_specs=[out, out],
        out_shape=[jax.ShapeDtypeStruct((N_KV, n_pages, ow), out_dtype)] * 2,
        compiler_params=_params(("arbitrary",)),
        name="nsa_compress",
    )(x2d.reshape(n_pages, N_KV * GRP, width), pet, w1, b1, w2, b2)


def _block_ids(n_pages, lanes, axis):
    shape = (1, lanes) if axis == 1 else (lanes, 1)
    r = lax.broadcasted_iota(i32, shape, axis)
    bpp = PAGE_SIZE // CMP_BLOCK
    perm = jnp.where(r < n_pages, bpp * r, bpp * (r - n_pages) + 1)
    return jnp.where(r < bpp * n_pages, perm, r)


def _unpage(x):
    return jnp.concatenate([x[:, :HEAD_DIM], x[:, HEAD_DIM:]], axis=0)


def _cmp_branch(q4, kc, vc, slope, t_col, nq, jrow):
    s = _dot_nt(q4, kc)
    c_end = (jrow + 1) * CMP_BLOCK - 1
    dist = t_col.astype(f32) - c_end.astype(f32)
    s = s - slope * dist
    mask = c_end <= t_col
    s = jnp.where(mask, s, NEG)
    e = jnp.exp(s - jnp.max(s, axis=-1, keepdims=True))
    p = e / jnp.sum(e, axis=-1, keepdims=True)
    p = jnp.where(mask, p, 0.0)
    o = _dot(p.astype(bf16), vc)
    imp = p[0:nq]
    for h in range(1, HPG):
        imp = imp + p[h * nq:(h + 1) * nq]
    return o, imp


def _select_blocks(imp, tq_col, n_sb, jr):
    nq, lanes = imp.shape
    cb = tq_col // CMP_BLOCK
    forced = (jr == 0) | (jr == cb) | (jr == cb - 1)
    causal = jr <= cb
    score = jnp.where(forced, REMOVED, jnp.where(causal, imp, -1.0))
    score = jnp.where(jr < n_sb, score, REMOVED)
    jf = jr.astype(f32)
    sel = jnp.where(forced & (jr < n_sb), 1.0, 0.0) + jnp.zeros((nq, lanes), f32)
    for _ in range(max(min(N_SEL, n_sb) - 3, 0)):
        mx = jnp.max(score, axis=-1, keepdims=True)
        idx = jnp.min(jnp.where(score == mx, jf, 1e9), axis=-1, keepdims=True)
        hit = jf == idx
        sel = jnp.where(hit, 1.0, sel)
        score = jnp.where(hit, REMOVED, score)
    return sel


def _flash_update(carry, q4, kt, vt, slope, valid, pos_rel):
    m_i, l_i, acc = carry
    s = _dot(q4, kt) + slope * pos_rel
    s = jnp.where(valid, s, NEG)
    m_new = jnp.maximum(m_i, jnp.max(s, axis=-1, keepdims=True))
    alpha = jnp.exp(m_i - m_new)
    p = jnp.exp(s - m_new)
    l_new = alpha * l_i + jnp.sum(p, axis=-1, keepdims=True)
    acc = alpha * acc + _dot_nt(p.astype(bf16), vt)
    return m_new, l_new, acc


def _block_mask(selb, jcol, pos, reps):
    onehot = (jcol == pos // CMP_BLOCK).astype(bf16)
    msk = _dot(selb, onehot)
    return jnp.concatenate([msk] * reps, axis=0) > 0.5


def _gate_mix(gl, o_cmp, o_sel, o_win, nq):
    gs = _sigmoid(gl)
    outs = []
    for h in range(HPG):
        sl = slice(h * nq, (h + 1) * nq)
        outs.append(gs[:, 3 * h:3 * h + 1] * o_cmp[sl] + gs[:, 3 * h + 1:3 * h + 2] * o_sel[sl]
                    + gs[:, 3 * h + 2:3 * h + 3] * o_win[sl])
    return jnp.concatenate(outs, axis=1)


def _init_carry(rows):
    return (jnp.full((rows, 1), NEG, f32), jnp.zeros((rows, 1), f32), jnp.zeros((rows, HEAD_DIM), f32))


def _select_blocks_t(impt, tq_row, n_sb, jcol, fillers=()):
    n_c, nq = impt.shape
    cb = tq_row // CMP_BLOCK
    forced = (jcol == 0) | (jcol == cb) | (jcol == cb - 1)
    causal = jcol <= cb
    score = jnp.where(forced, REMOVED, jnp.where(causal, impt, -1.0))
    score = jnp.where(jcol < n_sb, score, REMOVED)
    jf = jcol.astype(f32)
    sel = jnp.where(forced & (jcol < n_sb), 1.0, 0.0) + jnp.zeros((n_c, nq), f32)
    rounds = max(min(N_SEL, n_sb) - 3, 0)
    fillers = list(fillers)
    every = max(rounds // max(len(fillers), 1), 1)
    for it in range(rounds):
        mx = jnp.max(score, axis=0, keepdims=True)
        idx = jnp.min(jnp.where(score == mx, jf, 1e9), axis=0, keepdims=True)
        hit = jf == idx
        sel = jnp.where(hit, 1.0, sel)
        score = jnp.where(hit, REMOVED, score)
        if fillers and it % every == 0:
            fillers.pop(0)()
    for f in fillers:
        f()
    return sel


def _nsa_prompt_kernel(q_ref, gl_ref, kc_ref, vc_ref, kaug_ref, vaug_ref, kwin_ref, scol_ref, srow_ref, coef_ref,
                       wbias_ref, cbias_ref, eye_ref, tilemap_ref, o_ref,
                       s0_s, s1_s, p0_s, p1_s, a0_s, a1_s, m_s, acc_s, tiles_s, *, t_len, tq, tk, tkw):
    qi = pl.program_id(2)
    t0 = qi * tq
    rows = HPG * tq
    q = q_ref[0]
    q4 = jnp.concatenate([q[:, h * HEAD_DIM:(h + 1) * HEAD_DIM] for h in range(HPG)], axis=0)
    slope = scol_ref[0]
    t_col = t0 + lax.broadcasted_iota(i32, (rows, 1), 0) % tq

    n_c = t_len // CMP_BLOCK
    n_pages = t_len // PAGE_SIZE
    jrow = _block_ids(n_pages, n_c, 1)
    jcol = _block_ids(n_pages, n_c, 0)
    kc = _unpage(kc_ref[0]).astype(bf16)
    vc = _unpage(vc_ref[0]).astype(bf16)
    o_cmp_box = []

    def cmp_output():
        o_cmp_box.append(_cmp_branch(q4, kc, vc, slope, t_col, tq, jrow)[0])

    def init_state():
        m_s[...] = jnp.full((rows, 1), NEG, f32)
        acc_s[...] = jnp.zeros(acc_s.shape, f32)
        p1_s[...] = jnp.zeros(p1_s.shape, bf16)
        a1_s[...] = jnp.ones((rows, 1), f32)

    nslots = (WINDOW + tq) // tkw
    base = (t0 - WINDOW) // tkw
    kts, vts, pens = [], [], []
    for m in range(nslots):
        kw = base + m
        kvw = kwin_ref[0, jnp.maximum(kw, 0)]
        kts.append(kvw[:HEAD_DIM])
        vts.append(kvw[HEAD_DIM:])
        pens.append(jnp.broadcast_to(jnp.where(kw >= 0, 0.0, NEG).astype(f32), (1, tkw)))
    sw = _dot(q4, jnp.concatenate(kts, axis=1))
    pen = jnp.concatenate(pens, axis=1)
    pw_h, lw_h = [], []

    def window_softmax(h):
        def run():
            s_h = sw[h * tq:(h + 1) * tq] + wbias_ref[0, h * tq:(h + 1) * tq] + pen
            p_h = jnp.exp(s_h - jnp.max(s_h, axis=-1, keepdims=True))
            pw_h.append(p_h.astype(bf16))
            lw_h.append(jnp.sum(p_h, axis=-1, keepdims=True))
        return run

    t_row = t0 + lax.broadcasted_iota(i32, (1, rows), 1) % tq
    c_end = (jcol + 1) * CMP_BLOCK - 1
    st = _dot_nt(kc, q4) - srow_ref[0] * (t_row.astype(f32) - c_end.astype(f32))
    maskt = c_end <= t_row
    st = jnp.where(maskt, st, NEG)
    et = jnp.exp(st - jnp.max(st, axis=0, keepdims=True))
    pt = et / jnp.sum(et, axis=0, keepdims=True)
    pt = jnp.where(maskt, pt, 0.0)
    impt = pt[:, 0:tq]
    for h in range(1, HPG):
        impt = impt + pt[:, h * tq:(h + 1) * tq]
    selt = _select_blocks_t(impt, t0 + lax.broadcasted_iota(i32, (1, tq), 1), n_c, jcol,
                            fillers=[window_softmax(h) for h in range(HPG)] + [cmp_output, init_state])
    o_cmp = o_cmp_box[0]
    sel = _dot_nt(eye_ref[...], selt.astype(bf16))
    selbias = ((sel - 1.0) * -NEG).astype(bf16)
    q_aug = jnp.concatenate([jnp.concatenate([selbias] * HPG, axis=0), q4, coef_ref[0]], axis=1)


    def scores_into(dst, kt):
        dst[...] = _dot(q_aug, kaug_ref[0, 0, kt])

    def softmax_into(s, kt, p_dst, a_dst):
        c = slope * (kt * tk - t0).astype(f32)
        m_i = m_s[...]
        m_new = jnp.maximum(m_i, jnp.max(s, axis=-1, keepdims=True) + c)
        p_dst[...] = jnp.exp(s - (m_new - c)).astype(bf16)
        a_dst[...] = jnp.exp(m_i - m_new)
        m_s[...] = m_new

    def values_from(p_src, a_src, kt):
        acc_s[...] = a_src[...] * acc_s[...] + _dot_nt(p_src[...], vaug_ref[0, 0, kt])

    kd = t0 // tk
    tile_hits = jnp.max(_dot(tilemap_ref[...], selt.astype(bf16)), axis=1, keepdims=True)
    n_act = jnp.int32(0)
    tiles_s[0] = 0
    for kt in range(t_len // tk):
        tiles_s[n_act] = kt
        hit = (jnp.max(tile_hits[kt:kt + 1, :]) > 0.5) & (kt < kd)
        n_act = n_act + hit.astype(i32)
    npairs = n_act // 2
    last_piped = jnp.maximum(2 * npairs - 1, 0)
    scores_into(s0_s, tiles_s[0])

    def pair(j, carry):
        ia = 2 * j
        ka = tiles_s[ia]
        kb = tiles_s[ia + 1]
        values_from(p1_s, a1_s, tiles_s[jnp.maximum(ia - 1, 0)])
        scores_into(s1_s, kb)
        softmax_into(s0_s[...], ka, p0_s, a0_s)
        values_from(p0_s, a0_s, ka)
        scores_into(s0_s, tiles_s[jnp.minimum(ia + 2, last_piped)])
        softmax_into(s1_s[...], kb, p1_s, a1_s)
        return carry

    lax.fori_loop(0, npairs, pair, 0)
    values_from(p1_s, a1_s, tiles_s[last_piped])

    def tile_plain(kt, bias):
        s = _dot(q_aug, kaug_ref[0, 0, kt])
        if bias is not None:
            s = s + bias
        softmax_into(s, kt, p0_s, a0_s)
        values_from(p0_s, a0_s, kt)

    @pl.when(n_act % 2 == 1)
    def _():
        tile_plain(tiles_s[jnp.maximum(n_act - 1, 0)], None)

    s_d = _dot(q_aug, kaug_ref[0, 0, kd]) + jnp.concatenate([cbias_ref[(t0 % tk) // tq]] * HPG, axis=0)
    o_win = (_dot_nt(jnp.concatenate(pw_h, axis=0), jnp.concatenate(vts, axis=1))
             / jnp.concatenate(lw_h, axis=0))
    softmax_into(s_d, kd, p0_s, a0_s)
    values_from(p0_s, a0_s, kd)
    acc = acc_s[...]
    o_sel = acc[:, :HEAD_DIM] / acc[:, HEAD_DIM:HEAD_DIM + 1]

    o_ref[0] = _gate_mix(gl_ref[0], o_cmp, o_sel, o_win, tq)


def _prompt_consts(t_len, tq, tk, tkw):
    n_c = t_len // CMP_BLOCK
    n_pages = t_len // PAGE_SIZE
    rows = HPG * tq
    r = np.arange(n_c)
    bpp = PAGE_SIZE // CMP_BLOCK
    ids = np.where(r < n_pages, bpp * r, bpp * (r - n_pages) + 1)
    onehot = ids[:, None] == (np.arange(t_len) // CMP_BLOCK)[None, :]
    e = onehot.reshape(n_c, t_len // tk, tk).transpose(1, 0, 2)
    lane = np.arange(tk)
    prow = np.zeros((HEAD_DIM, tk), np.float32)
    prow[0:3] = lane // 256
    prow[3:6] = lane % 256
    slopes = (2.0 ** (-8.0 * np.arange(1, N_HEADS + 1) / N_HEADS)).astype(np.float32)
    srow = np.repeat(slopes.reshape(N_KV, HPG), tq, axis=1)
    s1 = srow.astype(jnp.bfloat16).astype(np.float32)
    s2 = (srow - s1).astype(jnp.bfloat16).astype(np.float32)
    s3 = (srow - s1 - s2).astype(jnp.bfloat16).astype(np.float32)
    coef = np.zeros((N_KV, rows, HEAD_DIM), np.float32)
    for k, piece in enumerate((s1, s2, s3)):
        coef[:, :, k] = 256.0 * piece
        coef[:, :, 3 + k] = piece
    a = np.tile(np.arange(tq), HPG)[:, None]
    nslots = (WINDOW + tq) // tkw
    dpos = a + WINDOW - np.arange(nslots * tkw)[None, :]
    band = np.where((dpos >= 0) & (dpos <= WINDOW), 0.0, NEG)
    wbias = -srow[:, :, None] * dpos[None] + band[None]
    off = np.arange(tk // tq)[:, None, None] * tq
    cbias = np.where(lane[None, None, :] <= off + np.arange(tq)[None, :, None], 0.0, NEG)
    n_tiles = t_len // tk
    tilemap = np.arange(-(-n_tiles // 16) * 16)[:, None] == (ids // (tk // CMP_BLOCK))[None, :]
    return dict(
        tilemap=jnp.asarray(tilemap, bf16),
        scol=jnp.asarray(srow.reshape(N_KV, rows, 1), f32), srow=jnp.asarray(srow.reshape(N_KV, 1, rows), f32),
        coef=jnp.asarray(coef, bf16), e=jnp.asarray(e, bf16), prow=jnp.asarray(prow, bf16),
        ones=jnp.ones((16, tk), bf16), wbias=jnp.asarray(wbias, f32), cbias=jnp.asarray(cbias, f32),
        eye=jnp.eye(tq, dtype=bf16))


def _nsa_prompt(q, gl, kc, vc, kaug, vaug, kwin, c, tq):
    bsz, t_len, _ = q.shape
    tk = kaug.shape[-1]
    tkw = kwin.shape[-1]
    npb = t_len // PAGE_SIZE
    rows = HPG * tq
    kern = functools.partial(_nsa_prompt_kernel, t_len=t_len, tq=tq, tk=tk, tkw=tkw)
    augspec = lambda arr: pl.BlockSpec((1, 1) + arr.shape[2:], lambda b, g, i: (b, g, 0, 0, 0))
    cspec = pl.BlockSpec((1, npb, kc.shape[-1]), lambda b, g, i: (g, b, 0))
    perg = lambda arr: pl.BlockSpec((1,) + arr.shape[1:], lambda b, g, i: (g,) + (0,) * (arr.ndim - 1))
    whole = lambda arr: pl.BlockSpec(arr.shape, lambda b, g, i: (0,) * arr.ndim)
    return pl.pallas_call(
        kern,
        grid=(bsz, N_KV, t_len // tq),
        in_specs=[
            pl.BlockSpec((1, tq, QGRP), lambda b, g, i: (b, i, g)),
            pl.BlockSpec((1, tq, 128), lambda b, g, i: (b, i, g)),
            cspec, cspec,
            augspec(kaug), augspec(vaug),
            pl.BlockSpec((1, t_len // tkw, GRP, tkw), lambda b, g, i: (b, 0, g, 0)),
            perg(c["scol"]), perg(c["srow"]), perg(c["coef"]),
            perg(c["wbias"]), whole(c["cbias"]), whole(c["eye"]), whole(c["tilemap"]),
        ],
        out_specs=pl.BlockSpec((1, tq, QGRP), lambda b, g, i: (b, i, g)),
        out_shape=jax.ShapeDtypeStruct((bsz, t_len, N_KV * QGRP), f32),
        scratch_shapes=[
            pltpu.VMEM((rows, tk), f32), pltpu.VMEM((rows, tk), f32),
            pltpu.VMEM((rows, tk), bf16), pltpu.VMEM((rows, tk), bf16),
            pltpu.VMEM((rows, 1), f32), pltpu.VMEM((rows, 1), f32),
            pltpu.VMEM((rows, 1), f32), pltpu.VMEM((rows, vaug.shape[3]), f32),
            pltpu.SMEM((t_len // tk + 8,), i32),
        ],
        compiler_params=_params(("arbitrary", "arbitrary", "arbitrary")),
        name="nsa_prompt_attn",
    )(q, gl, kc, vc, kaug, vaug, kwin, c["scol"], c["srow"], c["coef"], c["wbias"], c["cbias"], c["eye"],
      c["tilemap"])


def _nsa_sample_kernel(pt_ref, q_ref, gl_ref, kcall_ref, vcall_ref, pid_ref, newsel_ref, swin_ref, nwin_ref,
                       slope_ref, *rest, s_len, past_len, pp, n_groups, lanes):
    page_refs = rest[:pp]
    o_ref = rest[pp]
    sel_s, ocmp_s, qbd_s, m_s, l_s, acc_s = rest[pp + 1:]
    j = pl.program_id(1)
    rows = HPG * s_len
    n_c = past_len // CMP_BLOCK
    npb = past_len // PAGE_SIZE
    jrow = _block_ids(npb, lanes, 1)
    jcol = _block_ids(npb, lanes, 0)
    q = q_ref[0].astype(f32)
    t_col = past_len + lax.broadcasted_iota(i32, (rows, 1), 0) % s_len
    tq_col = past_len + lax.broadcasted_iota(i32, (s_len, 1), 0)

    def q_of(g):
        return jnp.concatenate(
            [q[:, (g * HPG + h) * HEAD_DIM:(g * HPG + h + 1) * HEAD_DIM] for h in range(HPG)],
            axis=0).astype(bf16)

    groups = range(N_KV)
    all_rows = N_KV * rows
    slope_all = jnp.concatenate([slope_ref[g] for g in groups], axis=0)
    t_all = past_len + lax.broadcasted_iota(i32, (all_rows, 1), 0) % s_len

    @pl.when(j == 0)
    def _():
        n_all = kcall_ref.shape[1]
        gather = (lax.broadcasted_iota(i32, (npb, n_all), 1) == pid_ref[0]).astype(bf16)
        zeros = jnp.zeros((rows, HEAD_DIM), bf16)
        imps = []
        for g in groups:
            kc = _unpage(_dot(gather, kcall_ref[g])).astype(bf16)
            vc = _unpage(_dot(gather, vcall_ref[g])).astype(bf16)
            q4 = q_of(g)
            o_cmp, imp = _cmp_branch(q4, kc, vc, slope_ref[g], t_col, s_len, jrow[:, :n_c])
            imps.append(jnp.concatenate([imp, jnp.zeros((s_len, lanes - n_c), f32)], axis=1))
            ocmp_s[g] = o_cmp
            qbd_s[g * rows:(g + 1) * rows] = jnp.concatenate([q4 if gg == g else zeros for gg in groups], axis=1)
        sel_s[...] = _select_blocks(jnp.concatenate(imps, axis=0), jnp.concatenate([tq_col] * N_KV, axis=0),
                                    n_c + 1, jrow)
        m_s[...] = jnp.full((all_rows, 1), NEG, f32)
        l_s[...] = jnp.zeros((all_rows, 1), f32)
        acc_s[...] = jnp.zeros(acc_s.shape, f32)

    def stack(keys):
        kt = jnp.concatenate([keys[g * GRP:g * GRP + HEAD_DIM] for g in groups], axis=0).astype(bf16)
        vt = jnp.concatenate([keys[g * GRP + HEAD_DIM:(g + 1) * GRP] for g in groups], axis=0).astype(bf16)
        return kt, vt

    def attend(keys, pos):
        onehot = (jcol == pos // CMP_BLOCK).astype(bf16)
        msk = _dot(sel_s[...].astype(bf16), onehot)
        msk = jnp.concatenate([msk[g * s_len:(g + 1) * s_len] for g in groups for _ in range(HPG)], axis=0)
        valid = (msk > 0.5) & (pos <= t_all)
        kt, vt = stack(keys)
        m_new, l_new, acc = _flash_update((m_s[...], l_s[...], acc_s[...]), qbd_s[...], kt, vt, slope_all,
                                          valid, (pos - past_len).astype(f32))
        m_s[...] = m_new
        l_s[...] = l_new
        acc_s[...] = acc

    nk = pp * PAGE_SIZE
    keys = jnp.concatenate([r[0] for r in page_refs], axis=1)
    attend(keys, j * nk + lax.broadcasted_iota(i32, (1, nk), 1))

    @pl.when(j == n_groups - 1)
    def _():
        new_keys = newsel_ref[0]
        attend(new_keys, past_len + lax.broadcasted_iota(i32, (1, new_keys.shape[1]), 1))
        o_sel = acc_s[...] / l_s[...]
        carry = (jnp.full((all_rows, 1), NEG, f32), jnp.zeros((all_rows, 1), f32),
                 jnp.zeros((all_rows, N_KV * HEAD_DIM), f32))
        for wkeys, wbase in ((swin_ref[0], past_len - swin_ref.shape[2]), (nwin_ref[0], past_len)):
            wpos = wbase + lax.broadcasted_iota(i32, (1, wkeys.shape[1]), 1)
            dpos = t_all - wpos
            valid = (dpos >= 0) & (dpos <= WINDOW)
            kt, vt = stack(wkeys)
            carry = _flash_update(carry, qbd_s[...], kt, vt, slope_all, valid, (wpos - past_len).astype(f32))
        o_win = carry[2] / carry[1]
        own = lambda o, g: o[g * rows:(g + 1) * rows, g * HEAD_DIM:(g + 1) * HEAD_DIM]
        gl = gl_ref[0]
        o_ref[0] = jnp.concatenate(
            [_gate_mix(gl[:, g * 128:(g + 1) * 128], ocmp_s[g], own(o_sel, g), own(o_win, g), s_len)
             for g in groups], axis=1)


def _nsa_sample(page_table, q, gl, kc_all, vc_all, pids, new_sel, state_win, new_win, slopes, cache_sel, pp):
    bsz, s_len, hd = q.shape
    n_pages = page_table.shape[1]
    past_len = n_pages * PAGE_SIZE
    pp = min(pp, n_pages)
    n_groups = n_pages // pp
    n_c = past_len // CMP_BLOCK
    lanes = -(-(n_c + 2) // 128) * 128
    rows = HPG * s_len
    kvw = cache_sel.shape[1]
    perb = lambda b, j, pt: (b, 0, 0)

    def page_spec(i):
        return pl.BlockSpec((1, kvw, PAGE_SIZE), lambda b, j, pt: (pt[b, j * pp + i], 0, 0))

    kern = functools.partial(_nsa_sample_kernel, s_len=s_len, past_len=past_len, pp=pp,
                             n_groups=n_groups, lanes=lanes)
    grid_spec = pltpu.PrefetchScalarGridSpec(
        num_scalar_prefetch=1,
        grid=(bsz, n_groups),
        in_specs=[
            pl.BlockSpec((1, s_len, hd), perb),
            pl.BlockSpec((1, s_len, gl.shape[-1]), perb),
            pl.BlockSpec(kc_all.shape, lambda b, j, pt: (0, 0, 0)),
            pl.BlockSpec(vc_all.shape, lambda b, j, pt: (0, 0, 0)),
            pl.BlockSpec((1, n_pages, 1), perb),
            pl.BlockSpec((1,) + new_sel.shape[1:], perb),
            pl.BlockSpec((1,) + state_win.shape[1:], perb),
            pl.BlockSpec((1,) + new_win.shape[1:], perb),
            pl.BlockSpec(slopes.shape, lambda b, j, pt: (0, 0, 0)),
        ] + [page_spec(i) for i in range(pp)],
        out_specs=pl.BlockSpec((1, s_len, hd), perb),
        scratch_shapes=[
            pltpu.VMEM((N_KV * s_len, lanes), f32),
            pltpu.VMEM((N_KV, rows, HEAD_DIM), f32),
            pltpu.VMEM((N_KV * rows, N_KV * HEAD_DIM), bf16),
            pltpu.VMEM((N_KV * rows, 1), f32),
            pltpu.VMEM((N_KV * rows, 1), f32),
            pltpu.VMEM((N_KV * rows, N_KV * HEAD_DIM), f32),
        ],
    )
    return pl.pallas_call(
        kern,
        grid_spec=grid_spec,
        out_shape=jax.ShapeDtypeStruct((bsz, s_len, hd), f32),
        compiler_params=_params(("arbitrary", "arbitrary")),
        name="nsa_sample_attn",
    )(page_table, q, gl, kc_all, vc_all, pids, new_sel, state_win, new_win, slopes, *([cache_sel] * pp))


def _out_kernel(o_ref, z_ref, x_ref, mod_ref, w_ref, lng_ref, lnb_ref, xo_ref, *, nb, tr, d):
    y = _dot((o_ref[...] * _silu(z_ref[...])).astype(bf16), w_ref[...]).reshape(nb, tr, d)
    gate = mod_ref[...][:, 2:3, :]
    xo_ref[...] = _layer_norm(DN_ALPHA * x_ref[...] + (1.0 + gate) * y, lng_ref[...], lnb_ref[...])


def _out_proj(o, z, x, mod, w_out, ln_g, ln_b, nb, tr):
    bsz, t, d = x.shape
    hd = o.shape[-1]
    nt = t // tr
    const2 = lambda b, i: (0, 0)
    kern = functools.partial(_out_kernel, nb=nb, tr=tr, d=d)
    return pl.pallas_call(
        kern,
        grid=(bsz // nb, nt),
        in_specs=[
            pl.BlockSpec((nb * tr, hd), lambda b, i: (b * nt + i, 0)),
            pl.BlockSpec((nb * tr, hd), lambda b, i: (b * nt + i, 0)),
            pl.BlockSpec((nb, tr, d), lambda b, i: (b, i, 0)),
            pl.BlockSpec((nb, 3, d), lambda b, i: (b, 0, 0)),
            pl.BlockSpec(w_out.shape, const2),
            pl.BlockSpec((1, d), const2),
            pl.BlockSpec((1, d), const2),
        ],
        out_specs=pl.BlockSpec((nb, tr, d), lambda b, i: (b, i, 0)),
        out_shape=jax.ShapeDtypeStruct((bsz, t, d), f32),
        compiler_params=_params(("arbitrary", "arbitrary")),
        name="nsa_out",
    )(o, z, x, mod, w_out, ln_g, ln_b)


def _head_slopes(nq):
    s = 2.0 ** (-8.0 * np.arange(1, N_HEADS + 1) / N_HEADS)
    s = np.repeat(s.reshape(N_KV, HPG), nq, axis=1)
    return jnp.asarray(s.reshape(N_KV, HPG * nq, 1), f32)


def kernel(x_prompt, x_sample, c_prompt, c_sample, state_h, state_conv, cache_cmp, cache_sel, state_win, page_table, w_ada, b_ada, ln_g, ln_b, w_in_a, conv_w_a, conv_b_a, w_r_a, b_r_a, w_i_a, b_i_a, lam_a, w_out_a, w_kv, phi_pe, w_phi1, b_phi1, w_phi2, b_phi2, w_in_b, b_gate_b, w_out_b):
    bp, t_len, d = x_prompt.shape
    bs, s_len, _ = x_sample.shape
    n_pages = page_table.shape[1]
    past_len = n_pages * PAGE_SIZE
    hd = N_HEADS * HEAD_DIM
    kvw = N_KV * GRP
    assert s_len <= CMP_BLOCK and t_len % 128 == 0 and w_ada.shape[0] == DEPTH == 2

    n_c = bp + bs
    pad = -n_c % 8
    c_all = jnp.concatenate([c_prompt, c_sample, jnp.zeros((pad, d), f32)], axis=0)
    mods = _ada_mod(c_all, w_ada, b_ada).reshape(DEPTH, n_c + pad, 3, d)
    mod_p = mods[:, :bp]
    mod_s = mods[:, bp:n_c]

    rg = d // N_RG_BLOCKS
    w_in0 = w_in_a[0].astype(bf16)
    w_gate = jnp.concatenate([w_r_a[0], w_i_a[0]], axis=-1).astype(bf16)
    w_out0 = w_out_a[0].astype(bf16)
    row = lambda v: v.reshape(1, -1)
    rg_args = (w_in0, conv_w_a[0], row(conv_b_a[0]), w_gate, row(b_r_a[0]), row(b_i_a[0]), row(lam_a[0]),
               w_out0, row(ln_g[0]), row(ln_b[0]))
    xp1, h_p, conv_p = _rglru_prompt(x_prompt, mod_p[0], jnp.zeros((bp, 1, d), f32),
                                     jnp.zeros((bp, CONV_W - 1, d), f32), *rg_args, tc=256)
    xs1_tm, h_s, conv_s_tm = _rglru_sample(
        x_sample.transpose(1, 0, 2), mod_s[0].transpose(1, 0, 2), state_h[0],
        state_conv[0].transpose(1, 0, 2), *rg_args)
    xs1 = xs1_tm.transpose(1, 0, 2)
    new_h_p = h_p.reshape(1, bp, d)
    new_conv_p = conv_p.reshape(1, bp, CONV_W - 1, d)
    new_h_s = h_s.reshape(1, bs, d)
    new_conv_s = conv_s_tm.transpose(1, 0, 2).reshape(1, bs, CONV_W - 1, d)

    w_b = w_in_b[0]
    w_q = w_b[:, :hd].astype(bf16)
    w_z = w_b[:, hd:2 * hd].astype(bf16)
    gpg = HPG * 3
    w_g = jnp.pad(w_b[:, 2 * hd:].reshape(d, N_KV, gpg), ((0, 0), (0, 0), (0, 128 - gpg)))
    w_g = w_g.reshape(d, N_KV * 128).astype(bf16)
    b_g = jnp.pad(b_gate_b[0].reshape(N_KV, gpg), ((0, 0), (0, 128 - gpg))).reshape(1, N_KV * 128)
    tmin = lambda v: v.transpose(0, 2, 3, 4, 1)
    tq = min(128, t_len)
    tk = min(512, t_len)
    consts = _prompt_consts(t_len, tq, tk, PAGE_SIZE)
    cmpt_p, selt_p, wint_p, cmp_pages, kaug_p, vaug_p, wint_tiles, q_p, z_p, gl_p = _proj_prompt(
        xp1, mod_p[1], w_kv.T.astype(bf16), w_q, w_z, w_g, b_g, consts["e"], consts["prow"], consts["ones"],
        tr=tk)
    cmp_s, sel_s, win_s, selb_s, winb_s, q_s, z_s, gl_s = _proj(
        xs1, mod_s[1], w_kv.astype(bf16), w_q, w_z, w_g, b_g, nb=bs, tr=s_len)
    per_p = lambda v: v.reshape(bp, t_len, v.shape[-1])
    per_s = lambda v: v.reshape(bs, s_len, v.shape[-1])

    bpp = PAGE_SIZE // CMP_BLOCK
    dphi = w_phi2.shape[1]
    diag2 = lambda w: jnp.concatenate([jnp.concatenate([w, jnp.zeros_like(w)], axis=-1),
                                       jnp.concatenate([jnp.zeros_like(w), w], axis=-1)], axis=-2)
    pet = jnp.tile(phi_pe.transpose(1, 2, 0), (1, 1, bpp))
    w1 = diag2(w_phi1.transpose(0, 2, 1, 3)).astype(bf16)
    w1 = w1.reshape(2, HEAD_DIM // 2, 2 * PAGE_SIZE, bpp * dphi)
    b1 = jnp.tile(b_phi1, (1, bpp)).reshape(2, 1, bpp * dphi)
    w2 = diag2(w_phi2).astype(bf16)
    b2 = jnp.tile(b_phi2, (1, bpp)).reshape(2, 1, bpp * HEAD_DIM)
    kc_p, vc_p = _compress(cmp_pages.reshape(-1, PAGE_SIZE), pet, w1, b1, w2, b2, npg=64, out_dtype=f32)
    kc_all, vc_all = _compress(tmin(cache_cmp).reshape(-1, PAGE_SIZE), pet, w1, b1, w2, b2,
                               npg=64, out_dtype=bf16)

    o_p = _nsa_prompt(per_p(q_p), per_p(gl_p), kc_p, vc_p, kaug_p, vaug_p, wint_tiles, consts, tq=tq)
    tpad = lambda v: jnp.pad(per_s(v), ((0, 0), (0, PAGE_SIZE - s_len), (0, 0))).transpose(0, 2, 1)
    o_s = _nsa_sample(page_table, per_s(q_s), per_s(gl_s), kc_all, vc_all, page_table.reshape(bs, n_pages, 1),
                      tpad(selb_s), tmin(state_win).reshape(bs, kvw, -1), tpad(winb_s), _head_slopes(s_len),
                      tmin(cache_sel).reshape(-1, kvw, PAGE_SIZE), pp=16)

    w_out1 = w_out_b[0].astype(bf16)
    y_p = _out_proj(o_p.reshape(bp * t_len, hd), z_p, xp1, mod_p[1], w_out1, row(ln_g[1]), row(ln_b[1]),
                    nb=1, tr=min(512, t_len))
    y_s = _out_proj(o_s.reshape(bs * s_len, hd), z_s, xs1, mod_s[1], w_out1, row(ln_g[1]), row(ln_b[1]),
                    nb=bs, tr=s_len)

    kv5 = lambda v: v.reshape(bs, s_len, N_KV, 2, HEAD_DIM)
    kv5t = lambda v: v.reshape(bp, N_KV, 2, HEAD_DIM, -1).transpose(0, 4, 1, 2, 3)
    wb = state_win.shape[1]
    new_win_p = kv5t(wint_p[:, :, -min(WINDOW, t_len):])
    new_win_s = jnp.concatenate([state_win, kv5(win_s)], axis=1)[:, -wb:]
    return (y_p, y_s, kv5t(cmpt_p), kv5t(selt_p), new_win_p, new_h_p, new_conv_p,
            kv5(cmp_s), kv5(sel_s), new_win_s, new_h_s, new_conv_s)
```
